```python
import math
import jax, jax.numpy as jnp
from jax import lax
import numpy as np

D_MODEL = 1024
BATCH = 4
SEQ = 8192
DEPTH = 2

N_A_LAYERS = DEPTH // 2
N_B_LAYERS = DEPTH - N_A_LAYERS
CONV_WIDTH = 3
DIL_GROUPS = ((128, 1), (512, 4), (2048, 16))
N_ATTN_GROUPS = len(DIL_GROUPS)
HEAD_DIM = 128
HEADS_PER_GROUP = D_MODEL // HEAD_DIM
ATTN_GROUP_WIDTH = HEADS_PER_GROUP * HEAD_DIM
Q_WIDTH = N_ATTN_GROUPS * ATTN_GROUP_WIDTH
N_ATTN_HEADS = N_ATTN_GROUPS * HEADS_PER_GROUP
BLOCK = 128
NUM_BUCKETS = 32
MAX_DISTANCE = 2048
N_EXPERTS = 16
N_EXPERT_GROUPS = 4
EXPERTS_PER_GROUP = N_EXPERTS // N_EXPERT_GROUPS
TOP_K = 2
EXPERT_FF = D_MODEL // 4
ALPHA = (2 * DEPTH) ** 0.25
BETA = (8 * DEPTH) ** -0.25
LN_EPS = 1e-5
NEG_INF = -1e30

kernel_name = 'yoco_shortconv_dilated_attn_grouped_moe'


def layer_norm(x, g, b):
    xf = x.astype(jnp.float32)
    mu = jnp.mean(xf, axis=-1, keepdims=True)
    var = jnp.mean(jnp.square(xf - mu), axis=-1, keepdims=True)
    y = (xf - mu) * lax.rsqrt(var + LN_EPS) * g.astype(jnp.float32) + b.astype(jnp.float32)
    return y.astype(x.dtype)


def modulate(x, shift, scale):
    return x * (1 + scale[:, None, :]) + shift[:, None, :]


def short_conv_mixer(h, w_in, conv_w, w_out):
    bgate, cgate, v = jnp.split(h @ w_in, 3, axis=-1)
    u = cgate * v
    u = lax.conv_general_dilated(
        u, conv_w[:, None, :], window_strides=(1,), padding=[(CONV_WIDTH - 1, 0)],
        dimension_numbers=('NWC', 'WIO', 'NWC'), feature_group_count=D_MODEL)
    return (bgate * u) @ w_out


def t5_bucket(n):
    max_exact = NUM_BUCKETS // 2
    nf = jnp.maximum(n, 1).astype(jnp.float32)
    large = max_exact + (jnp.log(nf / max_exact) / math.log(MAX_DISTANCE / max_exact)
                         * (NUM_BUCKETS - max_exact)).astype(jnp.int32)
    large = jnp.minimum(large, NUM_BUCKETS - 1)
    return jnp.where(n < max_exact, n, large)


def dilated_window_attention(q, k, v, window, dilation, bias_table):
    b, s, h, hd = q.shape
    n_keys = window // dilation
    sub_len = s // dilation
    nb = -(-sub_len // BLOCK)
    lp = nb * BLOCK

    def to_sub(t):
        return t.reshape(b, sub_len, dilation, h, hd).transpose(0, 2, 1, 3, 4)

    qb = jnp.pad(to_sub(q), ((0, 0), (0, 0), (0, lp - sub_len), (0, 0), (0, 0)))
    qb = qb.reshape(b, dilation, nb, BLOCK, h, hd)

    def windows(t):
        tp = jnp.pad(to_sub(t), ((0, 0), (0, 0), (BLOCK, lp - sub_len), (0, 0), (0, 0)))
        tp = tp.reshape(b, dilation, nb + 1, BLOCK, h, hd)
        return jnp.concatenate([tp[:, :, :-1], tp[:, :, 1:]], axis=3)

    kw = windows(k)
    vw = windows(v)
    qi = jnp.arange(BLOCK)[:, None]
    kj = jnp.arange(2 * BLOCK)[None, :]
    dist = BLOCK + qi - kj
    bucket = t5_bucket(jnp.maximum(dist, 0) * dilation)
    bias = bias_table[bucket].astype(jnp.float32).transpose(2, 0, 1)
    key_idx = (jnp.arange(nb)[:, None, None] - 1) * BLOCK + kj[None]
    valid = (dist >= 0) & (dist <= n_keys) & (key_idx >= 0)

    scores = jnp.einsum('brnqhd,brnkhd->brhnqk', qb, kw).astype(jnp.float32) * (HEAD_DIM ** -0.5)
    scores = jnp.where(valid, scores + bias[:, None], NEG_INF)
    lse = jax.nn.logsumexp(scores, axis=-1)
    p = jnp.exp(scores - lse[..., None])
    out = jnp.einsum('brhnqk,brnkhd->brnqhd', p.astype(vw.dtype), vw)
    out = out.reshape(b, dilation, lp, h, hd)[:, :, :sub_len]
    out = out.transpose(0, 2, 1, 3, 4).reshape(b, s, h, hd)
    lse = lse.transpose(0, 1, 3, 4, 2).reshape(b, dilation, lp, h)[:, :, :sub_len]
    lse = lse.transpose(0, 2, 1, 3).reshape(b, s, h)
    return out, lse


def shared_kv(x, cond, kv_ada_w, kv_ada_b, w_kv):
    b, s, _ = x.shape
    shift, scale = jnp.split(cond @ kv_ada_w + kv_ada_b, 2, axis=-1)
    kv = (modulate(x, shift, scale) @ w_kv).reshape(b, s, 2, N_ATTN_GROUPS, HEADS_PER_GROUP, HEAD_DIM)
    return kv[:, :, 0], kv[:, :, 1]


def dilated_attention_mixer(h, k_sh, v_sh, w_q, w_o, rel_bias):
    b, s, _ = h.shape
    q = (h @ w_q).reshape(b, s, N_ATTN_GROUPS, HEADS_PER_GROUP, HEAD_DIM)
    outs = []
    lses = []
    for g, (window, dil) in enumerate(DIL_GROUPS):
        o, l = dilated_window_attention(
            q[:, :, g], k_sh[:, :, g], v_sh[:, :, g], window, dil,
            rel_bias[:, g * HEADS_PER_GROUP:(g + 1) * HEADS_PER_GROUP])
        outs.append(o)
        lses.append(l)
    out = jnp.stack(outs, axis=2)
    lse = jnp.stack(lses, axis=2)
    wts = jax.nn.softmax(lse, axis=2)
    mixed = jnp.sum(wts[..., None] * out.astype(jnp.float32), axis=2).astype(h.dtype)
    return mixed.reshape(b, s, ATTN_GROUP_WIDTH) @ w_o


def grouped_moe(h, router_w, router_bias, w_gate, w_up, w_down):
    b, s, _ = h.shape
    aff = jax.nn.sigmoid((h @ router_w).astype(jnp.float32))
    sel = aff + router_bias.astype(jnp.float32)
    group_score = lax.top_k(sel.reshape(b, s, N_EXPERT_GROUPS, EXPERTS_PER_GROUP), TOP_K)[0].sum(-1)
    best_group = jnp.argmax(group_score, axis=-1)
    expert_group = jnp.arange(N_EXPERTS) // EXPERTS_PER_GROUP
    masked = jnp.where(expert_group == best_group[..., None], sel, NEG_INF)
    _, idx = lax.top_k(masked, TOP_K)
    w_sel = jnp.take_along_axis(aff, idx, axis=-1)
    w_sel = w_sel / jnp.sum(w_sel, axis=-1, keepdims=True)
    combine = jnp.sum(jax.nn.one_hot(idx, N_EXPERTS, dtype=jnp.float32) * w_sel[..., None], axis=-2)
    y = jnp.zeros(h.shape, jnp.float32)
    for e in range(N_EXPERTS):
        he = jax.nn.silu(h @ w_gate[e]) * (h @ w_up[e])
        y = y + combine[..., e:e + 1] * (he @ w_down[e]).astype(jnp.float32)
    return y.astype(h.dtype)


def setup_inputs(seed: int = 0) -> dict:
    key = jax.random.key(seed)
    ks = jax.random.split(key, 20)

    def nrm(k, shape, scale):
        return jax.random.normal(k, shape, jnp.float32) * scale

    d = D_MODEL
    conv_col_scale = jnp.concatenate([jnp.ones((2 * d,), jnp.float32), jnp.full((d,), BETA, jnp.float32)])
    kv_col_scale = jnp.concatenate([jnp.ones((Q_WIDTH,), jnp.float32), jnp.full((Q_WIDTH,), BETA, jnp.float32)])
    return {
        'x': nrm(ks[0], (BATCH, SEQ, d), 1.0),
        'c': nrm(ks[1], (BATCH, d), 1.0),
        'ada_w': nrm(ks[2], (DEPTH, d, 6 * d), 0.5 * d ** -0.5),
        'ada_b': nrm(ks[3], (DEPTH, 6 * d), 0.02),
        'ln_g': 1.0 + nrm(ks[4], (DEPTH, 2, d), 0.02),
        'ln_b': nrm(ks[5], (DEPTH, 2, d), 0.02),
        'conv_w_in': nrm(ks[6], (N_A_LAYERS, d, 3 * d), d ** -0.5) * conv_col_scale,
        'conv_w': nrm(ks[7], (N_A_LAYERS, CONV_WIDTH, d), CONV_WIDTH ** -0.5),
        'conv_w_out': nrm(ks[8], (N_A_LAYERS, d, d), BETA * d ** -0.5),
        'kv_ada_w': nrm(ks[9], (d, 2 * d), 0.5 * d ** -0.5),
        'kv_ada_b': nrm(ks[10], (2 * d,), 0.02),
        'w_kv': nrm(ks[11], (d, 2 * Q_WIDTH), d ** -0.5) * kv_col_scale,
        'attn_w_q': nrm(ks[12], (N_B_LAYERS, d, Q_WIDTH), d ** -0.5),
        'attn_w_o': nrm(ks[13], (N_B_LAYERS, ATTN_GROUP_WIDTH, d), BETA * ATTN_GROUP_WIDTH ** -0.5),
        'rel_bias': nrm(ks[14], (NUM_BUCKETS, N_ATTN_HEADS), 0.2),
        'router_w': nrm(ks[15], (d, N_EXPERTS), d ** -0.5),
        'router_bias': nrm(ks[16], (N_EXPERTS,), 0.01),
        'moe_w_gate': nrm(ks[17], (DEPTH, N_EXPERTS, d, EXPERT_FF), d ** -0.5),
        'moe_w_up': nrm(ks[18], (DEPTH, N_EXPERTS, d, EXPERT_FF), BETA * d ** -0.5),
        'moe_w_down': nrm(ks[19], (DEPTH, N_EXPERTS, EXPERT_FF, d), BETA * EXPERT_FF ** -0.5),
    }


def reference(x, c, ada_w, ada_b, ln_g, ln_b, conv_w_in, conv_w, conv_w_out, kv_ada_w, kv_ada_b,
              w_kv, attn_w_q, attn_w_o, rel_bias, router_w, router_bias, moe_w_gate, moe_w_up,
              moe_w_down):
    cond = jax.nn.silu(c)
    k_sh = None
    v_sh = None
    for layer in range(DEPTH):
        sh1, sc1, g1, sh2, sc2, g2 = jnp.split(cond @ ada_w[layer] + ada_b[layer], 6, axis=-1)
        h = modulate(x, sh1, sc1)
        if layer < N_A_LAYERS:
            y = short_conv_mixer(h, conv_w_in[layer], conv_w[layer], conv_w_out[layer])
        else:
            if layer == N_A_LAYERS:
                k_sh, v_sh = shared_kv(x, cond, kv_ada_w, kv_ada_b, w_kv)
            j = layer - N_A_LAYERS
            y = dilated_attention_mixer(h, k_sh, v_sh, attn_w_q[j], attn_w_o[j], rel_bias)
        x = layer_norm(ALPHA * x + (1 + g1[:, None, :]) * y, ln_g[layer, 0], ln_b[layer, 0])
        h = modulate(x, sh2, sc2)
        y = grouped_moe(h, router_w, router_bias, moe_w_gate[layer], moe_w_up[layer], moe_w_down[layer])
        x = layer_norm(ALPHA * x + (1 + g2[:, None, :]) * y, ln_g[layer, 1], ln_b[layer, 1])
    return x
```

```python
import functools
import math

import jax
import jax.numpy as jnp
from jax import lax
from jax.experimental import pallas as pl
from jax.experimental.pallas import tpu as pltpu

F32 = jnp.float32
BF16 = jnp.bfloat16

D = 1024
DEPTH = 2
CONV_WIDTH = 3
DIL_GROUPS = ((128, 1), (512, 4), (2048, 16))
N_GROUPS = len(DIL_GROUPS)
HEAD_DIM = 128
HEADS = D // HEAD_DIM
QW = N_GROUPS * D
BLOCK = 128
NUM_BUCKETS = 32
MAX_DISTANCE = 2048
N_EXPERTS = 16
EXPERTS_PER_GROUP = 4
N_EXPERT_GROUPS = N_EXPERTS // EXPERTS_PER_GROUP
EXPERT_FF = D // 4
ALPHA = (2 * DEPTH) ** 0.25
LN_EPS = 1e-5
NEG_INF = -1e30

LANES = 128
SUBLANES = 8
VMEM_LIMIT = 56 * 1024 * 1024
TOKEN_TILE = 512
ADA_COL_TILE = 2048


def _cparams(n_axes):
    return pltpu.CompilerParams(
        dimension_semantics=("arbitrary",) * n_axes, vmem_limit_bytes=VMEM_LIMIT)


def _layer_norm(r, g, b):
    mu = jnp.mean(r, axis=-1, keepdims=True)
    c = r - mu
    var = jnp.mean(c * c, axis=-1, keepdims=True)
    return c * lax.rsqrt(var + LN_EPS) * g + b


def _bdot(a, b):
    return jnp.dot(a, b, preferred_element_type=F32)


def _ada_kernel(c_ref, w_ref, b_ref, o_ref):
    c = c_ref[...]
    cond = c * jax.nn.sigmoid(c)
    o_ref[...] = jnp.dot(cond, w_ref[...], preferred_element_type=F32,
                         precision=lax.Precision.HIGHEST) + b_ref[...]


def _ada_linear(c8, w, bias):
    n_layers, _, n = w.shape
    nt = ADA_COL_TILE
    return pl.pallas_call(
        _ada_kernel,
        grid=(n_layers, n // nt),
        in_specs=[pl.BlockSpec((SUBLANES, D), lambda l, j: (0, 0)),
                  pl.BlockSpec((None, D, nt), lambda l, j: (l, 0, j)),
                  pl.BlockSpec((None, 1, nt), lambda l, j: (l, 0, j))],
        out_specs=pl.BlockSpec((None, SUBLANES, nt), lambda l, j: (l, 0, j)),
        out_shape=jax.ShapeDtypeStruct((n_layers, SUBLANES, n), F32),
        compiler_params=_cparams(2),
        name="ada_linear",
    )(c8, w, bias.reshape(n_layers, 1, n))


def _mod_rows(mods, batch, n_vec):
    m = mods[:batch].reshape(batch, n_vec, D)
    return jnp.pad(m, ((0, 0), (0, SUBLANES - n_vec), (0, 0)))


def _conv_mixer_kernel(x_ref, mod_ref, w_in_ref, cw_ref, w_out_ref, ln_ref,
                       o_ref, u_scr):
    tm = x_ref.shape[0]
    i = pl.program_id(1)
    mod = mod_ref[...]
    sh1, sc1, g1 = mod[0:1], mod[1:2], mod[2:3]
    x = x_ref[...]
    h = (x * (1.0 + sc1) + sh1).astype(BF16)
    cgate = _bdot(h, w_in_ref[:, D:2 * D])
    v = _bdot(h, w_in_ref[:, 2 * D:3 * D])
    u = cgate * v

    @pl.when(i == 0)
    def _():
        u_scr[0:SUBLANES, :] = jnp.zeros((SUBLANES, D), F32)

    u_scr[SUBLANES:SUBLANES + tm, :] = u
    cw = cw_ref[...]
    conv = (cw[2:3] * u
            + cw[1:2] * u_scr[SUBLANES - 1:SUBLANES - 1 + tm, :]
            + cw[0:1] * u_scr[SUBLANES - 2:SUBLANES - 2 + tm, :])
    u_scr[0:SUBLANES, :] = u_scr[tm:tm + SUBLANES, :]
    bgate = _bdot(h, w_in_ref[:, 0:D])
    y = _bdot((bgate * conv).astype(BF16), w_out_ref[...])
    ln = ln_ref[...]
    o_ref[...] = _layer_norm(ALPHA * x + (1.0 + g1) * y, ln[0:1], ln[1:2])


def _conv_mixer(x, mod, w_in, conv_w, w_out, ln):
    batch, seq, _ = x.shape
    tm = TOKEN_TILE
    const = lambda b, i: (0, 0)
    return pl.pallas_call(
        _conv_mixer_kernel,
        grid=(batch, seq // tm),
        in_specs=[pl.BlockSpec((None, tm, D), lambda b, i: (b, i, 0)),
                  pl.BlockSpec((None, SUBLANES, D), lambda b, i: (b, 0, 0)),
                  pl.BlockSpec((D, 3 * D), const),
                  pl.BlockSpec((SUBLANES, D), const),
                  pl.BlockSpec((D, D), const),
                  pl.BlockSpec((2, D), const)],
        out_specs=pl.BlockSpec((None, tm, D), lambda b, i: (b, i, 0)),
        out_shape=jax.ShapeDtypeStruct(x.shape, F32),
        scratch_shapes=[pltpu.VMEM((tm + SUBLANES, D), F32)],
        compiler_params=_cparams(2),
        name="conv_mixer",
    )(x, mod, w_in, conv_w, w_out, ln)


def _route(logits_t, rbias_col):
    aff = jax.nn.sigmoid(logits_t)
    sel = aff + rbias_col
    rows = [sel[e:e + 1, :] for e in range(N_EXPERTS)]
    scores = []
    for g in range(N_EXPERT_GROUPS):
        s0, s1, s2, s3 = rows[EXPERTS_PER_GROUP * g:EXPERTS_PER_GROUP * (g + 1)]
        a, b = jnp.maximum(s0, s1), jnp.minimum(s0, s1)
        c, d = jnp.maximum(s2, s3), jnp.minimum(s2, s3)
        scores.append(jnp.maximum(a, c) + jnp.maximum(jnp.minimum(a, c), jnp.maximum(b, d)))
    best = scores[0]
    best_group = jnp.zeros(best.shape, jnp.int32)
    for g in range(1, N_EXPERT_GROUPS):
        upd = scores[g] > best
        best_group = jnp.where(upd, g, best_group)
        best = jnp.where(upd, scores[g], best)
    e_iota = lax.broadcasted_iota(jnp.int32, sel.shape, 0)
    masked = jnp.where((e_iota // EXPERTS_PER_GROUP) == best_group, sel, NEG_INF)
    m1 = jnp.max(masked, axis=0, keepdims=True)
    idx1 = jnp.min(jnp.where(masked == m1, e_iota, N_EXPERTS), axis=0, keepdims=True)
    masked2 = jnp.where(e_iota == idx1, -jnp.inf, masked)
    m2 = jnp.max(masked2, axis=0, keepdims=True)
    idx2 = jnp.min(jnp.where(masked2 == m2, e_iota, N_EXPERTS), axis=0, keepdims=True)
    a1 = jnp.sum(jnp.where(e_iota == idx1, aff, 0.0), axis=0, keepdims=True)
    a2 = jnp.sum(jnp.where(e_iota == idx2, aff, 0.0), axis=0, keepdims=True)
    den = a1 + a2
    return (jnp.where(e_iota == idx1, a1 / den, 0.0)
            + jnp.where(e_iota == idx2, a2 / den, 0.0))


def _moe_kernel(x_ref, mod_ref, rw_cat_ref, rw_hi_ref, rb_ref, wgu_ref, wd_ref,
                ln_ref, o_ref, he_scr):
    tm = x_ref.shape[0]
    mod = mod_ref[...]
    sh2, sc2, g2 = mod[3:4], mod[4:5], mod[5:6]
    x = x_ref[...]
    h = x * (1.0 + sc2) + sh2
    hb = h.astype(BF16)
    h_lo = (h - hb.astype(F32)).astype(BF16)
    lg = (_bdot(hb, rw_cat_ref[...]) + _bdot(h_lo, rw_hi_ref[...])).T
    logits_t = lg[0:N_EXPERTS] + lg[N_EXPERTS:2 * N_EXPERTS]
    comb_t = _route(logits_t, rb_ref[...])
    comb = jnp.concatenate(
        [comb_t, jnp.zeros((LANES - N_EXPERTS, tm), F32)], axis=0).T
    for e in range(N_EXPERTS):
        gu = _bdot(hb, wgu_ref[e])
        gate, up = gu[:, :EXPERT_FF], gu[:, EXPERT_FF:]
        he = gate * jax.nn.sigmoid(gate) * up * comb[:, e:e + 1]
        he_scr[:, e * EXPERT_FF:(e + 1) * EXPERT_FF] = he.astype(BF16)
    y = _bdot(he_scr[...], wd_ref[...])
    ln = ln_ref[...]
    o_ref[...] = _layer_norm(ALPHA * x + (1.0 + g2) * y, ln[0:1], ln[1:2])


def _moe(x, mod, rw_cat, rw_hi, rbias, wgu, wd, ln):
    batch, seq, _ = x.shape
    tm = TOKEN_TILE
    const2 = lambda b, i: (0, 0)
    return pl.pallas_call(
        _moe_kernel,
        grid=(batch, seq // tm),
        in_specs=[pl.BlockSpec((None, tm, D), lambda b, i: (b, i, 0)),
                  pl.BlockSpec((None, SUBLANES, D), lambda b, i: (b, 0, 0)),
                  pl.BlockSpec((D, LANES), const2),
                  pl.BlockSpec((D, LANES), const2),
                  pl.BlockSpec((N_EXPERTS, 1), const2),
                  pl.BlockSpec((N_EXPERTS, D, 2 * EXPERT_FF), lambda b, i: (0, 0, 0)),
                  pl.BlockSpec((N_EXPERTS * EXPERT_FF, D), const2),
                  pl.BlockSpec((2, D), const2)],
        out_specs=pl.BlockSpec((None, tm, D), lambda b, i: (b, i, 0)),
        out_shape=jax.ShapeDtypeStruct(x.shape, F32),
        scratch_shapes=[pltpu.VMEM((tm, N_EXPERTS * EXPERT_FF), BF16)],
        compiler_params=_cparams(2),
        name="moe",
    )(x, mod, rw_cat, rw_hi, rbias, wgu, wd, ln)


def _qkv_kernel(x_ref, mod_ref, kvmod_ref, wq_ref, wkv_ref, q_ref, kv_ref):
    mod = mod_ref[...]
    kvmod = kvmod_ref[...]
    x = x_ref[...]
    hq = (x * (1.0 + mod[1:2]) + mod[0:1]).astype(BF16)
    hkv = (x * (1.0 + kvmod[1:2]) + kvmod[0:1]).astype(BF16)
    scale = HEAD_DIM ** -0.5
    for j in range(QW // D):
        cols = slice(j * D, (j + 1) * D)
        q_ref[:, cols] = (_bdot(hq, wq_ref[:, cols]) * scale).astype(BF16)
    for j in range(2 * QW // D):
        cols = slice(j * D, (j + 1) * D)
        kv_ref[:, cols] = _bdot(hkv, wkv_ref[:, cols]).astype(BF16)


def _qkv(x, mod, kvmod, wq, wkv):
    batch, seq, _ = x.shape
    tm = TOKEN_TILE
    const = lambda b, i: (0, 0)
    return pl.pallas_call(
        _qkv_kernel,
        grid=(batch, seq // tm),
        in_specs=[pl.BlockSpec((None, tm, D), lambda b, i: (b, i, 0)),
                  pl.BlockSpec((None, SUBLANES, D), lambda b, i: (b, 0, 0)),
                  pl.BlockSpec((None, SUBLANES, D), lambda b, i: (b, 0, 0)),
                  pl.BlockSpec((D, QW), const),
                  pl.BlockSpec((D, 2 * QW), const)],
        out_specs=[pl.BlockSpec((None, tm, QW), lambda b, i: (b, i, 0)),
                   pl.BlockSpec((None, tm, 2 * QW), lambda b, i: (b, i, 0))],
        out_shape=[jax.ShapeDtypeStruct((batch, seq, QW), BF16),
                   jax.ShapeDtypeStruct((batch, seq, 2 * QW), BF16)],
        compiler_params=_cparams(2),
        name="qkv_proj",
    )(x, mod, kvmod, wq, wkv)


def _bucket_tables():
    qi = jnp.arange(BLOCK)[:, None]
    kj = jnp.arange(2 * BLOCK)[None, :]
    dist = BLOCK + qi - kj
    max_exact = NUM_BUCKETS // 2
    tables = []
    for window, dil in DIL_GROUPS:
        n = jnp.maximum(dist, 0) * dil
        nf = jnp.maximum(n, 1).astype(F32)
        large = max_exact + (jnp.log(nf / max_exact) / math.log(MAX_DISTANCE / max_exact)
                             * (NUM_BUCKETS - max_exact)).astype(jnp.int32)
        large = jnp.minimum(large, NUM_BUCKETS - 1)
        bucket = jnp.where(n < max_exact, n, large)
        valid = (dist >= 0) & (dist <= window // dil)
        tables.append(jnp.where(valid, bucket, -1).astype(jnp.int32))
    return jnp.stack(tables)


def _bias_kernel(rb_ref, bkt_ref, o_ref):
    g = pl.program_id(0)
    h = pl.program_id(1)
    bkt = bkt_ref[...]
    acc = jnp.full(bkt.shape, NEG_INF, F32)
    for b in range(NUM_BUCKETS):
        acc = jnp.where(bkt == b, rb_ref[b, g * HEADS + h], acc)
    o_ref[...] = acc


def _bias_tiles(rel_bias):
    return pl.pallas_call(
        _bias_kernel,
        grid=(N_GROUPS, HEADS),
        in_specs=[pl.BlockSpec(memory_space=pltpu.SMEM),
                  pl.BlockSpec((None, BLOCK, 2 * BLOCK), lambda g, h: (g, 0, 0))],
        out_specs=pl.BlockSpec((None, None, BLOCK, 2 * BLOCK), lambda g, h: (g, h, 0, 0)),
        out_shape=jax.ShapeDtypeStruct((N_GROUPS, HEADS, BLOCK, 2 * BLOCK), F32),
        compiler_params=_cparams(2),
        name="rel_bias_tiles",
    )(rel_bias, _bucket_tables())


def _attn_kernel(q_ref, kp_ref, kc_ref, vp_ref, vc_ref, bias_ref, o_ref, lse_ref):
    n = pl.program_id(2)
    col = lax.broadcasted_iota(jnp.int32, (BLOCK, 2 * BLOCK), 1)
    no_prev = jnp.logical_and(n == 0, col < BLOCK)
    lane = lax.broadcasted_iota(jnp.int32, (BLOCK, LANES), 1)
    lse_tile = jnp.zeros((BLOCK, LANES), F32)
    for h in range(HEADS):
        hs = slice(h * HEAD_DIM, (h + 1) * HEAD_DIM)
        k = jnp.concatenate([kp_ref[:, hs], kc_ref[:, hs]], axis=0)
        v = jnp.concatenate([vp_ref[:, hs], vc_ref[:, hs]], axis=0)
        s = lax.dot_general(q_ref[:, hs], k, (((1,), (1,)), ((), ())),
                            preferred_element_type=F32)
        s = jnp.where(no_prev, NEG_INF, s + bias_ref[h])
        m = jnp.max(s, axis=-1, keepdims=True)
        p = jnp.exp(s - m)
        l = jnp.sum(p, axis=-1, keepdims=True)
        o = _bdot(p.astype(BF16), v)
        o_ref[:, hs] = (o / l).astype(BF16)
        lse_tile = jnp.where(lane == h, m + jnp.log(l), lse_tile)
    lse_ref[...] = lse_tile


def _attention_group(g, q, kv, bias):
    batch, seq, _ = q.shape
    dil = DIL_GROUPS[g][1]
    sub_len = seq // dil
    nb = sub_len // BLOCK
    qr = q.reshape(batch, sub_len, dil * QW)
    kvr = kv.reshape(batch, sub_len, dil * 2 * QW)
    kv_blocks = 2 * N_GROUPS
    prev = lambda n: jnp.maximum(n - 1, 0)
    blk = (None, BLOCK, D)
    o, lse = pl.pallas_call(
        _attn_kernel,
        grid=(batch, dil, nb),
        in_specs=[
            pl.BlockSpec(blk, lambda b, r, n: (b, n, r * N_GROUPS + g)),
            pl.BlockSpec(blk, lambda b, r, n: (b, prev(n), r * kv_blocks + g)),
            pl.BlockSpec(blk, lambda b, r, n: (b, n, r * kv_blocks + g)),
            pl.BlockSpec(blk, lambda b, r, n: (b, prev(n), r * kv_blocks + N_GROUPS + g)),
            pl.BlockSpec(blk, lambda b, r, n: (b, n, r * kv_blocks + N_GROUPS + g)),
            pl.BlockSpec((None, HEADS, BLOCK, 2 * BLOCK), lambda b, r, n: (g, 0, 0, 0)),
        ],
        out_specs=[pl.BlockSpec(blk, lambda b, r, n: (b, n, r)),
                   pl.BlockSpec((None, BLOCK, LANES), lambda b, r, n: (b, n, r))],
        out_shape=[jax.ShapeDtypeStruct((batch, sub_len, dil * D), BF16),
                   jax.ShapeDtypeStruct((batch, sub_len, dil * LANES), F32)],
        compiler_params=_cparams(3),
        name=f"dilated_attn_g{g}",
    )(qr, kvr, kvr, kvr, kvr, bias)
    return o.reshape(batch, seq, D), lse.reshape(batch, seq, LANES)


def _mix_kernel(o0_ref, o1_ref, o2_ref, l0_ref, l1_ref, l2_ref, x_ref, mod_ref,
                expand_ref, wo_ref, ln_ref, out_ref):
    lses = (l0_ref[...], l1_ref[...], l2_ref[...])
    m = jnp.maximum(lses[0], jnp.maximum(lses[1], lses[2]))
    es = [jnp.exp(a - m) for a in lses]
    den = es[0] + es[1] + es[2]
    expand = expand_ref[...]
    mixed = None
    for e_g, o_ref in zip(es, (o0_ref, o1_ref, o2_ref)):
        w = _bdot((e_g / den).astype(BF16), expand)
        t = w * o_ref[...].astype(F32)
        mixed = t if mixed is None else mixed + t
    y = _bdot(mixed.astype(BF16), wo_ref[...])
    g1 = mod_ref[...][2:3]
    ln = ln_ref[...]
    out_ref[...] = _layer_norm(ALPHA * x_ref[...] + (1.0 + g1) * y, ln[0:1], ln[1:2])


def _mix(outs, lses, x, mod, wo, ln):
    batch, seq, _ = x.shape
    tm = TOKEN_TILE
    const = lambda b, i: (0, 0)
    tok = lambda w: pl.BlockSpec((None, tm, w), lambda b, i: (b, i, 0))
    head_of_lane = jnp.arange(D)[None, :] // HEAD_DIM
    expand = (jnp.arange(LANES)[:, None] == head_of_lane).astype(BF16)
    return pl.pallas_call(
        _mix_kernel,
        grid=(batch, seq // tm),
        in_specs=[tok(D), tok(D), tok(D), tok(LANES), tok(LANES), tok(LANES), tok(D),
                  pl.BlockSpec((None, SUBLANES, D), lambda b, i: (b, 0, 0)),
                  pl.BlockSpec((LANES, D), const),
                  pl.BlockSpec((D, D), const),
                  pl.BlockSpec((2, D), const)],
        out_specs=tok(D),
        out_shape=jax.ShapeDtypeStruct(x.shape, F32),
        compiler_params=_cparams(2),
        name="attn_mix",
    )(*outs, *lses, x, mod, expand, wo, ln)


def kernel(x, c, ada_w, ada_b, ln_g, ln_b, conv_w_in, conv_w, conv_w_out, kv_ada_w,
           kv_ada_b, w_kv, attn_w_q, attn_w_o, rel_bias, router_w, router_bias,
           moe_w_gate, moe_w_up, moe_w_down):
    batch, seq, _ = x.shape
    assert x.shape[2] == D and batch <= SUBLANES
    assert seq % (TOKEN_TILE) == 0 and seq % (BLOCK * DIL_GROUPS[-1][1]) == 0

    c8 = jnp.pad(c, ((0, SUBLANES - batch), (0, 0)))
    mods = _ada_linear(c8, ada_w, ada_b)
    kvmods = _ada_linear(c8, kv_ada_w[None], kv_ada_b[None])[0]
    mod0 = _mod_rows(mods[0], batch, 6)
    mod1 = _mod_rows(mods[1], batch, 6)
    kvmod = _mod_rows(kvmods, batch, 2)
    ln = jnp.stack([ln_g, ln_b], axis=2)

    rw_hi = router_w.astype(BF16)
    rw_lo = (router_w - rw_hi.astype(F32)).astype(BF16)
    pad = lambda a: jnp.pad(a, ((0, 0), (0, LANES - a.shape[1])))
    rw_cat = pad(jnp.concatenate([rw_hi, rw_lo], axis=1))
    rw_hi = pad(rw_hi)
    rbias = router_bias.reshape(N_EXPERTS, 1)
    wgu = jnp.concatenate([moe_w_gate, moe_w_up], axis=-1).astype(BF16)
    wd = moe_w_down.astype(BF16).reshape(DEPTH, N_EXPERTS * EXPERT_FF, D)
    conv_w8 = jnp.pad(conv_w[0], ((0, SUBLANES - CONV_WIDTH), (0, 0)))

    x = _conv_mixer(x, mod0, conv_w_in[0].astype(BF16), conv_w8,
                    conv_w_out[0].astype(BF16), ln[0, 0])
    x = _moe(x, mod0, rw_cat, rw_hi, rbias, wgu[0], wd[0], ln[0, 1])

    q, kv = _qkv(x, mod1, kvmod, attn_w_q[0].astype(BF16), w_kv.astype(BF16))
    bias = _bias_tiles(rel_bias)
    outs, lses = zip(*[_attention_group(g, q, kv, bias) for g in range(N_GROUPS)])
    x = _mix(outs, lses, x, mod1, attn_w_o[0].astype(BF16), ln[1, 0])
    x = _moe(x, mod1, rw_cat, rw_hi, rbias, wgu[1], wd[1], ln[1, 1])
    return x
```

```python
import functools
import math

import jax
import jax.numpy as jnp
from jax import lax
from jax.experimental import pallas as pl
from jax.experimental.pallas import tpu as pltpu

F32 = jnp.float32
BF16 = jnp.bfloat16

D = 1024
DEPTH = 2
CONV_WIDTH = 3
DIL_GROUPS = ((128, 1), (512, 4), (2048, 16))
N_GROUPS = len(DIL_GROUPS)
HEAD_DIM = 128
HEADS = D // HEAD_DIM
QW = N_GROUPS * D
BLOCK = 128
NUM_BUCKETS = 32
MAX_DISTANCE = 2048
N_EXPERTS = 16
EXPERTS_PER_GROUP = 4
N_EXPERT_GROUPS = N_EXPERTS // EXPERTS_PER_GROUP
EXPERT_FF = D // 4
ALPHA = (2 * DEPTH) ** 0.25
LN_EPS = 1e-5
NEG_INF = -1e30

LANES = 128
SUBLANES = 8
VMEM_LIMIT = 56 * 1024 * 1024
TOKEN_TILE = 512
ADA_COL_TILE = 2048


def _cparams(n_axes):
    return pltpu.CompilerParams(
        dimension_semantics=("arbitrary",) * n_axes, vmem_limit_bytes=VMEM_LIMIT)


def _layer_norm(r, g, b):
    mu = jnp.mean(r, axis=-1, keepdims=True)
    c = r - mu
    var = jnp.mean(c * c, axis=-1, keepdims=True)
    return c * lax.rsqrt(var + LN_EPS) * g + b


def _bdot(a, b):
    return jnp.dot(a, b, preferred_element_type=F32)


def _ada_kernel(c_ref, w_ref, b_ref, o_ref):
    c = c_ref[...]
    cond = c * jax.nn.sigmoid(c)
    o_ref[...] = jnp.dot(cond, w_ref[...], preferred_element_type=F32,
                         precision=lax.Precision.HIGHEST) + b_ref[...]


def _ada_linear(c8, w, bias):
    n_layers, _, n = w.shape
    nt = ADA_COL_TILE
    return pl.pallas_call(
        _ada_kernel,
        grid=(n_layers, n // nt),
        in_specs=[pl.BlockSpec((SUBLANES, D), lambda l, j: (0, 0)),
                  pl.BlockSpec((None, D, nt), lambda l, j: (l, 0, j)),
                  pl.BlockSpec((None, 1, nt), lambda l, j: (l, 0, j))],
        out_specs=pl.BlockSpec((None, SUBLANES, nt), lambda l, j: (l, 0, j)),
        out_shape=jax.ShapeDtypeStruct((n_layers, SUBLANES, n), F32),
        compiler_params=_cparams(2),
        name="ada_linear",
    )(c8, w, bias.reshape(n_layers, 1, n))


def _mod_rows(mods, batch, n_vec):
    m = mods[:batch].reshape(batch, n_vec, D)
    return jnp.pad(m, ((0, 0), (0, SUBLANES - n_vec), (0, 0)))


def _conv_mixer_kernel(x_ref, mod_ref, w_in_ref, cw_ref, w_out_ref, ln_ref,
                       o_ref, u_scr):
    tm = x_ref.shape[0]
    i = pl.program_id(1)
    mod = mod_ref[...]
    sh1, sc1, g1 = mod[0:1], mod[1:2], mod[2:3]
    x = x_ref[...]
    h = (x * (1.0 + sc1) + sh1).astype(BF16)
    cgate = _bdot(h, w_in_ref[:, D:2 * D])
    v = _bdot(h, w_in_ref[:, 2 * D:3 * D])
    u = cgate * v

    @pl.when(i == 0)
    def _():
        u_scr[0:SUBLANES, :] = jnp.zeros((SUBLANES, D), F32)

    u_scr[SUBLANES:SUBLANES + tm, :] = u
    cw = cw_ref[...]
    conv = (cw[2:3] * u
            + cw[1:2] * u_scr[SUBLANES - 1:SUBLANES - 1 + tm, :]
            + cw[0:1] * u_scr[SUBLANES - 2:SUBLANES - 2 + tm, :])
    u_scr[0:SUBLANES, :] = u_scr[tm:tm + SUBLANES, :]
    bgate = _bdot(h, w_in_ref[:, 0:D])
    y = _bdot((bgate * conv).astype(BF16), w_out_ref[...])
    ln = ln_ref[...]
    o_ref[...] = _layer_norm(ALPHA * x + (1.0 + g1) * y, ln[0:1], ln[1:2])


def _conv_mixer(x, mod, w_in, conv_w, w_out, ln):
    batch, seq, _ = x.shape
    tm = TOKEN_TILE
    const = lambda b, i: (0, 0)
    return pl.pallas_call(
        _conv_mixer_kernel,
        grid=(batch, seq // tm),
        in_specs=[pl.BlockSpec((None, tm, D), lambda b, i: (b, i, 0)),
                  pl.BlockSpec((None, SUBLANES, D), lambda b, i: (b, 0, 0)),
                  pl.BlockSpec((D, 3 * D), const),
                  pl.BlockSpec((SUBLANES, D), const),
                  pl.BlockSpec((D, D), const),
                  pl.BlockSpec((2, D), const)],
        out_specs=pl.BlockSpec((None, tm, D), lambda b, i: (b, i, 0)),
        out_shape=jax.ShapeDtypeStruct(x.shape, F32),
        scratch_shapes=[pltpu.VMEM((tm + SUBLANES, D), F32)],
        compiler_params=_cparams(2),
        name="conv_mixer",
    )(x, mod, w_in, conv_w, w_out, ln)


def _route(logits_t, rbias_col):
    aff = jax.nn.sigmoid(logits_t)
    sel = aff + rbias_col
    rows = [sel[e:e + 1, :] for e in range(N_EXPERTS)]
    scores = []
    for g in range(N_EXPERT_GROUPS):
        s0, s1, s2, s3 = rows[EXPERTS_PER_GROUP * g:EXPERTS_PER_GROUP * (g + 1)]
        a, b = jnp.maximum(s0, s1), jnp.minimum(s0, s1)
        c, d = jnp.maximum(s2, s3), jnp.minimum(s2, s3)
        scores.append(jnp.maximum(a, c) + jnp.maximum(jnp.minimum(a, c), jnp.maximum(b, d)))
    best = scores[0]
    best_group = jnp.zeros(best.shape, jnp.int32)
    for g in range(1, N_EXPERT_GROUPS):
        upd = scores[g] > best
        best_group = jnp.where(upd, g, best_group)
        best = jnp.where(upd, scores[g], best)
    e_iota = lax.broadcasted_iota(jnp.int32, sel.shape, 0)
    masked = jnp.where((e_iota // EXPERTS_PER_GROUP) == best_group, sel, NEG_INF)
    m1 = jnp.max(masked, axis=0, keepdims=True)
    idx1 = jnp.min(jnp.where(masked == m1, e_iota, N_EXPERTS), axis=0, keepdims=True)
    masked2 = jnp.where(e_iota == idx1, -jnp.inf, masked)
    m2 = jnp.max(masked2, axis=0, keepdims=True)
    idx2 = jnp.min(jnp.where(masked2 == m2, e_iota, N_EXPERTS), axis=0, keepdims=True)
    a1 = jnp.sum(jnp.where(e_iota == idx1, aff, 0.0), axis=0, keepdims=True)
    a2 = jnp.sum(jnp.where(e_iota == idx2, aff, 0.0), axis=0, keepdims=True)
    den = a1 + a2
    return (jnp.where(e_iota == idx1, a1 / den, 0.0)
            + jnp.where(e_iota == idx2, a2 / den, 0.0))


def _moe_kernel(x_ref, mod_ref, rw_cat_ref, rw_hi_ref, rb_ref, wgu_ref, wd_ref,
                ln_ref, o_ref, he_scr):
    tm = x_ref.shape[0]
    mod = mod_ref[...]
    sh2, sc2, g2 = mod[3:4], mod[4:5], mod[5:6]
    x = x_ref[...]
    h = x * (1.0 + sc2) + sh2
    hb = h.astype(BF16)
    h_lo = (h - hb.astype(F32)).astype(BF16)
    lg = (_bdot(hb, rw_cat_ref[...]) + _bdot(h_lo, rw_hi_ref[...])).T
    logits_t = lg[0:N_EXPERTS] + lg[N_EXPERTS:2 * N_EXPERTS]
    comb_t = _route(logits_t, rb_ref[...])
    comb = jnp.concatenate(
        [comb_t, jnp.zeros((LANES - N_EXPERTS, tm), F32)], axis=0).T
    for e in range(N_EXPERTS):
        gu = _bdot(hb, wgu_ref[e])
        gate, up = gu[:, :EXPERT_FF], gu[:, EXPERT_FF:]
        he = gate * jax.nn.sigmoid(gate) * up * comb[:, e:e + 1]
        he_scr[:, e * EXPERT_FF:(e + 1) * EXPERT_FF] = he.astype(BF16)
    y = _bdot(he_scr[...], wd_ref[...])
    ln = ln_ref[...]
    o_ref[...] = _layer_norm(ALPHA * x + (1.0 + g2) * y, ln[0:1], ln[1:2])


def _moe(x, mod, rw_cat, rw_hi, rbias, wgu, wd, ln):
    batch, seq, _ = x.shape
    tm = TOKEN_TILE
    const2 = lambda b, i: (0, 0)
    return pl.pallas_call(
        _moe_kernel,
        grid=(batch, seq // tm),
        in_specs=[pl.BlockSpec((None, tm, D), lambda b, i: (b, i, 0)),
                  pl.BlockSpec((None, SUBLANES, D), lambda b, i: (b, 0, 0)),
                  pl.BlockSpec((D, LANES), const2),
                  pl.BlockSpec((D, LANES), const2),
                  pl.BlockSpec((N_EXPERTS, 1), const2),
                  pl.BlockSpec((N_EXPERTS, D, 2 * EXPERT_FF), lambda b, i: (0, 0, 0)),
                  pl.BlockSpec((N_EXPERTS * EXPERT_FF, D), const2),
                  pl.BlockSpec((2, D), const2)],
        out_specs=pl.BlockSpec((None, tm, D), lambda b, i: (b, i, 0)),
        out_shape=jax.ShapeDtypeStruct(x.shape, F32),
        scratch_shapes=[pltpu.VMEM((tm, N_EXPERTS * EXPERT_FF), BF16)],
        compiler_params=_cparams(2),
        name="moe",
    )(x, mod, rw_cat, rw_hi, rbias, wgu, wd, ln)


def _qkv_kernel(x_ref, mod_ref, kvmod_ref, wq_ref, wkv_ref, *refs):
    outs, (slab, hq_scr, hkv_scr) = refs[:3 * N_GROUPS], refs[3 * N_GROUPS:]
    tm = x_ref.shape[0]
    mod = mod_ref[...]
    kvmod = kvmod_ref[...]
    q_sc, q_sh = 1.0 + mod[1:2], mod[0:1]
    kv_sc, kv_sh = 1.0 + kvmod[1:2], kvmod[0:1]
    n_slabs = D // LANES
    for j in range(n_slabs):
        slab[j] = x_ref[:, j * LANES:(j + 1) * LANES]
    scale = HEAD_DIM ** -0.5
    for g, (_, dil) in enumerate(DIL_GROUPS):
        n = tm // dil
        for r in range(dil):
            rows = slice(r * n, (r + 1) * n)
            for j in range(n_slabs):
                ls = slice(j * LANES, (j + 1) * LANES)
                piece = slab[j] if dil == 1 else slab[j, pl.ds(r, n, stride=dil), :]
                hq_scr[rows, ls] = (piece * q_sc[:, ls] + q_sh[:, ls]).astype(BF16)
                hkv_scr[rows, ls] = (piece * kv_sc[:, ls] + kv_sh[:, ls]).astype(BF16)
        hq = hq_scr[...]
        hkv = hkv_scr[...]
        cols = slice(g * D, (g + 1) * D)
        vcols = slice(QW + g * D, QW + (g + 1) * D)
        q = (_bdot(hq, wq_ref[:, cols]) * scale).astype(BF16)
        k = _bdot(hkv, wkv_ref[:, cols]).astype(BF16)
        v = _bdot(hkv, wkv_ref[:, vcols]).astype(BF16)
        for r in range(dil):
            rows = slice(r * n, (r + 1) * n)
            outs[3 * g][r] = q[rows]
            outs[3 * g + 1][r] = k[rows]
            outs[3 * g + 2][r] = v[rows]


def _qkv(x, mod, kvmod, wq, wkv):
    batch, seq, _ = x.shape
    tm = TOKEN_TILE
    const = lambda b, i: (0, 0)
    out_specs, out_shape = [], []
    for _, dil in DIL_GROUPS:
        for _ in range(3):
            out_specs.append(pl.BlockSpec((None, dil, tm // dil, D), lambda b, i: (b, 0, i, 0)))
            out_shape.append(jax.ShapeDtypeStruct((batch, dil, seq // dil, D), BF16))
    return pl.pallas_call(
        _qkv_kernel,
        grid=(batch, seq // tm),
        in_specs=[pl.BlockSpec((None, tm, D), lambda b, i: (b, i, 0)),
                  pl.BlockSpec((None, SUBLANES, D), lambda b, i: (b, 0, 0)),
                  pl.BlockSpec((None, SUBLANES, D), lambda b, i: (b, 0, 0)),
                  pl.BlockSpec((D, QW), const),
                  pl.BlockSpec((D, 2 * QW), const)],
        out_specs=out_specs,
        out_shape=out_shape,
        scratch_shapes=[pltpu.VMEM((D // LANES, tm, LANES), F32),
                        pltpu.VMEM((tm, D), BF16),
                        pltpu.VMEM((tm, D), BF16)],
        compiler_params=_cparams(2),
        name="qkv_proj",
    )(x, mod, kvmod, wq, wkv)


def _bucket_tables():
    qi = jnp.arange(BLOCK)[:, None]
    kj = jnp.arange(2 * BLOCK)[None, :]
    dist = BLOCK + qi - kj
    max_exact = NUM_BUCKETS // 2
    tables = []
    for window, dil in DIL_GROUPS:
        n = jnp.maximum(dist, 0) * dil
        nf = jnp.maximum(n, 1).astype(F32)
        large = max_exact + (jnp.log(nf / max_exact) / math.log(MAX_DISTANCE / max_exact)
                             * (NUM_BUCKETS - max_exact)).astype(jnp.int32)
        large = jnp.minimum(large, NUM_BUCKETS - 1)
        bucket = jnp.where(n < max_exact, n, large)
        valid = (dist >= 0) & (dist <= window // dil)
        tables.append(jnp.where(valid, bucket, -1).astype(jnp.int32))
    return jnp.stack(tables)


def _bias_kernel(rb_ref, bkt_ref, o_ref):
    g = pl.program_id(0)
    h = pl.program_id(1)
    bkt = bkt_ref[...]
    acc = jnp.full(bkt.shape, NEG_INF, F32)
    for b in range(NUM_BUCKETS):
        acc = jnp.where(bkt == b, rb_ref[b, g * HEADS + h], acc)
    o_ref[...] = acc


def _bias_tiles(rel_bias):
    return pl.pallas_call(
        _bias_kernel,
        grid=(N_GROUPS, HEADS),
        in_specs=[pl.BlockSpec(memory_space=pltpu.SMEM),
                  pl.BlockSpec((None, BLOCK, 2 * BLOCK), lambda g, h: (g, 0, 0))],
        out_specs=pl.BlockSpec((None, None, BLOCK, 2 * BLOCK), lambda g, h: (g, h, 0, 0)),
        out_shape=jax.ShapeDtypeStruct((N_GROUPS, HEADS, BLOCK, 2 * BLOCK), F32),
        compiler_params=_cparams(2),
        name="rel_bias_tiles",
    )(rel_bias, _bucket_tables())


def _attn_kernel(q_ref, kp_ref, kc_ref, vp_ref, vc_ref, bias_ref, o_ref, lse_ref):
    n = pl.program_id(2)
    col = lax.broadcasted_iota(jnp.int32, (BLOCK, 2 * BLOCK), 1)
    no_prev = jnp.logical_and(n == 0, col < BLOCK)
    lane = lax.broadcasted_iota(jnp.int32, (BLOCK, LANES), 1)
    lse_tile = jnp.zeros((BLOCK, LANES), F32)
    for h in range(HEADS):
        hs = slice(h * HEAD_DIM, (h + 1) * HEAD_DIM)
        k = jnp.concatenate([kp_ref[:, hs], kc_ref[:, hs]], axis=0)
        v = jnp.concatenate([vp_ref[:, hs], vc_ref[:, hs]], axis=0)
        s = lax.dot_general(q_ref[:, hs], k, (((1,), (1,)), ((), ())),
                            preferred_element_type=F32)
        s = jnp.where(no_prev, NEG_INF, s + bias_ref[h])
        m = jnp.max(s, axis=-1, keepdims=True)
        p = jnp.exp(s - m)
        l = jnp.sum(p, axis=-1, keepdims=True)
        o = _bdot(p.astype(BF16), v)
        o_ref[:, hs] = (o / l).astype(BF16)
        lse_tile = jnp.where(lane == h, m + jnp.log(l), lse_tile)
    lse_ref[...] = lse_tile


def _attention_group(g, q, k, v, bias):
    batch, dil, sub_len, _ = q.shape
    nb = sub_len // BLOCK
    cur = lambda b, r, n: (b, r, n, 0)
    prev = lambda b, r, n: (b, r, jnp.maximum(n - 1, 0), 0)
    blk = (None, None, BLOCK, D)
    return pl.pallas_call(
        _attn_kernel,
        grid=(batch, dil, nb),
        in_specs=[
            pl.BlockSpec(blk, cur),
            pl.BlockSpec(blk, prev),
            pl.BlockSpec(blk, cur),
            pl.BlockSpec(blk, prev),
            pl.BlockSpec(blk, cur),
            pl.BlockSpec((None, HEADS, BLOCK, 2 * BLOCK), lambda b, r, n: (g, 0, 0, 0)),
        ],
        out_specs=[pl.BlockSpec(blk, cur),
                   pl.BlockSpec((None, None, BLOCK, LANES), cur)],
        out_shape=[jax.ShapeDtypeStruct((batch, dil, sub_len, D), BF16),
                   jax.ShapeDtypeStruct((batch, dil, sub_len, LANES), F32)],
        compiler_params=_cparams(3),
        name=f"dilated_attn_g{g}",
    )(q, k, k, v, v, bias)


def _mix_kernel(o0_ref, o1_ref, o2_ref, l0_ref, l1_ref, l2_ref, x_ref, mod_ref,
                expand_ref, wo_ref, ln_ref, out_ref, lse_scr, o_slab):
    tm = x_ref.shape[0]
    n_slabs = D // LANES
    lses = []
    for g, l_ref in enumerate((l0_ref, l1_ref, l2_ref)):
        dil = DIL_GROUPS[g][1]
        if dil == 1:
            lses.append(l_ref[0])
            continue
        for r in range(dil):
            lse_scr[g, pl.ds(r, tm // dil, stride=dil), :] = l_ref[r]
        lses.append(lse_scr[g])
    m = jnp.maximum(lses[0], jnp.maximum(lses[1], lses[2]))
    es = [jnp.exp(a - m) for a in lses]
    den = es[0] + es[1] + es[2]
    expand = expand_ref[...]
    mixed = None
    for g, o_ref in enumerate((o0_ref, o1_ref, o2_ref)):
        dil = DIL_GROUPS[g][1]
        w = _bdot((es[g] / den).astype(BF16), expand)
        if dil == 1:
            o = o_ref[0].astype(F32)
        else:
            for r in range(dil):
                for j in range(n_slabs):
                    o_slab[j, pl.ds(r, tm // dil, stride=dil), :] = (
                        o_ref[r, :, j * LANES:(j + 1) * LANES].astype(F32))
            o = jnp.concatenate([o_slab[j] for j in range(n_slabs)], axis=1)
        mixed = w * o if mixed is None else mixed + w * o
    y = _bdot(mixed.astype(BF16), wo_ref[...])
    g1 = mod_ref[...][2:3]
    ln = ln_ref[...]
    out_ref[...] = _layer_norm(ALPHA * x_ref[...] + (1.0 + g1) * y, ln[0:1], ln[1:2])


def _mix(outs, lses, x, mod, wo, ln):
    batch, seq, _ = x.shape
    tm = TOKEN_TILE
    const = lambda b, i: (0, 0)
    tok = pl.BlockSpec((None, tm, D), lambda b, i: (b, i, 0))
    dilated = lambda dil, w: pl.BlockSpec((None, dil, tm // dil, w), lambda b, i: (b, 0, i, 0))
    dils = [dil for _, dil in DIL_GROUPS]
    head_of_lane = jnp.arange(D)[None, :] // HEAD_DIM
    expand = (jnp.arange(LANES)[:, None] == head_of_lane).astype(BF16)
    return pl.pallas_call(
        _mix_kernel,
        grid=(batch, seq // tm),
        in_specs=[dilated(d, D) for d in dils] + [dilated(d, LANES) for d in dils] + [
            tok,
            pl.BlockSpec((None, SUBLANES, D), lambda b, i: (b, 0, 0)),
            pl.BlockSpec((LANES, D), const),
            pl.BlockSpec((D, D), const),
            pl.BlockSpec((2, D), const)],
        out_specs=tok,
        out_shape=jax.ShapeDtypeStruct(x.shape, F32),
        scratch_shapes=[pltpu.VMEM((N_GROUPS, tm, LANES), F32),
                        pltpu.VMEM((D // LANES, tm, LANES), F32)],
        compiler_params=_cparams(2),
        name="attn_mix",
    )(*outs, *lses, x, mod, expand, wo, ln)


def kernel(x, c, ada_w, ada_b, ln_g, ln_b, conv_w_in, conv_w, conv_w_out, kv_ada_w,
           kv_ada_b, w_kv, attn_w_q, attn_w_o, rel_bias, router_w, router_bias,
           moe_w_gate, moe_w_up, moe_w_down):
    batch, seq, _ = x.shape
    assert x.shape[2] == D and batch <= SUBLANES
    assert seq % (TOKEN_TILE) == 0 and seq % (BLOCK * DIL_GROUPS[-1][1]) == 0

    c8 = jnp.pad(c, ((0, SUBLANES - batch), (0, 0)))
    mods = _ada_linear(c8, ada_w, ada_b)
    kvmods = _ada_linear(c8, kv_ada_w[None], kv_ada_b[None])[0]
    mod0 = _mod_rows(mods[0], batch, 6)
    mod1 = _mod_rows(mods[1], batch, 6)
    kvmod = _mod_rows(kvmods, batch, 2)
    ln = jnp.stack([ln_g, ln_b], axis=2)

    rw_hi = router_w.astype(BF16)
    rw_lo = (router_w - rw_hi.astype(F32)).astype(BF16)
    pad = lambda a: jnp.pad(a, ((0, 0), (0, LANES - a.shape[1])))
    rw_cat = pad(jnp.concatenate([rw_hi, rw_lo], axis=1))
    rw_hi = pad(rw_hi)
    rbias = router_bias.reshape(N_EXPERTS, 1)
    wgu = jnp.concatenate([moe_w_gate, moe_w_up], axis=-1).astype(BF16)
    wd = moe_w_down.astype(BF16).reshape(DEPTH, N_EXPERTS * EXPERT_FF, D)
    conv_w8 = jnp.pad(conv_w[0], ((0, SUBLANES - CONV_WIDTH), (0, 0)))

    x = _conv_mixer(x, mod0, conv_w_in[0].astype(BF16), conv_w8,
                    conv_w_out[0].astype(BF16), ln[0, 0])
    x = _moe(x, mod0, rw_cat, rw_hi, rbias, wgu[0], wd[0], ln[0, 1])

    qkv = _qkv(x, mod1, kvmod, attn_w_q[0].astype(BF16), w_kv.astype(BF16))
    bias = _bias_tiles(rel_bias)
    outs, lses = zip(*[_attention_group(g, *qkv[3 * g:3 * g + 3], bias)
                       for g in range(N_GROUPS)])
    x = _mix(outs, lses, x, mod1, attn_w_o[0].astype(BF16), ln[1, 0])
    x = _moe(x, mod1, rw_cat, rw_hi, rbias, wgu[1], wd[1], ln[1, 1])
    return x
```

```python
import functools
import math

import jax
import jax.numpy as jnp
from jax import lax
from jax.experimental import pallas as pl
from jax.experimental.pallas import tpu as pltpu

F32 = jnp.float32
BF16 = jnp.bfloat16

D = 1024
DEPTH = 2
CONV_WIDTH = 3
DIL_GROUPS = ((128, 1), (512, 4), (2048, 16))
N_GROUPS = len(DIL_GROUPS)
HEAD_DIM = 128
HEADS = D // HEAD_DIM
QW = N_GROUPS * D
BLOCK = 128
NUM_BUCKETS = 32
MAX_DISTANCE = 2048
N_EXPERTS = 16
EXPERTS_PER_GROUP = 4
N_EXPERT_GROUPS = N_EXPERTS // EXPERTS_PER_GROUP
EXPERT_FF = D // 4
ALPHA = (2 * DEPTH) ** 0.25
LN_EPS = 1e-5
NEG_INF = -1e30

LANES = 128
SUBLANES = 8
VMEM_LIMIT = 56 * 1024 * 1024
TOKEN_TILE = 512
MOE_SUB = 128
ADA_COL_TILE = 2048


def _cparams(n_axes):
    return pltpu.CompilerParams(
        dimension_semantics=("arbitrary",) * n_axes, vmem_limit_bytes=VMEM_LIMIT)


def _layer_norm(r, g, b):
    mu = jnp.mean(r, axis=-1, keepdims=True)
    c = r - mu
    var = jnp.mean(c * c, axis=-1, keepdims=True)
    return c * lax.rsqrt(var + LN_EPS) * g + b


def _bdot(a, b):
    return jnp.dot(a, b, preferred_element_type=F32)


def _ada_kernel(c_ref, w_ref, b_ref, o_ref):
    c = c_ref[...]
    cond = c * jax.nn.sigmoid(c)
    o_ref[...] = jnp.dot(cond, w_ref[...], preferred_element_type=F32,
                         precision=lax.Precision.HIGHEST) + b_ref[...]


def _ada_linear(c8, w, bias):
    n_layers, _, n = w.shape
    nt = ADA_COL_TILE
    return pl.pallas_call(
        _ada_kernel,
        grid=(n_layers, n // nt),
        in_specs=[pl.BlockSpec((SUBLANES, D), lambda l, j: (0, 0)),
                  pl.BlockSpec((None, D, nt), lambda l, j: (l, 0, j)),
                  pl.BlockSpec((None, 1, nt), lambda l, j: (l, 0, j))],
        out_specs=pl.BlockSpec((None, SUBLANES, nt), lambda l, j: (l, 0, j)),
        out_shape=jax.ShapeDtypeStruct((n_layers, SUBLANES, n), F32),
        compiler_params=_cparams(2),
        name="ada_linear",
    )(c8, w, bias.reshape(n_layers, 1, n))


def _mod_rows(mods, batch, n_vec):
    m = mods[:batch].reshape(batch, n_vec, D)
    return jnp.pad(m, ((0, 0), (0, SUBLANES - n_vec), (0, 0)))


def _conv_mixer_kernel(x_ref, mod_ref, w_in_ref, cw_ref, w_out_ref, ln_ref,
                       o_ref, u_scr):
    tm = x_ref.shape[0]
    i = pl.program_id(1)
    mod = mod_ref[...]
    sh1, sc1, g1 = mod[0:1], mod[1:2], mod[2:3]
    x = x_ref[...]
    h = (x * (1.0 + sc1) + sh1).astype(BF16)
    cgate = _bdot(h, w_in_ref[:, D:2 * D])
    v = _bdot(h, w_in_ref[:, 2 * D:3 * D])
    u = cgate * v

    @pl.when(i == 0)
    def _():
        u_scr[0:SUBLANES, :] = jnp.zeros((SUBLANES, D), F32)

    u_scr[SUBLANES:SUBLANES + tm, :] = u
    cw = cw_ref[...]
    conv = (cw[2:3] * u
            + cw[1:2] * u_scr[SUBLANES - 1:SUBLANES - 1 + tm, :]
            + cw[0:1] * u_scr[SUBLANES - 2:SUBLANES - 2 + tm, :])
    u_scr[0:SUBLANES, :] = u_scr[tm:tm + SUBLANES, :]
    bgate = _bdot(h, w_in_ref[:, 0:D])
    y = _bdot((bgate * conv).astype(BF16), w_out_ref[...])
    ln = ln_ref[...]
    o_ref[...] = _layer_norm(ALPHA * x + (1.0 + g1) * y, ln[0:1], ln[1:2])


def _conv_mixer(x, mod, w_in, conv_w, w_out, ln):
    batch, seq, _ = x.shape
    tm = TOKEN_TILE
    const = lambda b, i: (0, 0)
    return pl.pallas_call(
        _conv_mixer_kernel,
        grid=(batch, seq // tm),
        in_specs=[pl.BlockSpec((None, tm, D), lambda b, i: (b, i, 0)),
                  pl.BlockSpec((None, SUBLANES, D), lambda b, i: (b, 0, 0)),
                  pl.BlockSpec((D, 3 * D), const),
                  pl.BlockSpec((SUBLANES, D), const),
                  pl.BlockSpec((D, D), const),
                  pl.BlockSpec((2, D), const)],
        out_specs=pl.BlockSpec((None, tm, D), lambda b, i: (b, i, 0)),
        out_shape=jax.ShapeDtypeStruct(x.shape, F32),
        scratch_shapes=[pltpu.VMEM((tm + SUBLANES, D), F32)],
        compiler_params=_cparams(2),
        name="conv_mixer",
    )(x, mod, w_in, conv_w, w_out, ln)


def _route(logits_t, rbias_col):
    aff = jax.nn.sigmoid(logits_t)
    sel = aff + rbias_col
    rows = [sel[e:e + 1, :] for e in range(N_EXPERTS)]
    scores = []
    for g in range(N_EXPERT_GROUPS):
        s0, s1, s2, s3 = rows[EXPERTS_PER_GROUP * g:EXPERTS_PER_GROUP * (g + 1)]
        a, b = jnp.maximum(s0, s1), jnp.minimum(s0, s1)
        c, d = jnp.maximum(s2, s3), jnp.minimum(s2, s3)
        scores.append(jnp.maximum(a, c) + jnp.maximum(jnp.minimum(a, c), jnp.maximum(b, d)))
    best = scores[0]
    best_group = jnp.zeros(best.shape, jnp.int32)
    for g in range(1, N_EXPERT_GROUPS):
        upd = scores[g] > best
        best_group = jnp.where(upd, g, best_group)
        best = jnp.where(upd, scores[g], best)
    e_iota = lax.broadcasted_iota(jnp.int32, sel.shape, 0)
    masked = jnp.where((e_iota // EXPERTS_PER_GROUP) == best_group, sel, NEG_INF)
    m1 = jnp.max(masked, axis=0, keepdims=True)
    idx1 = jnp.min(jnp.where(masked == m1, e_iota, N_EXPERTS), axis=0, keepdims=True)
    masked2 = jnp.where(e_iota == idx1, -jnp.inf, masked)
    m2 = jnp.max(masked2, axis=0, keepdims=True)
    idx2 = jnp.min(jnp.where(masked2 == m2, e_iota, N_EXPERTS), axis=0, keepdims=True)
    a1 = jnp.sum(jnp.where(e_iota == idx1, aff, 0.0), axis=0, keepdims=True)
    a2 = jnp.sum(jnp.where(e_iota == idx2, aff, 0.0), axis=0, keepdims=True)
    den = a1 + a2
    comb = (jnp.where(e_iota == idx1, a1 / den, 0.0)
            + jnp.where(e_iota == idx2, a2 / den, 0.0))
    group1 = idx1 // EXPERTS_PER_GROUP
    split = group1 != idx2 // EXPERTS_PER_GROUP
    return comb, group1, split


def _moe_kernel(x_ref, mod_ref, rw_cat_ref, rw_hi_ref, rb_ref, tri_ref, wgu_ref, wd_ref,
                ln_ref, o_ref, hs_scr, cs_scr, ys_scr):
    tm = x_ref.shape[0]
    gff = EXPERTS_PER_GROUP * EXPERT_FF
    mod = mod_ref[...]
    sh2, sc2, g2 = mod[3:4], mod[4:5], mod[5:6]
    x = x_ref[...]
    h = x * (1.0 + sc2) + sh2
    hb = h.astype(BF16)
    h_lo = (h - hb.astype(F32)).astype(BF16)
    nt = (((1,), (1,)), ((), ()))
    lg = lax.dot_general(rw_cat_ref[...], hb, nt, preferred_element_type=F32)
    logits_t = (lg[0:N_EXPERTS] + lg[N_EXPERTS:2 * N_EXPERTS]
                + lax.dot_general(rw_hi_ref[...], h_lo, nt, preferred_element_type=F32))
    comb_t, group, split = _route(logits_t, rb_ref[...])

    g_iota = lax.broadcasted_iota(jnp.int32, (SUBLANES, tm), 0)
    member = jnp.where(g_iota == group, 1.0, 0.0)
    rank = _bdot(member.astype(BF16), tri_ref[...])
    counts = jnp.sum(member, axis=1, keepdims=True)
    cnt = [counts[g, 0] for g in range(N_EXPERT_GROUPS)]
    start = [0.0]
    for g in range(N_EXPERT_GROUPS - 1):
        start.append(start[g] + cnt[g])
    dest = sum(member[g:g + 1] * (start[g] + rank[g:g + 1]) for g in range(N_EXPERT_GROUPS))
    row = lax.broadcasted_iota(jnp.int32, (tm, tm), 0).astype(F32)
    col = lax.broadcasted_iota(jnp.int32, (tm, tm), 1).astype(F32)
    perm = jnp.where(row == dest, 1.0, 0.0).astype(BF16)
    dest_col = jnp.broadcast_to(dest, (LANES, tm)).T
    dest_col = jnp.concatenate([dest_col] * (tm // LANES), axis=1)
    perm_t = jnp.where(dest_col == col, 1.0, 0.0).astype(BF16)

    hs_scr[...] = _bdot(perm, hb).astype(BF16)
    comb_hi = comb_t.astype(BF16)
    comb_lo = (comb_t - comb_hi.astype(F32)).astype(BF16)
    cs = _bdot(jnp.concatenate([comb_hi, comb_lo], axis=0), perm_t)
    cs = cs[0:N_EXPERTS] + cs[N_EXPERTS:2 * N_EXPERTS]
    cs_scr[...] = jnp.concatenate(
        [cs, jnp.zeros((LANES - N_EXPERTS, tm), F32)], axis=0).T
    ys_scr[...] = jnp.zeros(ys_scr.shape, F32)

    any_split = jnp.max(jnp.where(split, 1.0, 0.0)) > 0.0
    for k in range(tm // MOE_SUB):
        rows = slice(k * MOE_SUB, (k + 1) * MOE_SUB)
        for g in range(N_EXPERT_GROUPS):
            overlaps = jnp.logical_and(start[g] < (k + 1) * MOE_SUB,
                                       start[g] + cnt[g] > k * MOE_SUB)
            present = jnp.logical_and(overlaps, cnt[g] > 0.0)

            @pl.when(jnp.logical_or(present, any_split))
            def _(rows=rows, g=g):
                hs = hs_scr[rows, :]
                parts = []
                for j in range(EXPERTS_PER_GROUP):
                    e = g * EXPERTS_PER_GROUP + j
                    gu = _bdot(hs, wgu_ref[e])
                    gate, up = gu[:, :EXPERT_FF], gu[:, EXPERT_FF:]
                    he = gate * jax.nn.sigmoid(gate) * up * cs_scr[rows, e:e + 1]
                    parts.append(he.astype(BF16))
                he = jnp.concatenate(parts, axis=1)
                ys_scr[rows, :] += _bdot(he, wd_ref[g * gff:(g + 1) * gff, :])

    y = _bdot(perm_t, ys_scr[...].astype(BF16))
    ln = ln_ref[...]
    o_ref[...] = _layer_norm(ALPHA * x + (1.0 + g2) * y, ln[0:1], ln[1:2])


def _moe(x, mod, rw_cat, rw_hi, rbias, wgu, wd, ln):
    batch, seq, _ = x.shape
    tm = TOKEN_TILE
    const2 = lambda b, i: (0, 0)
    tri = (jnp.arange(tm)[:, None] < jnp.arange(tm)[None, :]).astype(BF16)
    return pl.pallas_call(
        _moe_kernel,
        grid=(batch, seq // tm),
        in_specs=[pl.BlockSpec((None, tm, D), lambda b, i: (b, i, 0)),
                  pl.BlockSpec((None, SUBLANES, D), lambda b, i: (b, 0, 0)),
                  pl.BlockSpec((2 * N_EXPERTS, D), const2),
                  pl.BlockSpec((N_EXPERTS, D), const2),
                  pl.BlockSpec((N_EXPERTS, 1), const2),
                  pl.BlockSpec((tm, tm), const2),
                  pl.BlockSpec((N_EXPERTS, D, 2 * EXPERT_FF), lambda b, i: (0, 0, 0)),
                  pl.BlockSpec((N_EXPERTS * EXPERT_FF, D), const2),
                  pl.BlockSpec((2, D), const2)],
        out_specs=pl.BlockSpec((None, tm, D), lambda b, i: (b, i, 0)),
        out_shape=jax.ShapeDtypeStruct(x.shape, F32),
        scratch_shapes=[pltpu.VMEM((tm, D), BF16),
                        pltpu.VMEM((tm, LANES), F32),
                        pltpu.VMEM((tm, D), F32)],
        compiler_params=_cparams(2),
        name="moe",
    )(x, mod, rw_cat, rw_hi, rbias, tri, wgu, wd, ln)


def _qkv_kernel(x_ref, mod_ref, kvmod_ref, wq_ref, wkv_ref, *refs):
    outs, (slab, hq_scr, hkv_scr) = refs[:3 * N_GROUPS], refs[3 * N_GROUPS:]
    tm = x_ref.shape[0]
    mod = mod_ref[...]
    kvmod = kvmod_ref[...]
    q_sc, q_sh = 1.0 + mod[1:2], mod[0:1]
    kv_sc, kv_sh = 1.0 + kvmod[1:2], kvmod[0:1]
    n_slabs = D // LANES
    for j in range(n_slabs):
        slab[j] = x_ref[:, j * LANES:(j + 1) * LANES]
    scale = HEAD_DIM ** -0.5
    for g, (_, dil) in enumerate(DIL_GROUPS):
        n = tm // dil
        for r in range(dil):
            rows = slice(r * n, (r + 1) * n)
            for j in range(n_slabs):
                ls = slice(j * LANES, (j + 1) * LANES)
                piece = slab[j] if dil == 1 else slab[j, pl.ds(r, n, stride=dil), :]
                hq_scr[rows, ls] = (piece * q_sc[:, ls] + q_sh[:, ls]).astype(BF16)
                hkv_scr[rows, ls] = (piece * kv_sc[:, ls] + kv_sh[:, ls]).astype(BF16)
        hq = hq_scr[...]
        hkv = hkv_scr[...]
        cols = slice(g * D, (g + 1) * D)
        vcols = slice(QW + g * D, QW + (g + 1) * D)
        q = (_bdot(hq, wq_ref[:, cols]) * scale).astype(BF16)
        k = _bdot(hkv, wkv_ref[:, cols]).astype(BF16)
        v = _bdot(hkv, wkv_ref[:, vcols]).astype(BF16)
        for r in range(dil):
            rows = slice(r * n, (r + 1) * n)
            outs[3 * g][r] = q[rows]
            outs[3 * g + 1][r] = k[rows]
            outs[3 * g + 2][r] = v[rows]


def _qkv(x, mod, kvmod, wq, wkv):
    batch, seq, _ = x.shape
    tm = TOKEN_TILE
    const = lambda b, i: (0, 0)
    out_specs, out_shape = [], []
    for _, dil in DIL_GROUPS:
        for _ in range(3):
            out_specs.append(pl.BlockSpec((None, dil, tm // dil, D), lambda b, i: (b, 0, i, 0)))
            out_shape.append(jax.ShapeDtypeStruct((batch, dil, seq // dil, D), BF16))
    return pl.pallas_call(
        _qkv_kernel,
        grid=(batch, seq // tm),
        in_specs=[pl.BlockSpec((None, tm, D), lambda b, i: (b, i, 0)),
                  pl.BlockSpec((None, SUBLANES, D), lambda b, i: (b, 0, 0)),
                  pl.BlockSpec((None, SUBLANES, D), lambda b, i: (b, 0, 0)),
                  pl.BlockSpec((D, QW), const),
                  pl.BlockSpec((D, 2 * QW), const)],
        out_specs=out_specs,
        out_shape=out_shape,
        scratch_shapes=[pltpu.VMEM((D // LANES, tm, LANES), F32),
                        pltpu.VMEM((tm, D), BF16),
                        pltpu.VMEM((tm, D), BF16)],
        compiler_params=_cparams(2),
        name="qkv_proj",
    )(x, mod, kvmod, wq, wkv)


def _bucket_tables():
    qi = jnp.arange(BLOCK)[:, None]
    kj = jnp.arange(2 * BLOCK)[None, :]
    dist = BLOCK + qi - kj
    max_exact = NUM_BUCKETS // 2
    tables = []
    for window, dil in DIL_GROUPS:
        n = jnp.maximum(dist, 0) * dil
        nf = jnp.maximum(n, 1).astype(F32)
        large = max_exact + (jnp.log(nf / max_exact) / math.log(MAX_DISTANCE / max_exact)
                             * (NUM_BUCKETS - max_exact)).astype(jnp.int32)
        large = jnp.minimum(large, NUM_BUCKETS - 1)
        bucket = jnp.where(n < max_exact, n, large)
        valid = (dist >= 0) & (dist <= window // dil)
        tables.append(jnp.where(valid, bucket, -1).astype(jnp.int32))
    return jnp.stack(tables)


def _bias_kernel(rb_ref, bkt_ref, o_ref):
    g = pl.program_id(0)
    h = pl.program_id(1)
    bkt = bkt_ref[...]
    acc = jnp.full(bkt.shape, NEG_INF, F32)
    for b in range(NUM_BUCKETS):
        acc = jnp.where(bkt == b, rb_ref[b, g * HEADS + h], acc)
    o_ref[...] = acc


def _bias_tiles(rel_bias):
    return pl.pallas_call(
        _bias_kernel,
        grid=(N_GROUPS, HEADS),
        in_specs=[pl.BlockSpec(memory_space=pltpu.SMEM),
                  pl.BlockSpec((None, BLOCK, 2 * BLOCK), lambda g, h: (g, 0, 0))],
        out_specs=pl.BlockSpec((None, None, BLOCK, 2 * BLOCK), lambda g, h: (g, h, 0, 0)),
        out_shape=jax.ShapeDtypeStruct((N_GROUPS, HEADS, BLOCK, 2 * BLOCK), F32),
        compiler_params=_cparams(2),
        name="rel_bias_tiles",
    )(rel_bias, _bucket_tables())


def _attn_kernel(q_ref, kp_ref, kc_ref, vp_ref, vc_ref, bias_ref, o_ref, lse_ref):
    n = pl.program_id(2)
    col = lax.broadcasted_iota(jnp.int32, (BLOCK, 2 * BLOCK), 1)
    no_prev = jnp.logical_and(n == 0, col < BLOCK)
    lane = lax.broadcasted_iota(jnp.int32, (BLOCK, LANES), 1)
    lse_tile = jnp.zeros((BLOCK, LANES), F32)
    for h in range(HEADS):
        hs = slice(h * HEAD_DIM, (h + 1) * HEAD_DIM)
        k = jnp.concatenate([kp_ref[:, hs], kc_ref[:, hs]], axis=0)
        v = jnp.concatenate([vp_ref[:, hs], vc_ref[:, hs]], axis=0)
        s = lax.dot_general(q_ref[:, hs], k, (((1,), (1,)), ((), ())),
                            preferred_element_type=F32)
        s = jnp.where(no_prev, NEG_INF, s + bias_ref[h])
        m = jnp.max(s, axis=-1, keepdims=True)
        p = jnp.exp(s - m)
        l = jnp.sum(p, axis=-1, keepdims=True)
        o = _bdot(p.astype(BF16), v)
        o_ref[:, hs] = (o / l).astype(BF16)
        lse_tile = jnp.where(lane == h, m + jnp.log(l), lse_tile)
    lse_ref[...] = lse_tile


def _attention_group(g, q, k, v, bias):
    batch, dil, sub_len, _ = q.shape
    nb = sub_len // BLOCK
    cur = lambda b, r, n: (b, r, n, 0)
    prev = lambda b, r, n: (b, r, jnp.maximum(n - 1, 0), 0)
    blk = (None, None, BLOCK, D)
    return pl.pallas_call(
        _attn_kernel,
        grid=(batch, dil, nb),
        in_specs=[
            pl.BlockSpec(blk, cur),
            pl.BlockSpec(blk, prev),
            pl.BlockSpec(blk, cur),
            pl.BlockSpec(blk, prev),
            pl.BlockSpec(blk, cur),
            pl.BlockSpec((None, HEADS, BLOCK, 2 * BLOCK), lambda b, r, n: (g, 0, 0, 0)),
        ],
        out_specs=[pl.BlockSpec(blk, cur),
                   pl.BlockSpec((None, None, BLOCK, LANES), cur)],
        out_shape=[jax.ShapeDtypeStruct((batch, dil, sub_len, D), BF16),
                   jax.ShapeDtypeStruct((batch, dil, sub_len, LANES), F32)],
        compiler_params=_cparams(3),
        name=f"dilated_attn_g{g}",
    )(q, k, k, v, v, bias)


def _mix_kernel(o0_ref, o1_ref, o2_ref, l0_ref, l1_ref, l2_ref, x_ref, mod_ref,
                expand_ref, wo_ref, ln_ref, out_ref, lse_scr, o_slab):
    tm = x_ref.shape[0]
    n_slabs = D // LANES
    lses = []
    for g, l_ref in enumerate((l0_ref, l1_ref, l2_ref)):
        dil = DIL_GROUPS[g][1]
        if dil == 1:
            lses.append(l_ref[0])
            continue
        for r in range(dil):
            lse_scr[g, pl.ds(r, tm // dil, stride=dil), :] = l_ref[r]
        lses.append(lse_scr[g])
    m = jnp.maximum(lses[0], jnp.maximum(lses[1], lses[2]))
    es = [jnp.exp(a - m) for a in lses]
    den = es[0] + es[1] + es[2]
    expand = expand_ref[...]
    mixed = None
    for g, o_ref in enumerate((o0_ref, o1_ref, o2_ref)):
        dil = DIL_GROUPS[g][1]
        w = _bdot((es[g] / den).astype(BF16), expand)
        if dil == 1:
            o = o_ref[0].astype(F32)
        else:
            for r in range(dil):
                for j in range(n_slabs):
                    o_slab[j, pl.ds(r, tm // dil, stride=dil), :] = (
                        o_ref[r, :, j * LANES:(j + 1) * LANES].astype(F32))
            o = jnp.concatenate([o_slab[j] for j in range(n_slabs)], axis=1)
        mixed = w * o if mixed is None else mixed + w * o
    y = _bdot(mixed.astype(BF16), wo_ref[...])
    g1 = mod_ref[...][2:3]
    ln = ln_ref[...]
    out_ref[...] = _layer_norm(ALPHA * x_ref[...] + (1.0 + g1) * y, ln[0:1], ln[1:2])


def _mix(outs, lses, x, mod, wo, ln):
    batch, seq, _ = x.shape
    tm = TOKEN_TILE
    const = lambda b, i: (0, 0)
    tok = pl.BlockSpec((None, tm, D), lambda b, i: (b, i, 0))
    dilated = lambda dil, w: pl.BlockSpec((None, dil, tm // dil, w), lambda b, i: (b, 0, i, 0))
    dils = [dil for _, dil in DIL_GROUPS]
    head_of_lane = jnp.arange(D)[None, :] // HEAD_DIM
    expand = (jnp.arange(LANES)[:, None] == head_of_lane).astype(BF16)
    return pl.pallas_call(
        _mix_kernel,
        grid=(batch, seq // tm),
        in_specs=[dilated(d, D) for d in dils] + [dilated(d, LANES) for d in dils] + [
            tok,
            pl.BlockSpec((None, SUBLANES, D), lambda b, i: (b, 0, 0)),
            pl.BlockSpec((LANES, D), const),
            pl.BlockSpec((D, D), const),
            pl.BlockSpec((2, D), const)],
        out_specs=tok,
        out_shape=jax.ShapeDtypeStruct(x.shape, F32),
        scratch_shapes=[pltpu.VMEM((N_GROUPS, tm, LANES), F32),
                        pltpu.VMEM((D // LANES, tm, LANES), F32)],
        compiler_params=_cparams(2),
        name="attn_mix",
    )(*outs, *lses, x, mod, expand, wo, ln)


def kernel(x, c, ada_w, ada_b, ln_g, ln_b, conv_w_in, conv_w, conv_w_out, kv_ada_w,
           kv_ada_b, w_kv, attn_w_q, attn_w_o, rel_bias, router_w, router_bias,
           moe_w_gate, moe_w_up, moe_w_down):
    batch, seq, _ = x.shape
    assert x.shape[2] == D and batch <= SUBLANES
    assert seq % (TOKEN_TILE) == 0 and seq % (BLOCK * DIL_GROUPS[-1][1]) == 0

    c8 = jnp.pad(c, ((0, SUBLANES - batch), (0, 0)))
    mods = _ada_linear(c8, ada_w, ada_b)
    kvmods = _ada_linear(c8, kv_ada_w[None], kv_ada_b[None])[0]
    mod0 = _mod_rows(mods[0], batch, 6)
    mod1 = _mod_rows(mods[1], batch, 6)
    kvmod = _mod_rows(kvmods, batch, 2)
    ln = jnp.stack([ln_g, ln_b], axis=2)

    rw_hi = router_w.T.astype(BF16)
    rw_lo = (router_w.T - rw_hi.astype(F32)).astype(BF16)
    rw_cat = jnp.concatenate([rw_hi, rw_lo], axis=0)
    rbias = router_bias.reshape(N_EXPERTS, 1)
    wgu = jnp.concatenate([moe_w_gate, moe_w_up], axis=-1).astype(BF16)
    wd = moe_w_down.astype(BF16).reshape(DEPTH, N_EXPERTS * EXPERT_FF, D)
    conv_w8 = jnp.pad(conv_w[0], ((0, SUBLANES - CONV_WIDTH), (0, 0)))

    x = _conv_mixer(x, mod0, conv_w_in[0].astype(BF16), conv_w8,
                    conv_w_out[0].astype(BF16), ln[0, 0])
    x = _moe(x, mod0, rw_cat, rw_hi, rbias, wgu[0], wd[0], ln[0, 1])

    qkv = _qkv(x, mod1, kvmod, attn_w_q[0].astype(BF16), w_kv.astype(BF16))
    bias = _bias_tiles(rel_bias)
    outs, lses = zip(*[_attention_group(g, *qkv[3 * g:3 * g + 3], bias)
                       for g in range(N_GROUPS)])
    x = _mix(outs, lses, x, mod1, attn_w_o[0].astype(BF16), ln[1, 0])
    x = _moe(x, mod1, rw_cat, rw_hi, rbias, wgu[1], wd[1], ln[1, 1])
    return x
```

```python
import functools
import math

import jax
import jax.numpy as jnp
from jax import lax
from jax.experimental import pallas as pl
from jax.experimental.pallas import tpu as pltpu

F32 = jnp.float32
BF16 = jnp.bfloat16

D = 1024
DEPTH = 2
CONV_WIDTH = 3
DIL_GROUPS = ((128, 1), (512, 4), (2048, 16))
N_GROUPS = len(DIL_GROUPS)
HEAD_DIM = 128
HEADS = D // HEAD_DIM
QW = N_GROUPS * D
BLOCK = 128
NUM_BUCKETS = 32
MAX_DISTANCE = 2048
N_EXPERTS = 16
EXPERTS_PER_GROUP = 4
N_EXPERT_GROUPS = N_EXPERTS // EXPERTS_PER_GROUP
EXPERT_FF = D // 4
ALPHA = (2 * DEPTH) ** 0.25
LN_EPS = 1e-5
NEG_INF = -1e30

LANES = 128
SUBLANES = 8
VMEM_LIMIT = 56 * 1024 * 1024
TOKEN_TILE = 512
MOE_SUB = 128
ATTN_QBLOCKS = 4
ADA_COL_TILE = 2048


def _cparams(n_axes):
    return pltpu.CompilerParams(
        dimension_semantics=("arbitrary",) * n_axes, vmem_limit_bytes=VMEM_LIMIT)


def _layer_norm(r, g, b):
    mu = jnp.mean(r, axis=-1, keepdims=True)
    c = r - mu
    var = jnp.mean(c * c, axis=-1, keepdims=True)
    return c * lax.rsqrt(var + LN_EPS) * g + b


def _bdot(a, b):
    return jnp.dot(a, b, preferred_element_type=F32)


def _ada_kernel(c_ref, w_ref, b_ref, o_ref):
    c = c_ref[...]
    cond = c * jax.nn.sigmoid(c)
    o_ref[...] = jnp.dot(cond, w_ref[...], preferred_element_type=F32,
                         precision=lax.Precision.HIGHEST) + b_ref[...]


def _ada_linear(c8, w, bias):
    n_layers, _, n = w.shape
    nt = ADA_COL_TILE
    return pl.pallas_call(
        _ada_kernel,
        grid=(n_layers, n // nt),
        in_specs=[pl.BlockSpec((SUBLANES, D), lambda l, j: (0, 0)),
                  pl.BlockSpec((None, D, nt), lambda l, j: (l, 0, j)),
                  pl.BlockSpec((None, 1, nt), lambda l, j: (l, 0, j))],
        out_specs=pl.BlockSpec((None, SUBLANES, nt), lambda l, j: (l, 0, j)),
        out_shape=jax.ShapeDtypeStruct((n_layers, SUBLANES, n), F32),
        compiler_params=_cparams(2),
        name="ada_linear",
    )(c8, w, bias.reshape(n_layers, 1, n))


def _mod_rows(mods, batch, n_vec):
    m = mods[:batch].reshape(batch, n_vec, D)
    return jnp.pad(m, ((0, 0), (0, SUBLANES - n_vec), (0, 0)))


def _conv_mixer_kernel(x_ref, mod_ref, w_in_ref, cw_ref, w_out_ref, ln_ref,
                       o_ref, u_scr):
    tm = x_ref.shape[0]
    i = pl.program_id(1)
    mod = mod_ref[...]
    sh1, sc1, g1 = mod[0:1], mod[1:2], mod[2:3]
    x = x_ref[...]
    h = (x * (1.0 + sc1) + sh1).astype(BF16)
    cgate = _bdot(h, w_in_ref[:, D:2 * D])
    v = _bdot(h, w_in_ref[:, 2 * D:3 * D])
    u = cgate * v

    @pl.when(i == 0)
    def _():
        u_scr[0:SUBLANES, :] = jnp.zeros((SUBLANES, D), F32)

    u_scr[SUBLANES:SUBLANES + tm, :] = u
    cw = cw_ref[...]
    conv = (cw[2:3] * u
            + cw[1:2] * u_scr[SUBLANES - 1:SUBLANES - 1 + tm, :]
            + cw[0:1] * u_scr[SUBLANES - 2:SUBLANES - 2 + tm, :])
    u_scr[0:SUBLANES, :] = u_scr[tm:tm + SUBLANES, :]
    bgate = _bdot(h, w_in_ref[:, 0:D])
    y = _bdot((bgate * conv).astype(BF16), w_out_ref[...])
    ln = ln_ref[...]
    o_ref[...] = _layer_norm(ALPHA * x + (1.0 + g1) * y, ln[0:1], ln[1:2])


def _conv_mixer(x, mod, w_in, conv_w, w_out, ln):
    batch, seq, _ = x.shape
    tm = TOKEN_TILE
    const = lambda b, i: (0, 0)
    return pl.pallas_call(
        _conv_mixer_kernel,
        grid=(batch, seq // tm),
        in_specs=[pl.BlockSpec((None, tm, D), lambda b, i: (b, i, 0)),
                  pl.BlockSpec((None, SUBLANES, D), lambda b, i: (b, 0, 0)),
                  pl.BlockSpec((D, 3 * D), const),
                  pl.BlockSpec((SUBLANES, D), const),
                  pl.BlockSpec((D, D), const),
                  pl.BlockSpec((2, D), const)],
        out_specs=pl.BlockSpec((None, tm, D), lambda b, i: (b, i, 0)),
        out_shape=jax.ShapeDtypeStruct(x.shape, F32),
        scratch_shapes=[pltpu.VMEM((tm + SUBLANES, D), F32)],
        compiler_params=_cparams(2),
        name="conv_mixer",
    )(x, mod, w_in, conv_w, w_out, ln)


def _route(logits_t, rbias_col):
    aff = jax.nn.sigmoid(logits_t)
    sel = aff + rbias_col
    rows = [sel[e:e + 1, :] for e in range(N_EXPERTS)]
    scores = []
    for g in range(N_EXPERT_GROUPS):
        s0, s1, s2, s3 = rows[EXPERTS_PER_GROUP * g:EXPERTS_PER_GROUP * (g + 1)]
        a, b = jnp.maximum(s0, s1), jnp.minimum(s0, s1)
        c, d = jnp.maximum(s2, s3), jnp.minimum(s2, s3)
        scores.append(jnp.maximum(a, c) + jnp.maximum(jnp.minimum(a, c), jnp.maximum(b, d)))
    best = scores[0]
    best_group = jnp.zeros(best.shape, jnp.int32)
    for g in range(1, N_EXPERT_GROUPS):
        upd = scores[g] > best
        best_group = jnp.where(upd, g, best_group)
        best = jnp.where(upd, scores[g], best)
    e_iota = lax.broadcasted_iota(jnp.int32, sel.shape, 0)
    masked = jnp.where((e_iota // EXPERTS_PER_GROUP) == best_group, sel, NEG_INF)
    m1 = jnp.max(masked, axis=0, keepdims=True)
    idx1 = jnp.min(jnp.where(masked == m1, e_iota, N_EXPERTS), axis=0, keepdims=True)
    masked2 = jnp.where(e_iota == idx1, -jnp.inf, masked)
    m2 = jnp.max(masked2, axis=0, keepdims=True)
    idx2 = jnp.min(jnp.where(masked2 == m2, e_iota, N_EXPERTS), axis=0, keepdims=True)
    a1 = jnp.sum(jnp.where(e_iota == idx1, aff, 0.0), axis=0, keepdims=True)
    a2 = jnp.sum(jnp.where(e_iota == idx2, aff, 0.0), axis=0, keepdims=True)
    den = a1 + a2
    comb = (jnp.where(e_iota == idx1, a1 / den, 0.0)
            + jnp.where(e_iota == idx2, a2 / den, 0.0))
    group1 = idx1 // EXPERTS_PER_GROUP
    split = group1 != idx2 // EXPERTS_PER_GROUP
    return comb, group1, split


def _moe_kernel(x_ref, mod_ref, rw_cat_ref, rw_hi_ref, rb_ref, tri_ref, wgu_ref, wd_ref,
                ln_ref, o_ref, hs_scr, cs_scr, ys_scr):
    tm = x_ref.shape[0]
    gff = EXPERTS_PER_GROUP * EXPERT_FF
    mod = mod_ref[...]
    sh2, sc2, g2 = mod[3:4], mod[4:5], mod[5:6]
    x = x_ref[...]
    h = x * (1.0 + sc2) + sh2
    hb = h.astype(BF16)
    h_lo = (h - hb.astype(F32)).astype(BF16)
    nt = (((1,), (1,)), ((), ()))
    lg = lax.dot_general(rw_cat_ref[...], hb, nt, preferred_element_type=F32)
    logits_t = (lg[0:N_EXPERTS] + lg[N_EXPERTS:2 * N_EXPERTS]
                + lax.dot_general(rw_hi_ref[...], h_lo, nt, preferred_element_type=F32))
    comb_t, group, split = _route(logits_t, rb_ref[...])

    g_iota = lax.broadcasted_iota(jnp.int32, (SUBLANES, tm), 0)
    member = jnp.where(g_iota == group, 1.0, 0.0)
    rank = _bdot(member.astype(BF16), tri_ref[...])
    counts = jnp.sum(member, axis=1, keepdims=True)
    cnt = [counts[g, 0] for g in range(N_EXPERT_GROUPS)]
    start = [0.0]
    for g in range(N_EXPERT_GROUPS - 1):
        start.append(start[g] + cnt[g])
    dest = sum(member[g:g + 1] * (start[g] + rank[g:g + 1]) for g in range(N_EXPERT_GROUPS))
    row = lax.broadcasted_iota(jnp.int32, (tm, tm), 0).astype(F32)
    col = lax.broadcasted_iota(jnp.int32, (tm, tm), 1).astype(F32)
    perm = jnp.where(row == dest, 1.0, 0.0).astype(BF16)
    dest_col = jnp.broadcast_to(dest, (LANES, tm)).T
    dest_col = jnp.concatenate([dest_col] * (tm // LANES), axis=1)
    perm_t = jnp.where(dest_col == col, 1.0, 0.0).astype(BF16)

    hs_scr[...] = _bdot(perm, hb).astype(BF16)
    comb_hi = comb_t.astype(BF16)
    comb_lo = (comb_t - comb_hi.astype(F32)).astype(BF16)
    cs = _bdot(jnp.concatenate([comb_hi, comb_lo], axis=0), perm_t)
    cs = cs[0:N_EXPERTS] + cs[N_EXPERTS:2 * N_EXPERTS]
    cs_scr[...] = jnp.concatenate(
        [cs, jnp.zeros((LANES - N_EXPERTS, tm), F32)], axis=0).T
    ys_scr[...] = jnp.zeros(ys_scr.shape, F32)

    any_split = jnp.max(jnp.where(split, 1.0, 0.0)) > 0.0
    for k in range(tm // MOE_SUB):
        rows = slice(k * MOE_SUB, (k + 1) * MOE_SUB)
        for g in range(N_EXPERT_GROUPS):
            overlaps = jnp.logical_and(start[g] < (k + 1) * MOE_SUB,
                                       start[g] + cnt[g] > k * MOE_SUB)
            present = jnp.logical_and(overlaps, cnt[g] > 0.0)

            @pl.when(jnp.logical_or(present, any_split))
            def _(rows=rows, g=g):
                hs = hs_scr[rows, :]
                parts = []
                for j in range(EXPERTS_PER_GROUP):
                    e = g * EXPERTS_PER_GROUP + j
                    gu = _bdot(hs, wgu_ref[e])
                    gate, up = gu[:, :EXPERT_FF], gu[:, EXPERT_FF:]
                    he = gate * jax.nn.sigmoid(gate) * up * cs_scr[rows, e:e + 1]
                    parts.append(he.astype(BF16))
                he = jnp.concatenate(parts, axis=1)
                ys_scr[rows, :] += _bdot(he, wd_ref[g * gff:(g + 1) * gff, :])

    y = _bdot(perm_t, ys_scr[...].astype(BF16))
    ln = ln_ref[...]
    o_ref[...] = _layer_norm(ALPHA * x + (1.0 + g2) * y, ln[0:1], ln[1:2])


def _moe(x, mod, rw_cat, rw_hi, rbias, wgu, wd, ln):
    batch, seq, _ = x.shape
    tm = TOKEN_TILE
    const2 = lambda b, i: (0, 0)
    tri = (jnp.arange(tm)[:, None] < jnp.arange(tm)[None, :]).astype(BF16)
    return pl.pallas_call(
        _moe_kernel,
        grid=(batch, seq // tm),
        in_specs=[pl.BlockSpec((None, tm, D), lambda b, i: (b, i, 0)),
                  pl.BlockSpec((None, SUBLANES, D), lambda b, i: (b, 0, 0)),
                  pl.BlockSpec((2 * N_EXPERTS, D), const2),
                  pl.BlockSpec((N_EXPERTS, D), const2),
                  pl.BlockSpec((N_EXPERTS, 1), const2),
                  pl.BlockSpec((tm, tm), const2),
                  pl.BlockSpec((N_EXPERTS, D, 2 * EXPERT_FF), lambda b, i: (0, 0, 0)),
                  pl.BlockSpec((N_EXPERTS * EXPERT_FF, D), const2),
                  pl.BlockSpec((2, D), const2)],
        out_specs=pl.BlockSpec((None, tm, D), lambda b, i: (b, i, 0)),
        out_shape=jax.ShapeDtypeStruct(x.shape, F32),
        scratch_shapes=[pltpu.VMEM((tm, D), BF16),
                        pltpu.VMEM((tm, LANES), F32),
                        pltpu.VMEM((tm, D), F32)],
        compiler_params=_cparams(2),
        name="moe",
    )(x, mod, rw_cat, rw_hi, rbias, tri, wgu, wd, ln)


def _qkv_kernel(x_ref, mod_ref, kvmod_ref, wq_ref, wkv_ref, *refs):
    outs, (slab, hq_scr, hkv_scr) = refs[:3 * N_GROUPS], refs[3 * N_GROUPS:]
    tm = x_ref.shape[0]
    mod = mod_ref[...]
    kvmod = kvmod_ref[...]
    q_sc, q_sh = 1.0 + mod[1:2], mod[0:1]
    kv_sc, kv_sh = 1.0 + kvmod[1:2], kvmod[0:1]
    n_slabs = D // LANES
    for j in range(n_slabs):
        slab[j] = x_ref[:, j * LANES:(j + 1) * LANES]
    scale = HEAD_DIM ** -0.5
    for g, (_, dil) in enumerate(DIL_GROUPS):
        n = tm // dil
        for r in range(dil):
            rows = slice(r * n, (r + 1) * n)
            for j in range(n_slabs):
                ls = slice(j * LANES, (j + 1) * LANES)
                piece = slab[j] if dil == 1 else slab[j, pl.ds(r, n, stride=dil), :]
                hq_scr[rows, ls] = (piece * q_sc[:, ls] + q_sh[:, ls]).astype(BF16)
                hkv_scr[rows, ls] = (piece * kv_sc[:, ls] + kv_sh[:, ls]).astype(BF16)
        hq = hq_scr[...]
        hkv = hkv_scr[...]
        cols = slice(g * D, (g + 1) * D)
        vcols = slice(QW + g * D, QW + (g + 1) * D)
        q = (_bdot(hq, wq_ref[:, cols]) * scale).astype(BF16)
        k = _bdot(hkv, wkv_ref[:, cols]).astype(BF16)
        v = _bdot(hkv, wkv_ref[:, vcols]).astype(BF16)
        for r in range(dil):
            rows = slice(r * n, (r + 1) * n)
            outs[3 * g][r] = q[rows]
            outs[3 * g + 1][r] = k[rows]
            outs[3 * g + 2][r] = v[rows]


def _qkv(x, mod, kvmod, wq, wkv):
    batch, seq, _ = x.shape
    tm = TOKEN_TILE
    const = lambda b, i: (0, 0)
    out_specs, out_shape = [], []
    for _, dil in DIL_GROUPS:
        for _ in range(3):
            out_specs.append(pl.BlockSpec((None, dil, tm // dil, D), lambda b, i: (b, 0, i, 0)))
            out_shape.append(jax.ShapeDtypeStruct((batch, dil, seq // dil, D), BF16))
    return pl.pallas_call(
        _qkv_kernel,
        grid=(batch, seq // tm),
        in_specs=[pl.BlockSpec((None, tm, D), lambda b, i: (b, i, 0)),
                  pl.BlockSpec((None, SUBLANES, D), lambda b, i: (b, 0, 0)),
                  pl.BlockSpec((None, SUBLANES, D), lambda b, i: (b, 0, 0)),
                  pl.BlockSpec((D, QW), const),
                  pl.BlockSpec((D, 2 * QW), const)],
        out_specs=out_specs,
        out_shape=out_shape,
        scratch_shapes=[pltpu.VMEM((D // LANES, tm, LANES), F32),
                        pltpu.VMEM((tm, D), BF16),
                        pltpu.VMEM((tm, D), BF16)],
        compiler_params=_cparams(2),
        name="qkv_proj",
    )(x, mod, kvmod, wq, wkv)


def _bucket_tables():
    qi = jnp.arange(BLOCK)[:, None]
    kj = jnp.arange(2 * BLOCK)[None, :]
    dist = BLOCK + qi - kj
    max_exact = NUM_BUCKETS // 2
    tables = []
    for window, dil in DIL_GROUPS:
        n = jnp.maximum(dist, 0) * dil
        nf = jnp.maximum(n, 1).astype(F32)
        large = max_exact + (jnp.log(nf / max_exact) / math.log(MAX_DISTANCE / max_exact)
                             * (NUM_BUCKETS - max_exact)).astype(jnp.int32)
        large = jnp.minimum(large, NUM_BUCKETS - 1)
        bucket = jnp.where(n < max_exact, n, large)
        valid = (dist >= 0) & (dist <= window // dil)
        tables.append(jnp.where(valid, bucket, -1).astype(jnp.int32))
    return jnp.stack(tables)


def _bias_kernel(rb_ref, bkt_ref, o_ref):
    g = pl.program_id(0)
    h = pl.program_id(1)
    bkt = bkt_ref[...]
    acc = jnp.full(bkt.shape, NEG_INF, F32)
    for b in range(NUM_BUCKETS):
        acc = jnp.where(bkt == b, rb_ref[b, g * HEADS + h], acc)
    o_ref[...] = acc


def _bias_tiles(rel_bias):
    return pl.pallas_call(
        _bias_kernel,
        grid=(N_GROUPS, HEADS),
        in_specs=[pl.BlockSpec(memory_space=pltpu.SMEM),
                  pl.BlockSpec((None, BLOCK, 2 * BLOCK), lambda g, h: (g, 0, 0))],
        out_specs=pl.BlockSpec((None, None, BLOCK, 2 * BLOCK), lambda g, h: (g, h, 0, 0)),
        out_shape=jax.ShapeDtypeStruct((N_GROUPS, HEADS, BLOCK, 2 * BLOCK), F32),
        compiler_params=_cparams(2),
        name="rel_bias_tiles",
    )(rel_bias, _bucket_tables())


def _attn_kernel(q_ref, kp_ref, kc_ref, vp_ref, vc_ref, bias_ref, o_ref, lse_ref):
    n = pl.program_id(2)
    col = lax.broadcasted_iota(jnp.int32, (BLOCK, 2 * BLOCK), 1)
    no_prev = jnp.logical_and(n == 0, col < BLOCK)
    lane = lax.broadcasted_iota(jnp.int32, (BLOCK, LANES), 1)
    for j in range(q_ref.shape[0] // BLOCK):
        rows = slice(j * BLOCK, (j + 1) * BLOCK)
        window = slice((j - 1) * BLOCK, (j + 1) * BLOCK)
        lse_tile = jnp.zeros((BLOCK, LANES), F32)
        for h in range(HEADS):
            hs = slice(h * HEAD_DIM, (h + 1) * HEAD_DIM)
            if j == 0:
                k = jnp.concatenate([kp_ref[:, hs], kc_ref[rows, hs]], axis=0)
                v = jnp.concatenate([vp_ref[:, hs], vc_ref[rows, hs]], axis=0)
            else:
                k = kc_ref[window, hs]
                v = vc_ref[window, hs]
            s = lax.dot_general(q_ref[rows, hs], k, (((1,), (1,)), ((), ())),
                                preferred_element_type=F32) + bias_ref[h]
            if j == 0:
                s = jnp.where(no_prev, NEG_INF, s)
            m = jnp.max(s, axis=-1, keepdims=True)
            p = jnp.exp(s - m)
            l = jnp.sum(p, axis=-1, keepdims=True)
            o = _bdot(p.astype(BF16), v)
            o_ref[rows, hs] = (o / l).astype(BF16)
            lse_tile = jnp.where(lane == h, m + jnp.log(l), lse_tile)
        lse_ref[rows, :] = lse_tile


def _attention_group(g, q, k, v, bias):
    batch, dil, sub_len, _ = q.shape
    nq = ATTN_QBLOCKS
    cur = lambda b, r, n: (b, r, n, 0)
    prev = lambda b, r, n: (b, r, jnp.maximum(n * nq - 1, 0), 0)
    blk = (None, None, nq * BLOCK, D)
    blk_prev = (None, None, BLOCK, D)
    return pl.pallas_call(
        _attn_kernel,
        grid=(batch, dil, sub_len // (nq * BLOCK)),
        in_specs=[
            pl.BlockSpec(blk, cur),
            pl.BlockSpec(blk_prev, prev),
            pl.BlockSpec(blk, cur),
            pl.BlockSpec(blk_prev, prev),
            pl.BlockSpec(blk, cur),
            pl.BlockSpec((None, HEADS, BLOCK, 2 * BLOCK), lambda b, r, n: (g, 0, 0, 0)),
        ],
        out_specs=[pl.BlockSpec(blk, cur),
                   pl.BlockSpec((None, None, nq * BLOCK, LANES), cur)],
        out_shape=[jax.ShapeDtypeStruct((batch, dil, sub_len, D), BF16),
                   jax.ShapeDtypeStruct((batch, dil, sub_len, LANES), F32)],
        compiler_params=_cparams(3),
        name=f"dilated_attn_g{g}",
    )(q, k, k, v, v, bias)


def _mix_kernel(o0_ref, o1_ref, o2_ref, l0_ref, l1_ref, l2_ref, x_ref, mod_ref,
                expand_ref, wo_ref, ln_ref, out_ref, lse_scr, o_slab):
    tm = x_ref.shape[0]
    n_slabs = D // LANES
    lses = []
    for g, l_ref in enumerate((l0_ref, l1_ref, l2_ref)):
        dil = DIL_GROUPS[g][1]
        if dil == 1:
            lses.append(l_ref[0])
            continue
        for r in range(dil):
            lse_scr[g, pl.ds(r, tm // dil, stride=dil), :] = l_ref[r]
        lses.append(lse_scr[g])
    m = jnp.maximum(lses[0], jnp.maximum(lses[1], lses[2]))
    es = [jnp.exp(a - m) for a in lses]
    den = es[0] + es[1] + es[2]
    expand = expand_ref[...]
    mixed = None
    for g, o_ref in enumerate((o0_ref, o1_ref, o2_ref)):
        dil = DIL_GROUPS[g][1]
        w = _bdot((es[g] / den).astype(BF16), expand)
        if dil == 1:
            o = o_ref[0].astype(F32)
        else:
            for r in range(dil):
                for j in range(n_slabs):
                    o_slab[j, pl.ds(r, tm // dil, stride=dil), :] = (
                        o_ref[r, :, j * LANES:(j + 1) * LANES].astype(F32))
            o = jnp.concatenate([o_slab[j] for j in range(n_slabs)], axis=1)
        mixed = w * o if mixed is None else mixed + w * o
    y = _bdot(mixed.astype(BF16), wo_ref[...])
    g1 = mod_ref[...][2:3]
    ln = ln_ref[...]
    out_ref[...] = _layer_norm(ALPHA * x_ref[...] + (1.0 + g1) * y, ln[0:1], ln[1:2])


def _mix(outs, lses, x, mod, wo, ln):
    batch, seq, _ = x.shape
    tm = TOKEN_TILE
    const = lambda b, i: (0, 0)
    tok = pl.BlockSpec((None, tm, D), lambda b, i: (b, i, 0))
    dilated = lambda dil, w: pl.BlockSpec((None, dil, tm // dil, w), lambda b, i: (b, 0, i, 0))
    dils = [dil for _, dil in DIL_GROUPS]
    head_of_lane = jnp.arange(D)[None, :] // HEAD_DIM
    expand = (jnp.arange(LANES)[:, None] == head_of_lane).astype(BF16)
    return pl.pallas_call(
        _mix_kernel,
        grid=(batch, seq // tm),
        in_specs=[dilated(d, D) for d in dils] + [dilated(d, LANES) for d in dils] + [
            tok,
            pl.BlockSpec((None, SUBLANES, D), lambda b, i: (b, 0, 0)),
            pl.BlockSpec((LANES, D), const),
            pl.BlockSpec((D, D), const),
            pl.BlockSpec((2, D), const)],
        out_specs=tok,
        out_shape=jax.ShapeDtypeStruct(x.shape, F32),
        scratch_shapes=[pltpu.VMEM((N_GROUPS, tm, LANES), F32),
                        pltpu.VMEM((D // LANES, tm, LANES), F32)],
        compiler_params=_cparams(2),
        name="attn_mix",
    )(*outs, *lses, x, mod, expand, wo, ln)


def kernel(x, c, ada_w, ada_b, ln_g, ln_b, conv_w_in, conv_w, conv_w_out, kv_ada_w,
           kv_ada_b, w_kv, attn_w_q, attn_w_o, rel_bias, router_w, router_bias,
           moe_w_gate, moe_w_up, moe_w_down):
    batch, seq, _ = x.shape
    assert x.shape[2] == D and batch <= SUBLANES
    assert seq % TOKEN_TILE == 0 and seq % (ATTN_QBLOCKS * BLOCK * DIL_GROUPS[-1][1]) == 0

    c8 = jnp.pad(c, ((0, SUBLANES - batch), (0, 0)))
    mods = _ada_linear(c8, ada_w, ada_b)
    kvmods = _ada_linear(c8, kv_ada_w[None], kv_ada_b[None])[0]
    mod0 = _mod_rows(mods[0], batch, 6)
    mod1 = _mod_rows(mods[1], batch, 6)
    kvmod = _mod_rows(kvmods, batch, 2)
    ln = jnp.stack([ln_g, ln_b], axis=2)

    rw_hi = router_w.T.astype(BF16)
    rw_lo = (router_w.T - rw_hi.astype(F32)).astype(BF16)
    rw_cat = jnp.concatenate([rw_hi, rw_lo], axis=0)
    rbias = router_bias.reshape(N_EXPERTS, 1)
    wgu = jnp.concatenate([moe_w_gate, moe_w_up], axis=-1).astype(BF16)
    wd = moe_w_down.astype(BF16).reshape(DEPTH, N_EXPERTS * EXPERT_FF, D)
    conv_w8 = jnp.pad(conv_w[0], ((0, SUBLANES - CONV_WIDTH), (0, 0)))

    x = _conv_mixer(x, mod0, conv_w_in[0].astype(BF16), conv_w8,
                    conv_w_out[0].astype(BF16), ln[0, 0])
    x = _moe(x, mod0, rw_cat, rw_hi, rbias, wgu[0], wd[0], ln[0, 1])

    qkv = _qkv(x, mod1, kvmod, attn_w_q[0].astype(BF16), w_kv.astype(BF16))
    bias = _bias_tiles(rel_bias)
    outs, lses = zip(*[_attention_group(g, *qkv[3 * g:3 * g + 3], bias)
                       for g in range(N_GROUPS)])
    x = _mix(outs, lses, x, mod1, attn_w_o[0].astype(BF16), ln[1, 0])
    x = _moe(x, mod1, rw_cat, rw_hi, rbias, wgu[1], wd[1], ln[1, 1])
    return x
```

```python
import functools
import math

import jax
import jax.numpy as jnp
from jax import lax
from jax.experimental import pallas as pl
from jax.experimental.pallas import tpu as pltpu

F32 = jnp.float32
BF16 = jnp.bfloat16

D = 1024
DEPTH = 2
CONV_WIDTH = 3
DIL_GROUPS = ((128, 1), (512, 4), (2048, 16))
N_GROUPS = len(DIL_GROUPS)
HEAD_DIM = 128
HEADS = D // HEAD_DIM
QW = N_GROUPS * D
BLOCK = 128
NUM_BUCKETS = 32
MAX_DISTANCE = 2048
N_EXPERTS = 16
EXPERTS_PER_GROUP = 4
N_EXPERT_GROUPS = N_EXPERTS // EXPERTS_PER_GROUP
EXPERT_FF = D // 4
ALPHA = (2 * DEPTH) ** 0.25
LN_EPS = 1e-5
NEG_INF = -1e30
LOG2E = math.log2(math.e)
LN2 = math.log(2.0)

LANES = 128
SUBLANES = 8
VMEM_LIMIT = 56 * 1024 * 1024
TOKEN_TILE = 512
MOE_SUB = 128
ATTN_QBLOCKS = 4
ADA_COL_TILE = 2048


def _cparams(n_axes):
    return pltpu.CompilerParams(
        dimension_semantics=("arbitrary",) * n_axes, vmem_limit_bytes=VMEM_LIMIT)


def _layer_norm(r, g, b):
    mu = jnp.mean(r, axis=-1, keepdims=True)
    c = r - mu
    var = jnp.mean(c * c, axis=-1, keepdims=True)
    return c * lax.rsqrt(var + LN_EPS) * g + b


def _bdot(a, b):
    return jnp.dot(a, b, preferred_element_type=F32)


def _ada_kernel(c_ref, w_ref, b_ref, o_ref):
    c = c_ref[...]
    cond = c * jax.nn.sigmoid(c)
    o_ref[...] = jnp.dot(cond, w_ref[...], preferred_element_type=F32,
                         precision=lax.Precision.HIGHEST) + b_ref[...]


def _ada_linear(c8, w, bias):
    n_layers, _, n = w.shape
    nt = ADA_COL_TILE
    return pl.pallas_call(
        _ada_kernel,
        grid=(n_layers, n // nt),
        in_specs=[pl.BlockSpec((SUBLANES, D), lambda l, j: (0, 0)),
                  pl.BlockSpec((None, D, nt), lambda l, j: (l, 0, j)),
                  pl.BlockSpec((None, 1, nt), lambda l, j: (l, 0, j))],
        out_specs=pl.BlockSpec((None, SUBLANES, nt), lambda l, j: (l, 0, j)),
        out_shape=jax.ShapeDtypeStruct((n_layers, SUBLANES, n), F32),
        compiler_params=_cparams(2),
        name="ada_linear",
    )(c8, w, bias.reshape(n_layers, 1, n))


def _mod_rows(mods, batch, n_vec):
    m = mods[:batch].reshape(batch, n_vec, D)
    return jnp.pad(m, ((0, 0), (0, SUBLANES - n_vec), (0, 0)))


def _conv_mixer_kernel(x_ref, mod_ref, w_in_ref, cw_ref, w_out_ref, ln_ref,
                       o_ref, u_scr):
    tm = x_ref.shape[0]
    i = pl.program_id(1)
    mod = mod_ref[...]
    sh1, sc1, g1 = mod[0:1], mod[1:2], mod[2:3]
    x = x_ref[...]
    h = (x * (1.0 + sc1) + sh1).astype(BF16)
    cgate = _bdot(h, w_in_ref[:, D:2 * D])
    v = _bdot(h, w_in_ref[:, 2 * D:3 * D])
    u = cgate * v

    @pl.when(i == 0)
    def _():
        u_scr[0:SUBLANES, :] = jnp.zeros((SUBLANES, D), F32)

    u_scr[SUBLANES:SUBLANES + tm, :] = u
    cw = cw_ref[...]
    conv = (cw[2:3] * u
            + cw[1:2] * u_scr[SUBLANES - 1:SUBLANES - 1 + tm, :]
            + cw[0:1] * u_scr[SUBLANES - 2:SUBLANES - 2 + tm, :])
    u_scr[0:SUBLANES, :] = u_scr[tm:tm + SUBLANES, :]
    bgate = _bdot(h, w_in_ref[:, 0:D])
    y = _bdot((bgate * conv).astype(BF16), w_out_ref[...])
    ln = ln_ref[...]
    o_ref[...] = _layer_norm(ALPHA * x + (1.0 + g1) * y, ln[0:1], ln[1:2])


def _conv_mixer(x, mod, w_in, conv_w, w_out, ln):
    batch, seq, _ = x.shape
    tm = TOKEN_TILE
    const = lambda b, i: (0, 0)
    return pl.pallas_call(
        _conv_mixer_kernel,
        grid=(batch, seq // tm),
        in_specs=[pl.BlockSpec((None, tm, D), lambda b, i: (b, i, 0)),
                  pl.BlockSpec((None, SUBLANES, D), lambda b, i: (b, 0, 0)),
                  pl.BlockSpec((D, 3 * D), const),
                  pl.BlockSpec((SUBLANES, D), const),
                  pl.BlockSpec((D, D), const),
                  pl.BlockSpec((2, D), const)],
        out_specs=pl.BlockSpec((None, tm, D), lambda b, i: (b, i, 0)),
        out_shape=jax.ShapeDtypeStruct(x.shape, F32),
        scratch_shapes=[pltpu.VMEM((tm + SUBLANES, D), F32)],
        compiler_params=_cparams(2),
        name="conv_mixer",
    )(x, mod, w_in, conv_w, w_out, ln)


def _route(logits_t, rbias_col):
    aff = jax.nn.sigmoid(logits_t)
    sel = aff + rbias_col
    rows = [sel[e:e + 1, :] for e in range(N_EXPERTS)]
    scores = []
    for g in range(N_EXPERT_GROUPS):
        s0, s1, s2, s3 = rows[EXPERTS_PER_GROUP * g:EXPERTS_PER_GROUP * (g + 1)]
        a, b = jnp.maximum(s0, s1), jnp.minimum(s0, s1)
        c, d = jnp.maximum(s2, s3), jnp.minimum(s2, s3)
        scores.append(jnp.maximum(a, c) + jnp.maximum(jnp.minimum(a, c), jnp.maximum(b, d)))
    best = scores[0]
    best_group = jnp.zeros(best.shape, jnp.int32)
    for g in range(1, N_EXPERT_GROUPS):
        upd = scores[g] > best
        best_group = jnp.where(upd, g, best_group)
        best = jnp.where(upd, scores[g], best)
    e_iota = lax.broadcasted_iota(jnp.int32, sel.shape, 0)
    masked = jnp.where((e_iota // EXPERTS_PER_GROUP) == best_group, sel, NEG_INF)
    m1 = jnp.max(masked, axis=0, keepdims=True)
    idx1 = jnp.min(jnp.where(masked == m1, e_iota, N_EXPERTS), axis=0, keepdims=True)
    masked2 = jnp.where(e_iota == idx1, -jnp.inf, masked)
    m2 = jnp.max(masked2, axis=0, keepdims=True)
    idx2 = jnp.min(jnp.where(masked2 == m2, e_iota, N_EXPERTS), axis=0, keepdims=True)
    a1 = jnp.sum(jnp.where(e_iota == idx1, aff, 0.0), axis=0, keepdims=True)
    a2 = jnp.sum(jnp.where(e_iota == idx2, aff, 0.0), axis=0, keepdims=True)
    den = a1 + a2
    comb = (jnp.where(e_iota == idx1, a1 / den, 0.0)
            + jnp.where(e_iota == idx2, a2 / den, 0.0))
    group1 = idx1 // EXPERTS_PER_GROUP
    split = group1 != idx2 // EXPERTS_PER_GROUP
    return comb, group1, split


def _moe_kernel(x_ref, mod_ref, rw_ref, rb_ref, tri_ref, wgu_ref, wd_ref,
                ln_ref, o_ref, hs_scr, cs_scr, ys_scr):
    tm = x_ref.shape[0]
    gff = EXPERTS_PER_GROUP * EXPERT_FF
    mod = mod_ref[...]
    sh2, sc2, g2 = mod[3:4], mod[4:5], mod[5:6]
    x = x_ref[...]
    h = x * (1.0 + sc2) + sh2
    hb = h.astype(BF16)
    logits_t = lax.dot_general(rw_ref[...], hb, (((1,), (1,)), ((), ())),
                               preferred_element_type=F32)
    comb_t, group, split = _route(logits_t, rb_ref[...])

    g_iota = lax.broadcasted_iota(jnp.int32, (SUBLANES, tm), 0)
    member = jnp.where(g_iota == group, 1.0, 0.0)
    rank = _bdot(member.astype(BF16), tri_ref[...])
    counts = jnp.sum(member, axis=1, keepdims=True)
    cnt = [counts[g, 0] for g in range(N_EXPERT_GROUPS)]
    start = [0.0]
    for g in range(N_EXPERT_GROUPS - 1):
        start.append(start[g] + cnt[g])
    dest = sum(member[g:g + 1] * (start[g] + rank[g:g + 1]) for g in range(N_EXPERT_GROUPS))
    row = lax.broadcasted_iota(jnp.int32, (tm, tm), 0).astype(F32)
    col = lax.broadcasted_iota(jnp.int32, (tm, tm), 1).astype(F32)
    perm = jnp.where(row == dest, 1.0, 0.0).astype(BF16)
    dest_col = jnp.broadcast_to(dest, (LANES, tm)).T
    dest_col = jnp.concatenate([dest_col] * (tm // LANES), axis=1)
    perm_t = jnp.where(dest_col == col, 1.0, 0.0).astype(BF16)

    hs_scr[...] = _bdot(perm, hb).astype(BF16)
    comb_hi = comb_t.astype(BF16)
    comb_lo = (comb_t - comb_hi.astype(F32)).astype(BF16)
    cs = _bdot(jnp.concatenate([comb_hi, comb_lo], axis=0), perm_t)
    cs = cs[0:N_EXPERTS] + cs[N_EXPERTS:2 * N_EXPERTS]
    cs_scr[...] = jnp.concatenate(
        [cs, jnp.zeros((LANES - N_EXPERTS, tm), F32)], axis=0).T
    ys_scr[...] = jnp.zeros(ys_scr.shape, F32)

    any_split = jnp.max(jnp.where(split, 1.0, 0.0)) > 0.0
    for k in range(tm // MOE_SUB):
        rows = slice(k * MOE_SUB, (k + 1) * MOE_SUB)
        for g in range(N_EXPERT_GROUPS):
            overlaps = jnp.logical_and(start[g] < (k + 1) * MOE_SUB,
                                       start[g] + cnt[g] > k * MOE_SUB)
            present = jnp.logical_and(overlaps, cnt[g] > 0.0)

            @pl.when(jnp.logical_or(present, any_split))
            def _(rows=rows, g=g):
                hs = hs_scr[rows, :]
                parts = []
                for j in range(EXPERTS_PER_GROUP):
                    e = g * EXPERTS_PER_GROUP + j
                    gu = _bdot(hs, wgu_ref[e])
                    gate, up = gu[:, :EXPERT_FF], gu[:, EXPERT_FF:]
                    he = gate * jax.nn.sigmoid(gate) * up * cs_scr[rows, e:e + 1]
                    parts.append(he.astype(BF16))
                he = jnp.concatenate(parts, axis=1)
                ys_scr[rows, :] += _bdot(he, wd_ref[g * gff:(g + 1) * gff, :])

    y = _bdot(perm_t, ys_scr[...].astype(BF16))
    ln = ln_ref[...]
    o_ref[...] = _layer_norm(ALPHA * x + (1.0 + g2) * y, ln[0:1], ln[1:2])


def _moe(x, mod, rw, rbias, wgu, wd, ln):
    batch, seq, _ = x.shape
    tm = TOKEN_TILE
    const2 = lambda b, i: (0, 0)
    tri = (jnp.arange(tm)[:, None] < jnp.arange(tm)[None, :]).astype(BF16)
    return pl.pallas_call(
        _moe_kernel,
        grid=(batch, seq // tm),
        in_specs=[pl.BlockSpec((None, tm, D), lambda b, i: (b, i, 0)),
                  pl.BlockSpec((None, SUBLANES, D), lambda b, i: (b, 0, 0)),
                  pl.BlockSpec((N_EXPERTS, D), const2),
                  pl.BlockSpec((N_EXPERTS, 1), const2),
                  pl.BlockSpec((tm, tm), const2),
                  pl.BlockSpec((N_EXPERTS, D, 2 * EXPERT_FF), lambda b, i: (0, 0, 0)),
                  pl.BlockSpec((N_EXPERTS * EXPERT_FF, D), const2),
                  pl.BlockSpec((2, D), const2)],
        out_specs=pl.BlockSpec((None, tm, D), lambda b, i: (b, i, 0)),
        out_shape=jax.ShapeDtypeStruct(x.shape, F32),
        scratch_shapes=[pltpu.VMEM((tm, D), BF16),
                        pltpu.VMEM((tm, LANES), F32),
                        pltpu.VMEM((tm, D), F32)],
        compiler_params=_cparams(2),
        name="moe",
    )(x, mod, rw, rbias, tri, wgu, wd, ln)


def _qkv_kernel(x_ref, mod_ref, kvmod_ref, wq_ref, wkv_ref, *refs):
    outs, (slab, hq_scr, hkv_scr) = refs[:3 * N_GROUPS], refs[3 * N_GROUPS:]
    tm = x_ref.shape[0]
    mod = mod_ref[...]
    kvmod = kvmod_ref[...]
    q_sc, q_sh = 1.0 + mod[1:2], mod[0:1]
    kv_sc, kv_sh = 1.0 + kvmod[1:2], kvmod[0:1]
    n_slabs = D // LANES
    for j in range(n_slabs):
        slab[j] = x_ref[:, j * LANES:(j + 1) * LANES]
    scale = HEAD_DIM ** -0.5 * LOG2E
    for g, (_, dil) in enumerate(DIL_GROUPS):
        n = tm // dil
        for r in range(dil):
            rows = slice(r * n, (r + 1) * n)
            for j in range(n_slabs):
                ls = slice(j * LANES, (j + 1) * LANES)
                piece = slab[j] if dil == 1 else slab[j, pl.ds(r, n, stride=dil), :]
                hq_scr[rows, ls] = (piece * q_sc[:, ls] + q_sh[:, ls]).astype(BF16)
                hkv_scr[rows, ls] = (piece * kv_sc[:, ls] + kv_sh[:, ls]).astype(BF16)
        hq = hq_scr[...]
        hkv = hkv_scr[...]
        cols = slice(g * D, (g + 1) * D)
        vcols = slice(QW + g * D, QW + (g + 1) * D)
        q = (_bdot(hq, wq_ref[:, cols]) * scale).astype(BF16)
        k = _bdot(hkv, wkv_ref[:, cols]).astype(BF16)
        v = _bdot(hkv, wkv_ref[:, vcols]).astype(BF16)
        for r in range(dil):
            rows = slice(r * n, (r + 1) * n)
            outs[3 * g][r] = q[rows]
            outs[3 * g + 1][r] = k[rows]
            outs[3 * g + 2][r] = v[rows]


def _qkv(x, mod, kvmod, wq, wkv):
    batch, seq, _ = x.shape
    tm = TOKEN_TILE
    const = lambda b, i: (0, 0)
    out_specs, out_shape = [], []
    for _, dil in DIL_GROUPS:
        for _ in range(3):
            out_specs.append(pl.BlockSpec((None, dil, tm // dil, D), lambda b, i: (b, 0, i, 0)))
            out_shape.append(jax.ShapeDtypeStruct((batch, dil, seq // dil, D), BF16))
    return pl.pallas_call(
        _qkv_kernel,
        grid=(batch, seq // tm),
        in_specs=[pl.BlockSpec((None, tm, D), lambda b, i: (b, i, 0)),
                  pl.BlockSpec((None, SUBLANES, D), lambda b, i: (b, 0, 0)),
                  pl.BlockSpec((None, SUBLANES, D), lambda b, i: (b, 0, 0)),
                  pl.BlockSpec((D, QW), const),
                  pl.BlockSpec((D, 2 * QW), const)],
        out_specs=out_specs,
        out_shape=out_shape,
        scratch_shapes=[pltpu.VMEM((D // LANES, tm, LANES), F32),
                        pltpu.VMEM((tm, D), BF16),
                        pltpu.VMEM((tm, D), BF16)],
        compiler_params=_cparams(2),
        name="qkv_proj",
    )(x, mod, kvmod, wq, wkv)


def _bucket_tables():
    qi = jnp.arange(BLOCK)[:, None]
    kj = jnp.arange(2 * BLOCK)[None, :]
    dist = BLOCK + qi - kj
    max_exact = NUM_BUCKETS // 2
    tables = []
    for window, dil in DIL_GROUPS:
        n = jnp.maximum(dist, 0) * dil
        nf = jnp.maximum(n, 1).astype(F32)
        large = max_exact + (jnp.log(nf / max_exact) / math.log(MAX_DISTANCE / max_exact)
                             * (NUM_BUCKETS - max_exact)).astype(jnp.int32)
        large = jnp.minimum(large, NUM_BUCKETS - 1)
        bucket = jnp.where(n < max_exact, n, large)
        valid = (dist >= 0) & (dist <= window // dil)
        tables.append(jnp.where(valid, bucket, -1).astype(jnp.int32))
    return jnp.stack(tables)


def _bias_kernel(rb_ref, bkt_ref, o_ref):
    g = pl.program_id(0)
    h = pl.program_id(1)
    bkt = bkt_ref[...]
    acc = jnp.full(bkt.shape, NEG_INF, F32)
    for b in range(NUM_BUCKETS):
        acc = jnp.where(bkt == b, rb_ref[b, g * HEADS + h] * LOG2E, acc)
    o_ref[...] = acc


def _bias_tiles(rel_bias):
    return pl.pallas_call(
        _bias_kernel,
        grid=(N_GROUPS, HEADS),
        in_specs=[pl.BlockSpec(memory_space=pltpu.SMEM),
                  pl.BlockSpec((None, BLOCK, 2 * BLOCK), lambda g, h: (g, 0, 0))],
        out_specs=pl.BlockSpec((None, None, BLOCK, 2 * BLOCK), lambda g, h: (g, h, 0, 0)),
        out_shape=jax.ShapeDtypeStruct((N_GROUPS, HEADS, BLOCK, 2 * BLOCK), F32),
        compiler_params=_cparams(2),
        name="rel_bias_tiles",
    )(rel_bias, _bucket_tables())


def _attn_kernel(q_ref, kp_ref, kc_ref, vp_ref, vc_ref, bias_ref, o_ref, lse_ref):
    n = pl.program_id(2)
    col = lax.broadcasted_iota(jnp.int32, (BLOCK, 2 * BLOCK), 1)
    no_prev = jnp.logical_and(n == 0, col < BLOCK)
    lane = lax.broadcasted_iota(jnp.int32, (BLOCK, LANES), 1)
    ones = jnp.ones((2 * BLOCK, HEAD_DIM), BF16)
    for j in range(q_ref.shape[0] // BLOCK):
        rows = slice(j * BLOCK, (j + 1) * BLOCK)
        window = slice((j - 1) * BLOCK, (j + 1) * BLOCK)
        lse_tile = jnp.zeros((BLOCK, LANES), F32)
        for h in range(HEADS):
            hs = slice(h * HEAD_DIM, (h + 1) * HEAD_DIM)
            if j == 0:
                k = jnp.concatenate([kp_ref[:, hs], kc_ref[rows, hs]], axis=0)
                v = jnp.concatenate([vp_ref[:, hs], vc_ref[rows, hs]], axis=0)
            else:
                k = kc_ref[window, hs]
                v = vc_ref[window, hs]
            s = lax.dot_general(q_ref[rows, hs], k, (((1,), (1,)), ((), ())),
                                preferred_element_type=F32) + bias_ref[h]
            if j == 0:
                s = jnp.where(no_prev, NEG_INF, s)
            m = jnp.max(s, axis=-1, keepdims=True)
            p = jnp.exp2(s - m)
            o_ext = _bdot(p.astype(BF16), jnp.concatenate([v, ones], axis=1))
            o, l = o_ext[:, :HEAD_DIM], o_ext[:, HEAD_DIM:]
            o_ref[rows, hs] = (o / l).astype(BF16)
            lse_tile = jnp.where(lane == h, m * LN2 + jnp.log(l), lse_tile)
        lse_ref[rows, :] = lse_tile


def _attention_group(g, q, k, v, bias):
    batch, dil, sub_len, _ = q.shape
    nq = ATTN_QBLOCKS
    cur = lambda b, r, n: (b, r, n, 0)
    prev = lambda b, r, n: (b, r, jnp.maximum(n * nq - 1, 0), 0)
    blk = (None, None, nq * BLOCK, D)
    blk_prev = (None, None, BLOCK, D)
    return pl.pallas_call(
        _attn_kernel,
        grid=(batch, dil, sub_len // (nq * BLOCK)),
        in_specs=[
            pl.BlockSpec(blk, cur),
            pl.BlockSpec(blk_prev, prev),
            pl.BlockSpec(blk, cur),
            pl.BlockSpec(blk_prev, prev),
            pl.BlockSpec(blk, cur),
            pl.BlockSpec((None, HEADS, BLOCK, 2 * BLOCK), lambda b, r, n: (g, 0, 0, 0)),
        ],
        out_specs=[pl.BlockSpec(blk, cur),
                   pl.BlockSpec((None, None, nq * BLOCK, LANES), cur)],
        out_shape=[jax.ShapeDtypeStruct((batch, dil, sub_len, D), BF16),
                   jax.ShapeDtypeStruct((batch, dil, sub_len, LANES), F32)],
        compiler_params=_cparams(3),
        name=f"dilated_attn_g{g}",
    )(q, k, k, v, v, bias)


def _mix_kernel(o0_ref, o1_ref, o2_ref, l0_ref, l1_ref, l2_ref, x_ref, mod_ref,
                expand_ref, wo_ref, ln_ref, out_ref, lse_scr, o_slab):
    tm = x_ref.shape[0]
    n_slabs = D // LANES
    lses = []
    for g, l_ref in enumerate((l0_ref, l1_ref, l2_ref)):
        dil = DIL_GROUPS[g][1]
        if dil == 1:
            lses.append(l_ref[0])
            continue
        for r in range(dil):
            lse_scr[g, pl.ds(r, tm // dil, stride=dil), :] = l_ref[r]
        lses.append(lse_scr[g])
    m = jnp.maximum(lses[0], jnp.maximum(lses[1], lses[2]))
    es = [jnp.exp(a - m) for a in lses]
    den = es[0] + es[1] + es[2]
    expand = expand_ref[...]
    mixed = None
    for g, o_ref in enumerate((o0_ref, o1_ref, o2_ref)):
        dil = DIL_GROUPS[g][1]
        w = _bdot((es[g] / den).astype(BF16), expand)
        if dil == 1:
            o = o_ref[0].astype(F32)
        else:
            for r in range(dil):
                for j in range(n_slabs):
                    o_slab[j, pl.ds(r, tm // dil, stride=dil), :] = (
                        o_ref[r, :, j * LANES:(j + 1) * LANES].astype(F32))
            o = jnp.concatenate([o_slab[j] for j in range(n_slabs)], axis=1)
        mixed = w * o if mixed is None else mixed + w * o
    y = _bdot(mixed.astype(BF16), wo_ref[...])
    g1 = mod_ref[...][2:3]
    ln = ln_ref[...]
    out_ref[...] = _layer_norm(ALPHA * x_ref[...] + (1.0 + g1) * y, ln[0:1], ln[1:2])


def _mix(outs, lses, x, mod, wo, ln):
    batch, seq, _ = x.shape
    tm = TOKEN_TILE
    const = lambda b, i: (0, 0)
    tok = pl.BlockSpec((None, tm, D), lambda b, i: (b, i, 0))
    dilated = lambda dil, w: pl.BlockSpec((None, dil, tm // dil, w), lambda b, i: (b, 0, i, 0))
    dils = [dil for _, dil in DIL_GROUPS]
    head_of_lane = jnp.arange(D)[None, :] // HEAD_DIM
    expand = (jnp.arange(LANES)[:, None] == head_of_lane).astype(BF16)
    return pl.pallas_call(
        _mix_kernel,
        grid=(batch, seq // tm),
        in_specs=[dilated(d, D) for d in dils] + [dilated(d, LANES) for d in dils] + [
            tok,
            pl.BlockSpec((None, SUBLANES, D), lambda b, i: (b, 0, 0)),
            pl.BlockSpec((LANES, D), const),
            pl.BlockSpec((D, D), const),
            pl.BlockSpec((2, D), const)],
        out_specs=tok,
        out_shape=jax.ShapeDtypeStruct(x.shape, F32),
        scratch_shapes=[pltpu.VMEM((N_GROUPS, tm, LANES), F32),
                        pltpu.VMEM((D // LANES, tm, LANES), F32)],
        compiler_params=_cparams(2),
        name="attn_mix",
    )(*outs, *lses, x, mod, expand, wo, ln)


def kernel(x, c, ada_w, ada_b, ln_g, ln_b, conv_w_in, conv_w, conv_w_out, kv_ada_w,
           kv_ada_b, w_kv, attn_w_q, attn_w_o, rel_bias, router_w, router_bias,
           moe_w_gate, moe_w_up, moe_w_down):
    batch, seq, _ = x.shape
    assert x.shape[2] == D and batch <= SUBLANES
    assert seq % TOKEN_TILE == 0 and seq % (ATTN_QBLOCKS * BLOCK * DIL_GROUPS[-1][1]) == 0

    c8 = jnp.pad(c, ((0, SUBLANES - batch), (0, 0)))
    mods = _ada_linear(c8, ada_w, ada_b)
    kvmods = _ada_linear(c8, kv_ada_w[None], kv_ada_b[None])[0]
    mod0 = _mod_rows(mods[0], batch, 6)
    mod1 = _mod_rows(mods[1], batch, 6)
    kvmod = _mod_rows(kvmods, batch, 2)
    ln = jnp.stack([ln_g, ln_b], axis=2)

    rw = router_w.T.astype(BF16)
    rbias = router_bias.reshape(N_EXPERTS, 1)
    wgu = jnp.concatenate([moe_w_gate, moe_w_up], axis=-1).astype(BF16)
    wd = moe_w_down.astype(BF16).reshape(DEPTH, N_EXPERTS * EXPERT_FF, D)
    conv_w8 = jnp.pad(conv_w[0], ((0, SUBLANES - CONV_WIDTH), (0, 0)))

    x = _conv_mixer(x, mod0, conv_w_in[0].astype(BF16), conv_w8,
                    conv_w_out[0].astype(BF16), ln[0, 0])
    x = _moe(x, mod0, rw, rbias, wgu[0], wd[0], ln[0, 1])

    qkv = _qkv(x, mod1, kvmod, attn_w_q[0].astype(BF16), w_kv.astype(BF16))
    bias = _bias_tiles(rel_bias)
    outs, lses = zip(*[_attention_group(g, *qkv[3 * g:3 * g + 3], bias)
                       for g in range(N_GROUPS)])
    x = _mix(outs, lses, x, mod1, attn_w_o[0].astype(BF16), ln[1, 0])
    x = _moe(x, mod1, rw, rbias, wgu[1], wd[1], ln[1, 1])
    return x
```

```python
import functools
import math

import jax
import jax.numpy as jnp
from jax import lax
from jax.experimental import pallas as pl
from jax.experimental.pallas import tpu as pltpu

F32 = jnp.float32
BF16 = jnp.bfloat16

D = 1024
DEPTH = 2
CONV_WIDTH = 3
DIL_GROUPS = ((128, 1), (512, 4), (2048, 16))
N_GROUPS = len(DIL_GROUPS)
HEAD_DIM = 128
HEADS = D // HEAD_DIM
QW = N_GROUPS * D
BLOCK = 128
NUM_BUCKETS = 32
MAX_DISTANCE = 2048
N_EXPERTS = 16
EXPERTS_PER_GROUP = 4
N_EXPERT_GROUPS = N_EXPERTS // EXPERTS_PER_GROUP
EXPERT_FF = D // 4
ALPHA = (2 * DEPTH) ** 0.25
LN_EPS = 1e-5
NEG_INF = -1e30
LOG2E = math.log2(math.e)
LN2 = math.log(2.0)

LANES = 128
SUBLANES = 8
VMEM_LIMIT = 56 * 1024 * 1024
TOKEN_TILE = 512
MOE_WIN = 160
BF16_ROWS = 16
ATTN_QBLOCKS = 4
ADA_COL_TILE = 2048


def _cparams(n_axes):
    return pltpu.CompilerParams(
        dimension_semantics=("arbitrary",) * n_axes, vmem_limit_bytes=VMEM_LIMIT)


def _layer_norm(r, g, b):
    mu = jnp.mean(r, axis=-1, keepdims=True)
    c = r - mu
    var = jnp.mean(c * c, axis=-1, keepdims=True)
    return c * lax.rsqrt(var + LN_EPS) * g + b


def _bdot(a, b):
    return jnp.dot(a, b, preferred_element_type=F32)


def _ada_kernel(c_ref, w_ref, b_ref, o_ref):
    c = c_ref[...]
    cond = c * jax.nn.sigmoid(c)
    o_ref[...] = jnp.dot(cond, w_ref[...], preferred_element_type=F32,
                         precision=lax.Precision.HIGHEST) + b_ref[...]


def _ada_linear(c8, w, bias):
    n_layers, _, n = w.shape
    nt = ADA_COL_TILE
    return pl.pallas_call(
        _ada_kernel,
        grid=(n_layers, n // nt),
        in_specs=[pl.BlockSpec((SUBLANES, D), lambda l, j: (0, 0)),
                  pl.BlockSpec((None, D, nt), lambda l, j: (l, 0, j)),
                  pl.BlockSpec((None, 1, nt), lambda l, j: (l, 0, j))],
        out_specs=pl.BlockSpec((None, SUBLANES, nt), lambda l, j: (l, 0, j)),
        out_shape=jax.ShapeDtypeStruct((n_layers, SUBLANES, n), F32),
        compiler_params=_cparams(2),
        name="ada_linear",
    )(c8, w, bias.reshape(n_layers, 1, n))


def _mod_rows(mods, batch, n_vec):
    m = mods[:batch].reshape(batch, n_vec, D)
    return jnp.pad(m, ((0, 0), (0, SUBLANES - n_vec), (0, 0)))


def _conv_mixer_kernel(x_ref, mod_ref, w_in_ref, cw_ref, w_out_ref, ln_ref,
                       o_ref, u_scr):
    tm = x_ref.shape[0]
    i = pl.program_id(1)
    mod = mod_ref[...]
    sh1, sc1, g1 = mod[0:1], mod[1:2], mod[2:3]
    x = x_ref[...]
    h = (x * (1.0 + sc1) + sh1).astype(BF16)
    cgate = _bdot(h, w_in_ref[:, D:2 * D])
    v = _bdot(h, w_in_ref[:, 2 * D:3 * D])
    u = cgate * v

    @pl.when(i == 0)
    def _():
        u_scr[0:SUBLANES, :] = jnp.zeros((SUBLANES, D), F32)

    u_scr[SUBLANES:SUBLANES + tm, :] = u
    cw = cw_ref[...]
    conv = (cw[2:3] * u
            + cw[1:2] * u_scr[SUBLANES - 1:SUBLANES - 1 + tm, :]
            + cw[0:1] * u_scr[SUBLANES - 2:SUBLANES - 2 + tm, :])
    u_scr[0:SUBLANES, :] = u_scr[tm:tm + SUBLANES, :]
    bgate = _bdot(h, w_in_ref[:, 0:D])
    y = _bdot((bgate * conv).astype(BF16), w_out_ref[...])
    ln = ln_ref[...]
    o_ref[...] = _layer_norm(ALPHA * x + (1.0 + g1) * y, ln[0:1], ln[1:2])


def _conv_mixer(x, mod, w_in, conv_w, w_out, ln):
    batch, seq, _ = x.shape
    tm = TOKEN_TILE
    const = lambda b, i: (0, 0)
    return pl.pallas_call(
        _conv_mixer_kernel,
        grid=(batch, seq // tm),
        in_specs=[pl.BlockSpec((None, tm, D), lambda b, i: (b, i, 0)),
                  pl.BlockSpec((None, SUBLANES, D), lambda b, i: (b, 0, 0)),
                  pl.BlockSpec((D, 3 * D), const),
                  pl.BlockSpec((SUBLANES, D), const),
                  pl.BlockSpec((D, D), const),
                  pl.BlockSpec((2, D), const)],
        out_specs=pl.BlockSpec((None, tm, D), lambda b, i: (b, i, 0)),
        out_shape=jax.ShapeDtypeStruct(x.shape, F32),
        scratch_shapes=[pltpu.VMEM((tm + SUBLANES, D), F32)],
        compiler_params=_cparams(2),
        name="conv_mixer",
    )(x, mod, w_in, conv_w, w_out, ln)


def _route(logits_t, rbias_col):
    aff = jax.nn.sigmoid(logits_t)
    sel = aff + rbias_col
    rows = [sel[e:e + 1, :] for e in range(N_EXPERTS)]
    scores = []
    for g in range(N_EXPERT_GROUPS):
        s0, s1, s2, s3 = rows[EXPERTS_PER_GROUP * g:EXPERTS_PER_GROUP * (g + 1)]
        a, b = jnp.maximum(s0, s1), jnp.minimum(s0, s1)
        c, d = jnp.maximum(s2, s3), jnp.minimum(s2, s3)
        scores.append(jnp.maximum(a, c) + jnp.maximum(jnp.minimum(a, c), jnp.maximum(b, d)))
    best = scores[0]
    best_group = jnp.zeros(best.shape, jnp.int32)
    for g in range(1, N_EXPERT_GROUPS):
        upd = scores[g] > best
        best_group = jnp.where(upd, g, best_group)
        best = jnp.where(upd, scores[g], best)
    e_iota = lax.broadcasted_iota(jnp.int32, sel.shape, 0)
    masked = jnp.where((e_iota // EXPERTS_PER_GROUP) == best_group, sel, NEG_INF)
    m1 = jnp.max(masked, axis=0, keepdims=True)
    idx1 = jnp.min(jnp.where(masked == m1, e_iota, N_EXPERTS), axis=0, keepdims=True)
    masked2 = jnp.where(e_iota == idx1, -jnp.inf, masked)
    m2 = jnp.max(masked2, axis=0, keepdims=True)
    idx2 = jnp.min(jnp.where(masked2 == m2, e_iota, N_EXPERTS), axis=0, keepdims=True)
    a1 = jnp.sum(jnp.where(e_iota == idx1, aff, 0.0), axis=0, keepdims=True)
    a2 = jnp.sum(jnp.where(e_iota == idx2, aff, 0.0), axis=0, keepdims=True)
    den = a1 + a2
    comb = (jnp.where(e_iota == idx1, a1 / den, 0.0)
            + jnp.where(e_iota == idx2, a2 / den, 0.0))
    group1 = idx1 // EXPERTS_PER_GROUP
    split = group1 != idx2 // EXPERTS_PER_GROUP
    return comb, group1, split


def _moe_kernel(x_ref, mod_ref, rw_ref, rb_ref, tri_ref, wgu_ref, wd_ref,
                ln_ref, o_ref, hs_scr, cs_scr, ys_scr):
    tm = x_ref.shape[0]
    gff = EXPERTS_PER_GROUP * EXPERT_FF
    mod = mod_ref[...]
    sh2, sc2, g2 = mod[3:4], mod[4:5], mod[5:6]
    x = x_ref[...]
    h = x * (1.0 + sc2) + sh2
    hb = h.astype(BF16)
    logits_t = lax.dot_general(rw_ref[...], hb, (((1,), (1,)), ((), ())),
                               preferred_element_type=F32)
    comb_t, group, split = _route(logits_t, rb_ref[...])

    g_iota = lax.broadcasted_iota(jnp.int32, (SUBLANES, tm), 0)
    member = jnp.where(g_iota == group, 1.0, 0.0)
    rank = _bdot(member.astype(BF16), tri_ref[...])
    counts = jnp.sum(member, axis=1, keepdims=True)
    cnt = [counts[g, 0] for g in range(N_EXPERT_GROUPS)]
    start = [0.0]
    for g in range(N_EXPERT_GROUPS - 1):
        start.append(start[g] + cnt[g])
    dest = sum(member[g:g + 1] * (start[g] + rank[g:g + 1]) for g in range(N_EXPERT_GROUPS))
    row = lax.broadcasted_iota(jnp.int32, (tm, tm), 0).astype(F32)
    col = lax.broadcasted_iota(jnp.int32, (tm, tm), 1).astype(F32)
    perm = jnp.where(row == dest, 1.0, 0.0).astype(BF16)
    dest_col = jnp.broadcast_to(dest, (LANES, tm)).T
    dest_col = jnp.concatenate([dest_col] * (tm // LANES), axis=1)
    perm_t = jnp.where(dest_col == col, 1.0, 0.0).astype(BF16)

    hs_scr[0:tm, :] = _bdot(perm, hb).astype(BF16)
    hs_scr[tm:, :] = jnp.zeros((MOE_WIN, D), BF16)
    cs_scr[tm:, :] = jnp.zeros((MOE_WIN, LANES), F32)
    comb_hi = comb_t.astype(BF16)
    comb_lo = (comb_t - comb_hi.astype(F32)).astype(BF16)
    cs = _bdot(jnp.concatenate([comb_hi, comb_lo], axis=0), perm_t)
    cs = cs[0:N_EXPERTS] + cs[N_EXPERTS:2 * N_EXPERTS]
    cs_scr[0:tm, :] = jnp.concatenate(
        [cs, jnp.zeros((LANES - N_EXPERTS, tm), F32)], axis=0).T
    ys_scr[...] = jnp.zeros(ys_scr.shape, F32)

    any_split = jnp.max(jnp.where(split, 1.0, 0.0)) > 0.0
    for g in range(N_EXPERT_GROUPS):
        first = jnp.where(any_split, 0, jnp.asarray(start[g], F32).astype(jnp.int32))
        last = jnp.where(any_split, tm, (start[g] + cnt[g]).astype(jnp.int32))
        first = (first // BF16_ROWS) * BF16_ROWS
        n_win = jnp.where(last > first, (last - first + MOE_WIN - 1) // MOE_WIN, 0)
        n_win = jnp.where(jnp.logical_or(cnt[g] > 0.0, any_split), n_win, 0)

        def window(w, carry, g=g, first=first):
            rows = pl.ds(pl.multiple_of(first + w * MOE_WIN, BF16_ROWS), MOE_WIN)
            hs = hs_scr[rows, :]
            parts = []
            for j in range(EXPERTS_PER_GROUP):
                e = g * EXPERTS_PER_GROUP + j
                gu = _bdot(hs, wgu_ref[e])
                gate, up = gu[:, :EXPERT_FF], gu[:, EXPERT_FF:]
                he = gate * jax.nn.sigmoid(gate) * up * cs_scr[rows, e:e + 1]
                parts.append(he.astype(BF16))
            he = jnp.concatenate(parts, axis=1)
            ys_scr[rows, :] += _bdot(he, wd_ref[g * gff:(g + 1) * gff, :])
            return carry

        lax.fori_loop(0, n_win, window, 0)

    y = _bdot(perm_t, ys_scr[0:tm, :].astype(BF16))
    ln = ln_ref[...]
    o_ref[...] = _layer_norm(ALPHA * x + (1.0 + g2) * y, ln[0:1], ln[1:2])


def _moe(x, mod, rw, rbias, wgu, wd, ln):
    batch, seq, _ = x.shape
    tm = TOKEN_TILE
    const2 = lambda b, i: (0, 0)
    tri = (jnp.arange(tm)[:, None] < jnp.arange(tm)[None, :]).astype(BF16)
    return pl.pallas_call(
        _moe_kernel,
        grid=(batch, seq // tm),
        in_specs=[pl.BlockSpec((None, tm, D), lambda b, i: (b, i, 0)),
                  pl.BlockSpec((None, SUBLANES, D), lambda b, i: (b, 0, 0)),
                  pl.BlockSpec((N_EXPERTS, D), const2),
                  pl.BlockSpec((N_EXPERTS, 1), const2),
                  pl.BlockSpec((tm, tm), const2),
                  pl.BlockSpec((N_EXPERTS, D, 2 * EXPERT_FF), lambda b, i: (0, 0, 0)),
                  pl.BlockSpec((N_EXPERTS * EXPERT_FF, D), const2),
                  pl.BlockSpec((2, D), const2)],
        out_specs=pl.BlockSpec((None, tm, D), lambda b, i: (b, i, 0)),
        out_shape=jax.ShapeDtypeStruct(x.shape, F32),
        scratch_shapes=[pltpu.VMEM((tm + MOE_WIN, D), BF16),
                        pltpu.VMEM((tm + MOE_WIN, LANES), F32),
                        pltpu.VMEM((tm + MOE_WIN, D), F32)],
        compiler_params=_cparams(2),
        name="moe",
    )(x, mod, rw, rbias, tri, wgu, wd, ln)


def _qkv_kernel(x_ref, mod_ref, kvmod_ref, wq_ref, wkv_ref, *refs):
    outs, (slab, hq_scr, hkv_scr) = refs[:3 * N_GROUPS], refs[3 * N_GROUPS:]
    tm = x_ref.shape[0]
    mod = mod_ref[...]
    kvmod = kvmod_ref[...]
    q_sc, q_sh = 1.0 + mod[1:2], mod[0:1]
    kv_sc, kv_sh = 1.0 + kvmod[1:2], kvmod[0:1]
    n_slabs = D // LANES
    for j in range(n_slabs):
        slab[j] = x_ref[:, j * LANES:(j + 1) * LANES]
    scale = HEAD_DIM ** -0.5 * LOG2E
    for g, (_, dil) in enumerate(DIL_GROUPS):
        n = tm // dil
        for r in range(dil):
            rows = slice(r * n, (r + 1) * n)
            for j in range(n_slabs):
                ls = slice(j * LANES, (j + 1) * LANES)
                piece = slab[j] if dil == 1 else slab[j, pl.ds(r, n, stride=dil), :]
                hq_scr[rows, ls] = (piece * q_sc[:, ls] + q_sh[:, ls]).astype(BF16)
                hkv_scr[rows, ls] = (piece * kv_sc[:, ls] + kv_sh[:, ls]).astype(BF16)
        hq = hq_scr[...]
        hkv = hkv_scr[...]
        cols = slice(g * D, (g + 1) * D)
        vcols = slice(QW + g * D, QW + (g + 1) * D)
        q = (_bdot(hq, wq_ref[:, cols]) * scale).astype(BF16)
        k = _bdot(hkv, wkv_ref[:, cols]).astype(BF16)
        v = _bdot(hkv, wkv_ref[:, vcols]).astype(BF16)
        for r in range(dil):
            rows = slice(r * n, (r + 1) * n)
            outs[3 * g][r] = q[rows]
            outs[3 * g + 1][r] = k[rows]
            outs[3 * g + 2][r] = v[rows]


def _qkv(x, mod, kvmod, wq, wkv):
    batch, seq, _ = x.shape
    tm = TOKEN_TILE
    const = lambda b, i: (0, 0)
    out_specs, out_shape = [], []
    for _, dil in DIL_GROUPS:
        for _ in range(3):
            out_specs.append(pl.BlockSpec((None, dil, tm // dil, D), lambda b, i: (b, 0, i, 0)))
            out_shape.append(jax.ShapeDtypeStruct((batch, dil, seq // dil, D), BF16))
    return pl.pallas_call(
        _qkv_kernel,
        grid=(batch, seq // tm),
        in_specs=[pl.BlockSpec((None, tm, D), lambda b, i: (b, i, 0)),
                  pl.BlockSpec((None, SUBLANES, D), lambda b, i: (b, 0, 0)),
                  pl.BlockSpec((None, SUBLANES, D), lambda b, i: (b, 0, 0)),
                  pl.BlockSpec((D, QW), const),
                  pl.BlockSpec((D, 2 * QW), const)],
        out_specs=out_specs,
        out_shape=out_shape,
        scratch_shapes=[pltpu.VMEM((D // LANES, tm, LANES), F32),
                        pltpu.VMEM((tm, D), BF16),
                        pltpu.VMEM((tm, D), BF16)],
        compiler_params=_cparams(2),
        name="qkv_proj",
    )(x, mod, kvmod, wq, wkv)


def _bucket_tables():
    qi = jnp.arange(BLOCK)[:, None]
    kj = jnp.arange(2 * BLOCK)[None, :]
    dist = BLOCK + qi - kj
    max_exact = NUM_BUCKETS // 2
    tables = []
    for window, dil in DIL_GROUPS:
        n = jnp.maximum(dist, 0) * dil
        nf = jnp.maximum(n, 1).astype(F32)
        large = max_exact + (jnp.log(nf / max_exact) / math.log(MAX_DISTANCE / max_exact)
                             * (NUM_BUCKETS - max_exact)).astype(jnp.int32)
        large = jnp.minimum(large, NUM_BUCKETS - 1)
        bucket = jnp.where(n < max_exact, n, large)
        valid = (dist >= 0) & (dist <= window // dil)
        tables.append(jnp.where(valid, bucket, -1).astype(jnp.int32))
    return jnp.stack(tables)


def _bias_kernel(rb_ref, bkt_ref, o_ref):
    g = pl.program_id(0)
    h = pl.program_id(1)
    bkt = bkt_ref[...]
    acc = jnp.full(bkt.shape, NEG_INF, F32)
    for b in range(NUM_BUCKETS):
        acc = jnp.where(bkt == b, rb_ref[b, g * HEADS + h] * LOG2E, acc)
    o_ref[...] = acc


def _bias_tiles(rel_bias):
    return pl.pallas_call(
        _bias_kernel,
        grid=(N_GROUPS, HEADS),
        in_specs=[pl.BlockSpec(memory_space=pltpu.SMEM),
                  pl.BlockSpec((None, BLOCK, 2 * BLOCK), lambda g, h: (g, 0, 0))],
        out_specs=pl.BlockSpec((None, None, BLOCK, 2 * BLOCK), lambda g, h: (g, h, 0, 0)),
        out_shape=jax.ShapeDtypeStruct((N_GROUPS, HEADS, BLOCK, 2 * BLOCK), F32),
        compiler_params=_cparams(2),
        name="rel_bias_tiles",
    )(rel_bias, _bucket_tables())


def _attn_kernel(q_ref, kp_ref, kc_ref, vp_ref, vc_ref, bias_ref, o_ref, lse_ref):
    n = pl.program_id(2)
    col = lax.broadcasted_iota(jnp.int32, (BLOCK, 2 * BLOCK), 1)
    no_prev = jnp.logical_and(n == 0, col < BLOCK)
    lane = lax.broadcasted_iota(jnp.int32, (BLOCK, LANES), 1)
    ones = jnp.ones((2 * BLOCK, HEAD_DIM), BF16)
    for j in range(q_ref.shape[0] // BLOCK):
        rows = slice(j * BLOCK, (j + 1) * BLOCK)
        window = slice((j - 1) * BLOCK, (j + 1) * BLOCK)
        lse_tile = jnp.zeros((BLOCK, LANES), F32)
        for h in range(HEADS):
            hs = slice(h * HEAD_DIM, (h + 1) * HEAD_DIM)
            if j == 0:
                k = jnp.concatenate([kp_ref[:, hs], kc_ref[rows, hs]], axis=0)
                v = jnp.concatenate([vp_ref[:, hs], vc_ref[rows, hs]], axis=0)
            else:
                k = kc_ref[window, hs]
                v = vc_ref[window, hs]
            s = lax.dot_general(q_ref[rows, hs], k, (((1,), (1,)), ((), ())),
                                preferred_element_type=F32) + bias_ref[h]
            if j == 0:
                s = jnp.where(no_prev, NEG_INF, s)
            m = jnp.max(s, axis=-1, keepdims=True)
            p = jnp.exp2(s - m)
            o_ext = _bdot(p.astype(BF16), jnp.concatenate([v, ones], axis=1))
            o, l = o_ext[:, :HEAD_DIM], o_ext[:, HEAD_DIM:]
            o_ref[rows, hs] = (o / l).astype(BF16)
            lse_tile = jnp.where(lane == h, m * LN2 + jnp.log(l), lse_tile)
        lse_ref[rows, :] = lse_tile


def _attention_group(g, q, k, v, bias):
    batch, dil, sub_len, _ = q.shape
    nq = ATTN_QBLOCKS
    cur = lambda b, r, n: (b, r, n, 0)
    prev = lambda b, r, n: (b, r, jnp.maximum(n * nq - 1, 0), 0)
    blk = (None, None, nq * BLOCK, D)
    blk_prev = (None, None, BLOCK, D)
    return pl.pallas_call(
        _attn_kernel,
        grid=(batch, dil, sub_len // (nq * BLOCK)),
        in_specs=[
            pl.BlockSpec(blk, cur),
            pl.BlockSpec(blk_prev, prev),
            pl.BlockSpec(blk, cur),
            pl.BlockSpec(blk_prev, prev),
            pl.BlockSpec(blk, cur),
            pl.BlockSpec((None, HEADS, BLOCK, 2 * BLOCK), lambda b, r, n: (g, 0, 0, 0)),
        ],
        out_specs=[pl.BlockSpec(blk, cur),
                   pl.BlockSpec((None, None, nq * BLOCK, LANES), cur)],
        out_shape=[jax.ShapeDtypeStruct((batch, dil, sub_len, D), BF16),
                   jax.ShapeDtypeStruct((batch, dil, sub_len, LANES), F32)],
        compiler_params=_cparams(3),
        name=f"dilated_attn_g{g}",
    )(q, k, k, v, v, bias)


def _mix_kernel(o0_ref, o1_ref, o2_ref, l0_ref, l1_ref, l2_ref, x_ref, mod_ref,
                expand_ref, wo_ref, ln_ref, out_ref, lse_scr, o_slab):
    tm = x_ref.shape[0]
    n_slabs = D // LANES
    lses = []
    for g, l_ref in enumerate((l0_ref, l1_ref, l2_ref)):
        dil = DIL_GROUPS[g][1]
        if dil == 1:
            lses.append(l_ref[0])
            continue
        for r in range(dil):
            lse_scr[g, pl.ds(r, tm // dil, stride=dil), :] = l_ref[r]
        lses.append(lse_scr[g])
    m = jnp.maximum(lses[0], jnp.maximum(lses[1], lses[2]))
    es = [jnp.exp(a - m) for a in lses]
    den = es[0] + es[1] + es[2]
    expand = expand_ref[...]
    mixed = None
    for g, o_ref in enumerate((o0_ref, o1_ref, o2_ref)):
        dil = DIL_GROUPS[g][1]
        w = _bdot((es[g] / den).astype(BF16), expand)
        if dil == 1:
            o = o_ref[0].astype(F32)
        else:
            for r in range(dil):
                for j in range(n_slabs):
                    o_slab[j, pl.ds(r, tm // dil, stride=dil), :] = (
                        o_ref[r, :, j * LANES:(j + 1) * LANES].astype(F32))
            o = jnp.concatenate([o_slab[j] for j in range(n_slabs)], axis=1)
        mixed = w * o if mixed is None else mixed + w * o
    y = _bdot(mixed.astype(BF16), wo_ref[...])
    g1 = mod_ref[...][2:3]
    ln = ln_ref[...]
    out_ref[...] = _layer_norm(ALPHA * x_ref[...] + (1.0 + g1) * y, ln[0:1], ln[1:2])


def _mix(outs, lses, x, mod, wo, ln):
    batch, seq, _ = x.shape
    tm = TOKEN_TILE
    const = lambda b, i: (0, 0)
    tok = pl.BlockSpec((None, tm, D), lambda b, i: (b, i, 0))
    dilated = lambda dil, w: pl.BlockSpec((None, dil, tm // dil, w), lambda b, i: (b, 0, i, 0))
    dils = [dil for _, dil in DIL_GROUPS]
    head_of_lane = jnp.arange(D)[None, :] // HEAD_DIM
    expand = (jnp.arange(LANES)[:, None] == head_of_lane).astype(BF16)
    return pl.pallas_call(
        _mix_kernel,
        grid=(batch, seq // tm),
        in_specs=[dilated(d, D) for d in dils] + [dilated(d, LANES) for d in dils] + [
            tok,
            pl.BlockSpec((None, SUBLANES, D), lambda b, i: (b, 0, 0)),
            pl.BlockSpec((LANES, D), const),
            pl.BlockSpec((D, D), const),
            pl.BlockSpec((2, D), const)],
        out_specs=tok,
        out_shape=jax.ShapeDtypeStruct(x.shape, F32),
        scratch_shapes=[pltpu.VMEM((N_GROUPS, tm, LANES), F32),
                        pltpu.VMEM((D // LANES, tm, LANES), F32)],
        compiler_params=_cparams(2),
        name="attn_mix",
    )(*outs, *lses, x, mod, expand, wo, ln)


def kernel(x, c, ada_w, ada_b, ln_g, ln_b, conv_w_in, conv_w, conv_w_out, kv_ada_w,
           kv_ada_b, w_kv, attn_w_q, attn_w_o, rel_bias, router_w, router_bias,
           moe_w_gate, moe_w_up, moe_w_down):
    batch, seq, _ = x.shape
    assert x.shape[2] == D and batch <= SUBLANES
    assert seq % TOKEN_TILE == 0 and seq % (ATTN_QBLOCKS * BLOCK * DIL_GROUPS[-1][1]) == 0

    c8 = jnp.pad(c, ((0, SUBLANES - batch), (0, 0)))
    mods = _ada_linear(c8, ada_w, ada_b)
    kvmods = _ada_linear(c8, kv_ada_w[None], kv_ada_b[None])[0]
    mod0 = _mod_rows(mods[0], batch, 6)
    mod1 = _mod_rows(mods[1], batch, 6)
    kvmod = _mod_rows(kvmods, batch, 2)
    ln = jnp.stack([ln_g, ln_b], axis=2)

    rw = router_w.T.astype(BF16)
    rbias = router_bias.reshape(N_EXPERTS, 1)
    wgu = jnp.concatenate([moe_w_gate, moe_w_up], axis=-1).astype(BF16)
    wd = moe_w_down.astype(BF16).reshape(DEPTH, N_EXPERTS * EXPERT_FF, D)
    conv_w8 = jnp.pad(conv_w[0], ((0, SUBLANES - CONV_WIDTH), (0, 0)))

    x = _conv_mixer(x, mod0, conv_w_in[0].astype(BF16), conv_w8,
                    conv_w_out[0].astype(BF16), ln[0, 0])
    x = _moe(x, mod0, rw, rbias, wgu[0], wd[0], ln[0, 1])

    qkv = _qkv(x, mod1, kvmod, attn_w_q[0].astype(BF16), w_kv.astype(BF16))
    bias = _bias_tiles(rel_bias)
    outs, lses = zip(*[_attention_group(g, *qkv[3 * g:3 * g + 3], bias)
                       for g in range(N_GROUPS)])
    x = _mix(outs, lses, x, mod1, attn_w_o[0].astype(BF16), ln[1, 0])
    x = _moe(x, mod1, rw, rbias, wgu[1], wd[1], ln[1, 1])
    return x
```

```python
import functools
import math

import jax
import jax.numpy as jnp
from jax import lax
from jax.experimental import pallas as pl
from jax.experimental.pallas import tpu as pltpu

F32 = jnp.float32
BF16 = jnp.bfloat16

D = 1024
DEPTH = 2
CONV_WIDTH = 3
DIL_GROUPS = ((128, 1), (512, 4), (2048, 16))
N_GROUPS = len(DIL_GROUPS)
HEAD_DIM = 128
HEADS = D // HEAD_DIM
QW = N_GROUPS * D
BLOCK = 128
NUM_BUCKETS = 32
MAX_DISTANCE = 2048
N_EXPERTS = 16
EXPERTS_PER_GROUP = 4
N_EXPERT_GROUPS = N_EXPERTS // EXPERTS_PER_GROUP
EXPERT_FF = D // 4
ALPHA = (2 * DEPTH) ** 0.25
LN_EPS = 1e-5
NEG_INF = -1e30
LOG2E = math.log2(math.e)
LN2 = math.log(2.0)

LANES = 128
SUBLANES = 8
VMEM_LIMIT = 56 * 1024 * 1024
TOKEN_TILE = 512
MOE_WIN_BIG = 256
MOE_WIN_SMALL = 128
BF16_ROWS = 16
ATTN_QBLOCKS = 4
ADA_COL_TILE = 2048


def _cparams(n_axes):
    return pltpu.CompilerParams(
        dimension_semantics=("arbitrary",) * n_axes, vmem_limit_bytes=VMEM_LIMIT)


def _layer_norm(r, g, b):
    mu = jnp.mean(r, axis=-1, keepdims=True)
    c = r - mu
    var = jnp.mean(c * c, axis=-1, keepdims=True)
    return c * lax.rsqrt(var + LN_EPS) * g + b


def _bdot(a, b):
    return jnp.dot(a, b, preferred_element_type=F32)


def _ada_kernel(c_ref, w_ref, b_ref, o_ref):
    c = c_ref[...]
    cond = c * jax.nn.sigmoid(c)
    o_ref[...] = jnp.dot(cond, w_ref[...], preferred_element_type=F32,
                         precision=lax.Precision.HIGHEST) + b_ref[...]


def _ada_linear(c8, w, bias):
    n_layers, _, n = w.shape
    nt = ADA_COL_TILE
    return pl.pallas_call(
        _ada_kernel,
        grid=(n_layers, n // nt),
        in_specs=[pl.BlockSpec((SUBLANES, D), lambda l, j: (0, 0)),
                  pl.BlockSpec((None, D, nt), lambda l, j: (l, 0, j)),
                  pl.BlockSpec((None, 1, nt), lambda l, j: (l, 0, j))],
        out_specs=pl.BlockSpec((None, SUBLANES, nt), lambda l, j: (l, 0, j)),
        out_shape=jax.ShapeDtypeStruct((n_layers, SUBLANES, n), F32),
        compiler_params=_cparams(2),
        name="ada_linear",
    )(c8, w, bias.reshape(n_layers, 1, n))


def _mod_rows(mods, batch, n_vec):
    m = mods[:batch].reshape(batch, n_vec, D)
    return jnp.pad(m, ((0, 0), (0, SUBLANES - n_vec), (0, 0)))


def _conv_mixer_kernel(x_ref, mod_ref, w_in_ref, cw_ref, w_out_ref, ln_ref,
                       o_ref, u_scr):
    tm = x_ref.shape[0]
    i = pl.program_id(1)
    mod = mod_ref[...]
    sh1, sc1, g1 = mod[0:1], mod[1:2], mod[2:3]
    x = x_ref[...]
    h = (x * (1.0 + sc1) + sh1).astype(BF16)
    cgate = _bdot(h, w_in_ref[:, D:2 * D])
    v = _bdot(h, w_in_ref[:, 2 * D:3 * D])
    u = cgate * v

    @pl.when(i == 0)
    def _():
        u_scr[0:SUBLANES, :] = jnp.zeros((SUBLANES, D), F32)

    u_scr[SUBLANES:SUBLANES + tm, :] = u
    cw = cw_ref[...]
    conv = (cw[2:3] * u
            + cw[1:2] * u_scr[SUBLANES - 1:SUBLANES - 1 + tm, :]
            + cw[0:1] * u_scr[SUBLANES - 2:SUBLANES - 2 + tm, :])
    u_scr[0:SUBLANES, :] = u_scr[tm:tm + SUBLANES, :]
    bgate = _bdot(h, w_in_ref[:, 0:D])
    y = _bdot((bgate * conv).astype(BF16), w_out_ref[...])
    ln = ln_ref[...]
    o_ref[...] = _layer_norm(ALPHA * x + (1.0 + g1) * y, ln[0:1], ln[1:2])


def _conv_mixer(x, mod, w_in, conv_w, w_out, ln):
    batch, seq, _ = x.shape
    tm = TOKEN_TILE
    const = lambda b, i: (0, 0)
    return pl.pallas_call(
        _conv_mixer_kernel,
        grid=(batch, seq // tm),
        in_specs=[pl.BlockSpec((None, tm, D), lambda b, i: (b, i, 0)),
                  pl.BlockSpec((None, SUBLANES, D), lambda b, i: (b, 0, 0)),
                  pl.BlockSpec((D, 3 * D), const),
                  pl.BlockSpec((SUBLANES, D), const),
                  pl.BlockSpec((D, D), const),
                  pl.BlockSpec((2, D), const)],
        out_specs=pl.BlockSpec((None, tm, D), lambda b, i: (b, i, 0)),
        out_shape=jax.ShapeDtypeStruct(x.shape, F32),
        scratch_shapes=[pltpu.VMEM((tm + SUBLANES, D), F32)],
        compiler_params=_cparams(2),
        name="conv_mixer",
    )(x, mod, w_in, conv_w, w_out, ln)


def _route(logits_t, rbias_col):
    aff = jax.nn.sigmoid(logits_t)
    sel = aff + rbias_col
    rows = [sel[e:e + 1, :] for e in range(N_EXPERTS)]
    scores = []
    for g in range(N_EXPERT_GROUPS):
        s0, s1, s2, s3 = rows[EXPERTS_PER_GROUP * g:EXPERTS_PER_GROUP * (g + 1)]
        a, b = jnp.maximum(s0, s1), jnp.minimum(s0, s1)
        c, d = jnp.maximum(s2, s3), jnp.minimum(s2, s3)
        scores.append(jnp.maximum(a, c) + jnp.maximum(jnp.minimum(a, c), jnp.maximum(b, d)))
    best = scores[0]
    best_group = jnp.zeros(best.shape, jnp.int32)
    for g in range(1, N_EXPERT_GROUPS):
        upd = scores[g] > best
        best_group = jnp.where(upd, g, best_group)
        best = jnp.where(upd, scores[g], best)
    e_iota = lax.broadcasted_iota(jnp.int32, sel.shape, 0)
    masked = jnp.where((e_iota // EXPERTS_PER_GROUP) == best_group, sel, NEG_INF)
    m1 = jnp.max(masked, axis=0, keepdims=True)
    idx1 = jnp.min(jnp.where(masked == m1, e_iota, N_EXPERTS), axis=0, keepdims=True)
    masked2 = jnp.where(e_iota == idx1, -jnp.inf, masked)
    m2 = jnp.max(masked2, axis=0, keepdims=True)
    idx2 = jnp.min(jnp.where(masked2 == m2, e_iota, N_EXPERTS), axis=0, keepdims=True)
    a1 = jnp.sum(jnp.where(e_iota == idx1, aff, 0.0), axis=0, keepdims=True)
    a2 = jnp.sum(jnp.where(e_iota == idx2, aff, 0.0), axis=0, keepdims=True)
    den = a1 + a2
    comb = (jnp.where(e_iota == idx1, a1 / den, 0.0)
            + jnp.where(e_iota == idx2, a2 / den, 0.0))
    group1 = idx1 // EXPERTS_PER_GROUP
    split = group1 != idx2 // EXPERTS_PER_GROUP
    return comb, group1, split


def _moe_kernel(x_ref, xp_ref, mod_ref, modp_ref, rw_ref, rb_ref, tri_ref, wgu_ref, wd_ref,
                ln_ref, o_ref, hs_scr, cs_scr, ys_scr, pt_scr):
    tm = x_ref.shape[0]
    gff = EXPERTS_PER_GROUP * EXPERT_FF
    t = pl.program_id(0)

    @pl.when(t == 0)
    def _():
        ys_scr[...] = jnp.zeros(ys_scr.shape, F32)
        pt_scr[...] = jnp.zeros(pt_scr.shape, BF16)

    y = _bdot(pt_scr[...], ys_scr[0:tm, :].astype(BF16))
    ln = ln_ref[...]
    o_ref[...] = _layer_norm(ALPHA * xp_ref[...] + (1.0 + modp_ref[...][5:6]) * y,
                             ln[0:1], ln[1:2])

    mod = mod_ref[...]
    sh2, sc2 = mod[3:4], mod[4:5]
    h = x_ref[...] * (1.0 + sc2) + sh2
    hb = h.astype(BF16)
    logits_t = lax.dot_general(rw_ref[...], hb, (((1,), (1,)), ((), ())),
                               preferred_element_type=F32)
    comb_t, group, split = _route(logits_t, rb_ref[...])

    g_iota = lax.broadcasted_iota(jnp.int32, (SUBLANES, tm), 0)
    member = jnp.where(g_iota == group, 1.0, 0.0)
    rank = _bdot(member.astype(BF16), tri_ref[...])
    counts = jnp.sum(member, axis=1, keepdims=True)
    cnt = [counts[g, 0] for g in range(N_EXPERT_GROUPS)]
    start = [0.0]
    for g in range(N_EXPERT_GROUPS - 1):
        start.append(start[g] + cnt[g])
    dest = sum(member[g:g + 1] * (start[g] + rank[g:g + 1]) for g in range(N_EXPERT_GROUPS))
    row = lax.broadcasted_iota(jnp.int32, (tm, tm), 0).astype(F32)
    col = lax.broadcasted_iota(jnp.int32, (tm, tm), 1).astype(F32)
    perm = jnp.where(row == dest, 1.0, 0.0).astype(BF16)
    dest_col = jnp.broadcast_to(dest, (LANES, tm)).T
    dest_col = jnp.concatenate([dest_col] * (tm // LANES), axis=1)
    perm_t = jnp.where(dest_col == col, 1.0, 0.0).astype(BF16)
    pt_scr[...] = perm_t

    hs_scr[0:tm, :] = _bdot(perm, hb).astype(BF16)
    hs_scr[tm:, :] = jnp.zeros((MOE_WIN_BIG, D), BF16)
    cs_scr[tm:, :] = jnp.zeros((MOE_WIN_BIG, LANES), F32)
    comb_hi = comb_t.astype(BF16)
    comb_lo = (comb_t - comb_hi.astype(F32)).astype(BF16)
    cs = _bdot(jnp.concatenate([comb_hi, comb_lo], axis=0), perm_t)
    cs = cs[0:N_EXPERTS] + cs[N_EXPERTS:2 * N_EXPERTS]
    cs_scr[0:tm, :] = jnp.concatenate(
        [cs, jnp.zeros((LANES - N_EXPERTS, tm), F32)], axis=0).T
    ys_scr[...] = jnp.zeros(ys_scr.shape, F32)

    any_split = jnp.max(jnp.where(split, 1.0, 0.0)) > 0.0
    active = t < pl.num_programs(0) - 1
    for g in range(N_EXPERT_GROUPS):
        first = jnp.where(any_split, 0, jnp.asarray(start[g], F32).astype(jnp.int32))
        last = jnp.where(any_split, tm, (start[g] + cnt[g]).astype(jnp.int32))
        first = (first // BF16_ROWS) * BF16_ROWS
        need = jnp.where(jnp.logical_and(active, last > first), last - first, 0)
        n_big = need // MOE_WIN_BIG
        rest = need - n_big * MOE_WIN_BIG
        n_big = n_big + jnp.where(rest > MOE_WIN_SMALL, 1, 0)
        n_small = jnp.where(jnp.logical_and(rest > 0, rest <= MOE_WIN_SMALL), 1, 0)

        def window(start_row, size, g=g):
            rows = pl.ds(pl.multiple_of(start_row, BF16_ROWS), size)
            hs = hs_scr[rows, :]
            parts = []
            for j in range(EXPERTS_PER_GROUP):
                e = g * EXPERTS_PER_GROUP + j
                gu = _bdot(hs, wgu_ref[e])
                gate, up = gu[:, :EXPERT_FF], gu[:, EXPERT_FF:]
                he = gate * jax.nn.sigmoid(gate) * up * cs_scr[rows, e:e + 1]
                parts.append(he.astype(BF16))
            he = jnp.concatenate(parts, axis=1)
            ys_scr[rows, :] += _bdot(he, wd_ref[g * gff:(g + 1) * gff, :])

        def big(w, carry, first=first, window=window):
            window(first + w * MOE_WIN_BIG, MOE_WIN_BIG)
            return carry

        def small(w, carry, first=first, n_big=n_big, window=window):
            window(first + n_big * MOE_WIN_BIG, MOE_WIN_SMALL)
            return carry

        lax.fori_loop(0, n_big, big, 0)
        lax.fori_loop(0, n_small, small, 0)


def _moe(x, mod, rw, rbias, wgu, wd, ln):
    batch, seq, _ = x.shape
    tm = TOKEN_TILE
    per_batch = seq // tm
    n_tiles = batch * per_batch
    const2 = lambda t: (0, 0)
    cur = lambda t: jnp.minimum(t, n_tiles - 1)
    prev = lambda t: jnp.maximum(t - 1, 0)
    tile = lambda which: pl.BlockSpec(
        (None, tm, D), lambda t: (which(t) // per_batch, which(t) % per_batch, 0))
    mods = lambda which: pl.BlockSpec(
        (None, SUBLANES, D), lambda t: (which(t) // per_batch, 0, 0))
    tri = (jnp.arange(tm)[:, None] < jnp.arange(tm)[None, :]).astype(BF16)
    return pl.pallas_call(
        _moe_kernel,
        grid=(n_tiles + 1,),
        in_specs=[tile(cur), tile(prev), mods(cur), mods(prev),
                  pl.BlockSpec((N_EXPERTS, D), const2),
                  pl.BlockSpec((N_EXPERTS, 1), const2),
                  pl.BlockSpec((tm, tm), const2),
                  pl.BlockSpec((N_EXPERTS, D, 2 * EXPERT_FF), lambda t: (0, 0, 0)),
                  pl.BlockSpec((N_EXPERTS * EXPERT_FF, D), const2),
                  pl.BlockSpec((2, D), const2)],
        out_specs=tile(prev),
        out_shape=jax.ShapeDtypeStruct(x.shape, F32),
        scratch_shapes=[pltpu.VMEM((tm + MOE_WIN_BIG, D), BF16),
                        pltpu.VMEM((tm + MOE_WIN_BIG, LANES), F32),
                        pltpu.VMEM((tm + MOE_WIN_BIG, D), F32),
                        pltpu.VMEM((tm, tm), BF16)],
        compiler_params=_cparams(1),
        name="moe",
    )(x, x, mod, mod, rw, rbias, tri, wgu, wd, ln)


def _qkv_kernel(x_ref, mod_ref, kvmod_ref, wq_ref, wkv_ref, *refs):
    outs, (slab, hq_scr, hkv_scr) = refs[:3 * N_GROUPS], refs[3 * N_GROUPS:]
    tm = x_ref.shape[0]
    mod = mod_ref[...]
    kvmod = kvmod_ref[...]
    q_sc, q_sh = 1.0 + mod[1:2], mod[0:1]
    kv_sc, kv_sh = 1.0 + kvmod[1:2], kvmod[0:1]
    n_slabs = D // LANES
    for j in range(n_slabs):
        slab[j] = x_ref[:, j * LANES:(j + 1) * LANES]
    scale = HEAD_DIM ** -0.5 * LOG2E
    for g, (_, dil) in enumerate(DIL_GROUPS):
        n = tm // dil
        for r in range(dil):
            rows = slice(r * n, (r + 1) * n)
            for j in range(n_slabs):
                ls = slice(j * LANES, (j + 1) * LANES)
                piece = slab[j] if dil == 1 else slab[j, pl.ds(r, n, stride=dil), :]
                hq_scr[rows, ls] = (piece * q_sc[:, ls] + q_sh[:, ls]).astype(BF16)
                hkv_scr[rows, ls] = (piece * kv_sc[:, ls] + kv_sh[:, ls]).astype(BF16)
        hq = hq_scr[...]
        hkv = hkv_scr[...]
        cols = slice(g * D, (g + 1) * D)
        vcols = slice(QW + g * D, QW + (g + 1) * D)
        q = (_bdot(hq, wq_ref[:, cols]) * scale).astype(BF16)
        k = _bdot(hkv, wkv_ref[:, cols]).astype(BF16)
        v = _bdot(hkv, wkv_ref[:, vcols]).astype(BF16)
        for r in range(dil):
            rows = slice(r * n, (r + 1) * n)
            outs[3 * g][r] = q[rows]
            outs[3 * g + 1][r] = k[rows]
            outs[3 * g + 2][r] = v[rows]


def _qkv(x, mod, kvmod, wq, wkv):
    batch, seq, _ = x.shape
    tm = TOKEN_TILE
    const = lambda b, i: (0, 0)
    out_specs, out_shape = [], []
    for _, dil in DIL_GROUPS:
        for _ in range(3):
            out_specs.append(pl.BlockSpec((None, dil, tm // dil, D), lambda b, i: (b, 0, i, 0)))
            out_shape.append(jax.ShapeDtypeStruct((batch, dil, seq // dil, D), BF16))
    return pl.pallas_call(
        _qkv_kernel,
        grid=(batch, seq // tm),
        in_specs=[pl.BlockSpec((None, tm, D), lambda b, i: (b, i, 0)),
                  pl.BlockSpec((None, SUBLANES, D), lambda b, i: (b, 0, 0)),
                  pl.BlockSpec((None, SUBLANES, D), lambda b, i: (b, 0, 0)),
                  pl.BlockSpec((D, QW), const),
                  pl.BlockSpec((D, 2 * QW), const)],
        out_specs=out_specs,
        out_shape=out_shape,
        scratch_shapes=[pltpu.VMEM((D // LANES, tm, LANES), F32),
                        pltpu.VMEM((tm, D), BF16),
                        pltpu.VMEM((tm, D), BF16)],
        compiler_params=_cparams(2),
        name="qkv_proj",
    )(x, mod, kvmod, wq, wkv)


def _bucket_tables():
    qi = jnp.arange(BLOCK)[:, None]
    kj = jnp.arange(2 * BLOCK)[None, :]
    dist = BLOCK + qi - kj
    max_exact = NUM_BUCKETS // 2
    tables = []
    for window, dil in DIL_GROUPS:
        n = jnp.maximum(dist, 0) * dil
        nf = jnp.maximum(n, 1).astype(F32)
        large = max_exact + (jnp.log(nf / max_exact) / math.log(MAX_DISTANCE / max_exact)
                             * (NUM_BUCKETS - max_exact)).astype(jnp.int32)
        large = jnp.minimum(large, NUM_BUCKETS - 1)
        bucket = jnp.where(n < max_exact, n, large)
        valid = (dist >= 0) & (dist <= window // dil)
        tables.append(jnp.where(valid, bucket, -1).astype(jnp.int32))
    return jnp.stack(tables)


def _bias_kernel(rb_ref, bkt_ref, o_ref):
    g = pl.program_id(0)
    bkt = bkt_ref[...]
    for h in range(HEADS):
        acc = jnp.full(bkt.shape, NEG_INF, F32)
        for b in range(NUM_BUCKETS):
            acc = jnp.where(bkt == b, rb_ref[b, g * HEADS + h] * LOG2E, acc)
        o_ref[h] = acc


def _bias_tiles(rel_bias):
    return pl.pallas_call(
        _bias_kernel,
        grid=(N_GROUPS,),
        in_specs=[pl.BlockSpec(memory_space=pltpu.SMEM),
                  pl.BlockSpec((None, BLOCK, 2 * BLOCK), lambda g: (g, 0, 0))],
        out_specs=pl.BlockSpec((None, HEADS, BLOCK, 2 * BLOCK), lambda g: (g, 0, 0, 0)),
        out_shape=jax.ShapeDtypeStruct((N_GROUPS, HEADS, BLOCK, 2 * BLOCK), F32),
        compiler_params=_cparams(1),
        name="rel_bias_tiles",
    )(rel_bias, _bucket_tables())


def _attn_kernel(q_ref, kp_ref, kc_ref, vp_ref, vc_ref, bias_ref, o_ref, lse_ref):
    n = pl.program_id(2)
    col = lax.broadcasted_iota(jnp.int32, (BLOCK, 2 * BLOCK), 1)
    no_prev = jnp.logical_and(n == 0, col < BLOCK)
    lane = lax.broadcasted_iota(jnp.int32, (BLOCK, LANES), 1)
    ones = jnp.ones((2 * BLOCK, HEAD_DIM), BF16)
    for j in range(q_ref.shape[0] // BLOCK):
        rows = slice(j * BLOCK, (j + 1) * BLOCK)
        window = slice((j - 1) * BLOCK, (j + 1) * BLOCK)
        lse_tile = jnp.zeros((BLOCK, LANES), F32)
        for h in range(HEADS):
            hs = slice(h * HEAD_DIM, (h + 1) * HEAD_DIM)
            if j == 0:
                k = jnp.concatenate([kp_ref[:, hs], kc_ref[rows, hs]], axis=0)
                v = jnp.concatenate([vp_ref[:, hs], vc_ref[rows, hs]], axis=0)
            else:
                k = kc_ref[window, hs]
                v = vc_ref[window, hs]
            s = lax.dot_general(q_ref[rows, hs], k, (((1,), (1,)), ((), ())),
                                preferred_element_type=F32) + bias_ref[h]
            if j == 0:
                s = jnp.where(no_prev, NEG_INF, s)
            m = jnp.max(s, axis=-1, keepdims=True)
            p = jnp.exp2(s - m)
            o_ext = _bdot(p.astype(BF16), jnp.concatenate([v, ones], axis=1))
            o, l = o_ext[:, :HEAD_DIM], o_ext[:, HEAD_DIM:]
            o_ref[rows, hs] = (o / l).astype(BF16)
            lse_tile = jnp.where(lane == h, m * LN2 + jnp.log(l), lse_tile)
        lse_ref[rows, :] = lse_tile


def _attention_group(g, q, k, v, bias):
    batch, dil, sub_len, _ = q.shape
    nq = ATTN_QBLOCKS
    cur = lambda b, r, n: (b, r, n, 0)
    prev = lambda b, r, n: (b, r, jnp.maximum(n * nq - 1, 0), 0)
    blk = (None, None, nq * BLOCK, D)
    blk_prev = (None, None, BLOCK, D)
    return pl.pallas_call(
        _attn_kernel,
        grid=(batch, dil, sub_len // (nq * BLOCK)),
        in_specs=[
            pl.BlockSpec(blk, cur),
            pl.BlockSpec(blk_prev, prev),
            pl.BlockSpec(blk, cur),
            pl.BlockSpec(blk_prev, prev),
            pl.BlockSpec(blk, cur),
            pl.BlockSpec((None, HEADS, BLOCK, 2 * BLOCK), lambda b, r, n: (g, 0, 0, 0)),
        ],
        out_specs=[pl.BlockSpec(blk, cur),
                   pl.BlockSpec((None, None, nq * BLOCK, LANES), cur)],
        out_shape=[jax.ShapeDtypeStruct((batch, dil, sub_len, D), BF16),
                   jax.ShapeDtypeStruct((batch, dil, sub_len, LANES), F32)],
        compiler_params=_cparams(3),
        name=f"dilated_attn_g{g}",
    )(q, k, k, v, v, bias)


def _mix_kernel(o0_ref, o1_ref, o2_ref, l0_ref, l1_ref, l2_ref, x_ref, mod_ref,
                expand_ref, wo_ref, ln_ref, out_ref, lse_scr, o_slab):
    tm = x_ref.shape[0]
    n_slabs = D // LANES
    lses = []
    for g, l_ref in enumerate((l0_ref, l1_ref, l2_ref)):
        dil = DIL_GROUPS[g][1]
        if dil == 1:
            lses.append(l_ref[0])
            continue
        for r in range(dil):
            lse_scr[g, pl.ds(r, tm // dil, stride=dil), :] = l_ref[r]
        lses.append(lse_scr[g])
    m = jnp.maximum(lses[0], jnp.maximum(lses[1], lses[2]))
    es = [jnp.exp(a - m) for a in lses]
    den = es[0] + es[1] + es[2]
    expand = expand_ref[...]
    mixed = None
    for g, o_ref in enumerate((o0_ref, o1_ref, o2_ref)):
        dil = DIL_GROUPS[g][1]
        w = _bdot((es[g] / den).astype(BF16), expand)
        if dil == 1:
            o = o_ref[0].astype(F32)
        else:
            for r in range(dil):
                for j in range(n_slabs):
                    o_slab[j, pl.ds(r, tm // dil, stride=dil), :] = (
                        o_ref[r, :, j * LANES:(j + 1) * LANES].astype(F32))
            o = jnp.concatenate([o_slab[j] for j in range(n_slabs)], axis=1)
        mixed = w * o if mixed is None else mixed + w * o
    y = _bdot(mixed.astype(BF16), wo_ref[...])
    g1 = mod_ref[...][2:3]
    ln = ln_ref[...]
    out_ref[...] = _layer_norm(ALPHA * x_ref[...] + (1.0 + g1) * y, ln[0:1], ln[1:2])


def _mix(outs, lses, x, mod, wo, ln):
    batch, seq, _ = x.shape
    tm = TOKEN_TILE
    const = lambda b, i: (0, 0)
    tok = pl.BlockSpec((None, tm, D), lambda b, i: (b, i, 0))
    dilated = lambda dil, w: pl.BlockSpec((None, dil, tm // dil, w), lambda b, i: (b, 0, i, 0))
    dils = [dil for _, dil in DIL_GROUPS]
    head_of_lane = jnp.arange(D)[None, :] // HEAD_DIM
    expand = (jnp.arange(LANES)[:, None] == head_of_lane).astype(BF16)
    return pl.pallas_call(
        _mix_kernel,
        grid=(batch, seq // tm),
        in_specs=[dilated(d, D) for d in dils] + [dilated(d, LANES) for d in dils] + [
            tok,
            pl.BlockSpec((None, SUBLANES, D), lambda b, i: (b, 0, 0)),
            pl.BlockSpec((LANES, D), const),
            pl.BlockSpec((D, D), const),
            pl.BlockSpec((2, D), const)],
        out_specs=tok,
        out_shape=jax.ShapeDtypeStruct(x.shape, F32),
        scratch_shapes=[pltpu.VMEM((N_GROUPS, tm, LANES), F32),
                        pltpu.VMEM((D // LANES, tm, LANES), F32)],
        compiler_params=_cparams(2),
        name="attn_mix",
    )(*outs, *lses, x, mod, expand, wo, ln)


def kernel(x, c, ada_w, ada_b, ln_g, ln_b, conv_w_in, conv_w, conv_w_out, kv_ada_w,
           kv_ada_b, w_kv, attn_w_q, attn_w_o, rel_bias, router_w, router_bias,
           moe_w_gate, moe_w_up, moe_w_down):
    batch, seq, _ = x.shape
    assert x.shape[2] == D and batch <= SUBLANES
    assert seq % TOKEN_TILE == 0 and seq % (ATTN_QBLOCKS * BLOCK * DIL_GROUPS[-1][1]) == 0

    c8 = jnp.pad(c, ((0, SUBLANES - batch), (0, 0)))
    mods = _ada_linear(c8, ada_w, ada_b)
    kvmods = _ada_linear(c8, kv_ada_w[None], kv_ada_b[None])[0]
    mod0 = _mod_rows(mods[0], batch, 6)
    mod1 = _mod_rows(mods[1], batch, 6)
    kvmod = _mod_rows(kvmods, batch, 2)
    ln = jnp.stack([ln_g, ln_b], axis=2)

    rw = router_w.T.astype(BF16)
    rbias = router_bias.reshape(N_EXPERTS, 1)
    wgu = jnp.concatenate([moe_w_gate, moe_w_up], axis=-1).astype(BF16)
    wd = moe_w_down.astype(BF16).reshape(DEPTH, N_EXPERTS * EXPERT_FF, D)
    conv_w8 = jnp.pad(conv_w[0], ((0, SUBLANES - CONV_WIDTH), (0, 0)))

    x = _conv_mixer(x, mod0, conv_w_in[0].astype(BF16), conv_w8,
                    conv_w_out[0].astype(BF16), ln[0, 0])
    x = _moe(x, mod0, rw, rbias, wgu[0], wd[0], ln[0, 1])

    qkv = _qkv(x, mod1, kvmod, attn_w_q[0].astype(BF16), w_kv.astype(BF16))
    bias = _bias_tiles(rel_bias)
    outs, lses = zip(*[_attention_group(g, *qkv[3 * g:3 * g + 3], bias)
                       for g in range(N_GROUPS)])
    x = _mix(outs, lses, x, mod1, attn_w_o[0].astype(BF16), ln[1, 0])
    x = _moe(x, mod1, rw, rbias, wgu[1], wd[1], ln[1, 1])
    return x
```

```python
import functools
import math

import jax
import jax.numpy as jnp
from jax import lax
from jax.experimental import pallas as pl
from jax.experimental.pallas import tpu as pltpu

F32 = jnp.float32
BF16 = jnp.bfloat16

D = 1024
DEPTH = 2
CONV_WIDTH = 3
DIL_GROUPS = ((128, 1), (512, 4), (2048, 16))
N_GROUPS = len(DIL_GROUPS)
HEAD_DIM = 128
HEADS = D // HEAD_DIM
QW = N_GROUPS * D
BLOCK = 128
NUM_BUCKETS = 32
MAX_DISTANCE = 2048
N_EXPERTS = 16
EXPERTS_PER_GROUP = 4
N_EXPERT_GROUPS = N_EXPERTS // EXPERTS_PER_GROUP
EXPERT_FF = D // 4
ALPHA = (2 * DEPTH) ** 0.25
LN_EPS = 1e-5
NEG_INF = -1e30
LOG2E = math.log2(math.e)
LN2 = math.log(2.0)

LANES = 128
SUBLANES = 8
VMEM_LIMIT = 56 * 1024 * 1024
TOKEN_TILE = 512
MOE_WIN_BIG = 256
MOE_WIN_MID = 192
MOE_WIN_SMALL = 128
BF16_ROWS = 16
ATTN_QBLOCKS = 16
ADA_COL_TILE = 2048


def _cparams(n_axes):
    return pltpu.CompilerParams(
        dimension_semantics=("arbitrary",) * n_axes, vmem_limit_bytes=VMEM_LIMIT)


def _layer_norm(r, g, b):
    mu = jnp.mean(r, axis=-1, keepdims=True)
    c = r - mu
    var = jnp.mean(c * c, axis=-1, keepdims=True)
    return c * lax.rsqrt(var + LN_EPS) * g + b


def _bdot(a, b):
    return jnp.dot(a, b, preferred_element_type=F32)


def _ada_kernel(c_ref, w_ref, b_ref, o_ref):
    c = c_ref[...]
    cond = c * jax.nn.sigmoid(c)
    o_ref[...] = jnp.dot(cond, w_ref[...], preferred_element_type=F32,
                         precision=lax.Precision.HIGHEST) + b_ref[...]


def _ada_linear(c8, w, bias):
    n_layers, _, n = w.shape
    nt = ADA_COL_TILE
    return pl.pallas_call(
        _ada_kernel,
        grid=(n_layers, n // nt),
        in_specs=[pl.BlockSpec((SUBLANES, D), lambda l, j: (0, 0)),
                  pl.BlockSpec((None, D, nt), lambda l, j: (l, 0, j)),
                  pl.BlockSpec((None, 1, nt), lambda l, j: (l, 0, j))],
        out_specs=pl.BlockSpec((None, SUBLANES, nt), lambda l, j: (l, 0, j)),
        out_shape=jax.ShapeDtypeStruct((n_layers, SUBLANES, n), F32),
        compiler_params=_cparams(2),
        name="ada_linear",
    )(c8, w, bias.reshape(n_layers, 1, n))


def _mod_rows(mods, batch, n_vec):
    m = mods[:batch].reshape(batch, n_vec, D)
    return jnp.pad(m, ((0, 0), (0, SUBLANES - n_vec), (0, 0)))


def _conv_mixer_kernel(x_ref, mod_ref, w_in_ref, cw_ref, w_out_ref, ln_ref,
                       o_ref, u_scr):
    tm = x_ref.shape[0]
    i = pl.program_id(1)
    mod = mod_ref[...]
    sh1, sc1, g1 = mod[0:1], mod[1:2], mod[2:3]
    x = x_ref[...]
    h = (x * (1.0 + sc1) + sh1).astype(BF16)
    cgate = _bdot(h, w_in_ref[:, D:2 * D])
    v = _bdot(h, w_in_ref[:, 2 * D:3 * D])
    u = cgate * v

    @pl.when(i == 0)
    def _():
        u_scr[0:SUBLANES, :] = jnp.zeros((SUBLANES, D), F32)

    u_scr[SUBLANES:SUBLANES + tm, :] = u
    cw = cw_ref[...]
    conv = (cw[2:3] * u
            + cw[1:2] * u_scr[SUBLANES - 1:SUBLANES - 1 + tm, :]
            + cw[0:1] * u_scr[SUBLANES - 2:SUBLANES - 2 + tm, :])
    u_scr[0:SUBLANES, :] = u_scr[tm:tm + SUBLANES, :]
    bgate = _bdot(h, w_in_ref[:, 0:D])
    y = _bdot((bgate * conv).astype(BF16), w_out_ref[...])
    ln = ln_ref[...]
    o_ref[...] = _layer_norm(ALPHA * x + (1.0 + g1) * y, ln[0:1], ln[1:2])


def _conv_mixer(x, mod, w_in, conv_w, w_out, ln):
    batch, seq, _ = x.shape
    tm = TOKEN_TILE
    const = lambda b, i: (0, 0)
    return pl.pallas_call(
        _conv_mixer_kernel,
        grid=(batch, seq // tm),
        in_specs=[pl.BlockSpec((None, tm, D), lambda b, i: (b, i, 0)),
                  pl.BlockSpec((None, SUBLANES, D), lambda b, i: (b, 0, 0)),
                  pl.BlockSpec((D, 3 * D), const),
                  pl.BlockSpec((SUBLANES, D), const),
                  pl.BlockSpec((D, D), const),
                  pl.BlockSpec((2, D), const)],
        out_specs=pl.BlockSpec((None, tm, D), lambda b, i: (b, i, 0)),
        out_shape=jax.ShapeDtypeStruct(x.shape, F32),
        scratch_shapes=[pltpu.VMEM((tm + SUBLANES, D), F32)],
        compiler_params=_cparams(2),
        name="conv_mixer",
    )(x, mod, w_in, conv_w, w_out, ln)


def _route(logits_t, rbias_col):
    aff = jax.nn.sigmoid(logits_t)
    sel = aff + rbias_col
    rows = [sel[e:e + 1, :] for e in range(N_EXPERTS)]
    scores = []
    for g in range(N_EXPERT_GROUPS):
        s0, s1, s2, s3 = rows[EXPERTS_PER_GROUP * g:EXPERTS_PER_GROUP * (g + 1)]
        a, b = jnp.maximum(s0, s1), jnp.minimum(s0, s1)
        c, d = jnp.maximum(s2, s3), jnp.minimum(s2, s3)
        scores.append(jnp.maximum(a, c) + jnp.maximum(jnp.minimum(a, c), jnp.maximum(b, d)))
    best = scores[0]
    best_group = jnp.zeros(best.shape, jnp.int32)
    for g in range(1, N_EXPERT_GROUPS):
        upd = scores[g] > best
        best_group = jnp.where(upd, g, best_group)
        best = jnp.where(upd, scores[g], best)
    e_iota = lax.broadcasted_iota(jnp.int32, sel.shape, 0)
    masked = jnp.where((e_iota // EXPERTS_PER_GROUP) == best_group, sel, NEG_INF)
    m1 = jnp.max(masked, axis=0, keepdims=True)
    idx1 = jnp.min(jnp.where(masked == m1, e_iota, N_EXPERTS), axis=0, keepdims=True)
    masked2 = jnp.where(e_iota == idx1, -jnp.inf, masked)
    m2 = jnp.max(masked2, axis=0, keepdims=True)
    idx2 = jnp.min(jnp.where(masked2 == m2, e_iota, N_EXPERTS), axis=0, keepdims=True)
    a1 = jnp.sum(jnp.where(e_iota == idx1, aff, 0.0), axis=0, keepdims=True)
    a2 = jnp.sum(jnp.where(e_iota == idx2, aff, 0.0), axis=0, keepdims=True)
    den = a1 + a2
    comb = (jnp.where(e_iota == idx1, a1 / den, 0.0)
            + jnp.where(e_iota == idx2, a2 / den, 0.0))
    group1 = idx1 // EXPERTS_PER_GROUP
    split = group1 != idx2 // EXPERTS_PER_GROUP
    return comb, group1, split


def _moe_kernel(x_ref, xp_ref, mod_ref, modp_ref, rw_ref, rb_ref, tri_ref, wgu_ref, wd_ref,
                ln_ref, o_ref, hs_scr, cs_scr, ys_scr, pt_scr):
    tm = x_ref.shape[0]
    gff = EXPERTS_PER_GROUP * EXPERT_FF
    t = pl.program_id(0)

    @pl.when(t == 0)
    def _():
        ys_scr[...] = jnp.zeros(ys_scr.shape, F32)
        pt_scr[...] = jnp.zeros(pt_scr.shape, BF16)

    y = _bdot(pt_scr[...], ys_scr[0:tm, :].astype(BF16))
    ln = ln_ref[...]
    o_ref[...] = _layer_norm(ALPHA * xp_ref[...] + (1.0 + modp_ref[...][5:6]) * y,
                             ln[0:1], ln[1:2])

    mod = mod_ref[...]
    sh2, sc2 = mod[3:4], mod[4:5]
    h = x_ref[...] * (1.0 + sc2) + sh2
    hb = h.astype(BF16)
    logits_t = lax.dot_general(rw_ref[...], hb, (((1,), (1,)), ((), ())),
                               preferred_element_type=F32)
    comb_t, group, split = _route(logits_t, rb_ref[...])

    g_iota = lax.broadcasted_iota(jnp.int32, (SUBLANES, tm), 0)
    member = jnp.where(g_iota == group, 1.0, 0.0)
    rank = _bdot(member.astype(BF16), tri_ref[...])
    counts = jnp.sum(member, axis=1, keepdims=True)
    cnt = [counts[g, 0] for g in range(N_EXPERT_GROUPS)]
    start = [0.0]
    for g in range(N_EXPERT_GROUPS - 1):
        start.append(start[g] + cnt[g])
    dest = sum(member[g:g + 1] * (start[g] + rank[g:g + 1]) for g in range(N_EXPERT_GROUPS))
    row = lax.broadcasted_iota(jnp.int32, (tm, tm), 0).astype(F32)
    col = lax.broadcasted_iota(jnp.int32, (tm, tm), 1).astype(F32)
    perm = jnp.where(row == dest, 1.0, 0.0).astype(BF16)
    dest_col = jnp.broadcast_to(dest, (LANES, tm)).T
    dest_col = jnp.concatenate([dest_col] * (tm // LANES), axis=1)
    perm_t = jnp.where(dest_col == col, 1.0, 0.0).astype(BF16)
    pt_scr[...] = perm_t

    hs_scr[0:tm, :] = _bdot(perm, hb).astype(BF16)
    hs_scr[tm:, :] = jnp.zeros((MOE_WIN_BIG, D), BF16)
    cs_scr[tm:, :] = jnp.zeros((MOE_WIN_BIG, LANES), F32)
    comb_hi = comb_t.astype(BF16)
    comb_lo = (comb_t - comb_hi.astype(F32)).astype(BF16)
    cs = _bdot(jnp.concatenate([comb_hi, comb_lo], axis=0), perm_t)
    cs = cs[0:N_EXPERTS] + cs[N_EXPERTS:2 * N_EXPERTS]
    cs_scr[0:tm, :] = jnp.concatenate(
        [cs, jnp.zeros((LANES - N_EXPERTS, tm), F32)], axis=0).T
    ys_scr[...] = jnp.zeros(ys_scr.shape, F32)

    any_split = jnp.max(jnp.where(split, 1.0, 0.0)) > 0.0
    active = t < pl.num_programs(0) - 1
    for g in range(N_EXPERT_GROUPS):
        first = jnp.where(any_split, 0, jnp.asarray(start[g], F32).astype(jnp.int32))
        last = jnp.where(any_split, tm, (start[g] + cnt[g]).astype(jnp.int32))
        first = (first // BF16_ROWS) * BF16_ROWS
        need = jnp.where(jnp.logical_and(active, last > first), last - first, 0)
        n_full = need // MOE_WIN_BIG
        rest = need - n_full * MOE_WIN_BIG
        n_big = n_full + jnp.where(rest > MOE_WIN_MID, 1, 0)
        n_mid = jnp.where(jnp.logical_and(rest > MOE_WIN_SMALL, rest <= MOE_WIN_MID), 1, 0)
        n_small = jnp.where(jnp.logical_and(rest > 0, rest <= MOE_WIN_SMALL), 1, 0)

        def window(start_row, size, g=g):
            rows = pl.ds(pl.multiple_of(start_row, BF16_ROWS), size)
            hs = hs_scr[rows, :]
            parts = []
            for j in range(EXPERTS_PER_GROUP):
                e = g * EXPERTS_PER_GROUP + j
                gu = _bdot(hs, wgu_ref[e])
                gate, up = gu[:, :EXPERT_FF], gu[:, EXPERT_FF:]
                he = gate * jax.nn.sigmoid(gate) * up * cs_scr[rows, e:e + 1]
                parts.append(he.astype(BF16))
            he = jnp.concatenate(parts, axis=1)
            ys_scr[rows, :] += _bdot(he, wd_ref[g * gff:(g + 1) * gff, :])

        def big(w, carry, first=first, window=window):
            window(first + w * MOE_WIN_BIG, MOE_WIN_BIG)
            return carry

        def mid(w, carry, first=first, n_full=n_full, window=window):
            window(first + n_full * MOE_WIN_BIG, MOE_WIN_MID)
            return carry

        def small(w, carry, first=first, n_full=n_full, window=window):
            window(first + n_full * MOE_WIN_BIG, MOE_WIN_SMALL)
            return carry

        lax.fori_loop(0, n_big, big, 0)
        lax.fori_loop(0, n_mid, mid, 0)
        lax.fori_loop(0, n_small, small, 0)


def _moe(x, mod, rw, rbias, wgu, wd, ln):
    batch, seq, _ = x.shape
    tm = TOKEN_TILE
    per_batch = seq // tm
    n_tiles = batch * per_batch
    const2 = lambda t: (0, 0)
    cur = lambda t: jnp.minimum(t, n_tiles - 1)
    prev = lambda t: jnp.maximum(t - 1, 0)
    tile = lambda which: pl.BlockSpec(
        (None, tm, D), lambda t: (which(t) // per_batch, which(t) % per_batch, 0))
    mods = lambda which: pl.BlockSpec(
        (None, SUBLANES, D), lambda t: (which(t) // per_batch, 0, 0))
    tri = (jnp.arange(tm)[:, None] < jnp.arange(tm)[None, :]).astype(BF16)
    return pl.pallas_call(
        _moe_kernel,
        grid=(n_tiles + 1,),
        in_specs=[tile(cur), tile(prev), mods(cur), mods(prev),
                  pl.BlockSpec((N_EXPERTS, D), const2),
                  pl.BlockSpec((N_EXPERTS, 1), const2),
                  pl.BlockSpec((tm, tm), const2),
                  pl.BlockSpec((N_EXPERTS, D, 2 * EXPERT_FF), lambda t: (0, 0, 0)),
                  pl.BlockSpec((N_EXPERTS * EXPERT_FF, D), const2),
                  pl.BlockSpec((2, D), const2)],
        out_specs=tile(prev),
        out_shape=jax.ShapeDtypeStruct(x.shape, F32),
        scratch_shapes=[pltpu.VMEM((tm + MOE_WIN_BIG, D), BF16),
                        pltpu.VMEM((tm + MOE_WIN_BIG, LANES), F32),
                        pltpu.VMEM((tm + MOE_WIN_BIG, D), F32),
                        pltpu.VMEM((tm, tm), BF16)],
        compiler_params=_cparams(1),
        name="moe",
    )(x, x, mod, mod, rw, rbias, tri, wgu, wd, ln)


def _qkv_kernel(x_ref, mod_ref, kvmod_ref, wq_ref, wkv_ref, *refs):
    outs, (slab, hq_scr, hkv_scr) = refs[:3 * N_GROUPS], refs[3 * N_GROUPS:]
    tm = x_ref.shape[0]
    mod = mod_ref[...]
    kvmod = kvmod_ref[...]
    q_sc, q_sh = 1.0 + mod[1:2], mod[0:1]
    kv_sc, kv_sh = 1.0 + kvmod[1:2], kvmod[0:1]
    n_slabs = D // LANES
    for j in range(n_slabs):
        slab[j] = x_ref[:, j * LANES:(j + 1) * LANES]
    scale = HEAD_DIM ** -0.5 * LOG2E
    for g, (_, dil) in enumerate(DIL_GROUPS):
        n = tm // dil
        for r in range(dil):
            rows = slice(r * n, (r + 1) * n)
            for j in range(n_slabs):
                ls = slice(j * LANES, (j + 1) * LANES)
                piece = slab[j] if dil == 1 else slab[j, pl.ds(r, n, stride=dil), :]
                hq_scr[rows, ls] = (piece * q_sc[:, ls] + q_sh[:, ls]).astype(BF16)
                hkv_scr[rows, ls] = (piece * kv_sc[:, ls] + kv_sh[:, ls]).astype(BF16)
        hq = hq_scr[...]
        hkv = hkv_scr[...]
        cols = slice(g * D, (g + 1) * D)
        vcols = slice(QW + g * D, QW + (g + 1) * D)
        q = (_bdot(hq, wq_ref[:, cols]) * scale).astype(BF16)
        k = _bdot(hkv, wkv_ref[:, cols]).astype(BF16)
        v = _bdot(hkv, wkv_ref[:, vcols]).astype(BF16)
        for r in range(dil):
            rows = slice(r * n, (r + 1) * n)
            outs[3 * g][r] = q[rows]
            outs[3 * g + 1][r] = k[rows]
            outs[3 * g + 2][r] = v[rows]


def _qkv(x, mod, kvmod, wq, wkv):
    batch, seq, _ = x.shape
    tm = TOKEN_TILE
    const = lambda b, i: (0, 0)
    out_specs, out_shape = [], []
    for _, dil in DIL_GROUPS:
        for _ in range(3):
            out_specs.append(pl.BlockSpec((None, dil, tm // dil, D), lambda b, i: (b, 0, i, 0)))
            out_shape.append(jax.ShapeDtypeStruct((batch, dil, seq // dil, D), BF16))
    return pl.pallas_call(
        _qkv_kernel,
        grid=(batch, seq // tm),
        in_specs=[pl.BlockSpec((None, tm, D), lambda b, i: (b, i, 0)),
                  pl.BlockSpec((None, SUBLANES, D), lambda b, i: (b, 0, 0)),
                  pl.BlockSpec((None, SUBLANES, D), lambda b, i: (b, 0, 0)),
                  pl.BlockSpec((D, QW), const),
                  pl.BlockSpec((D, 2 * QW), const)],
        out_specs=out_specs,
        out_shape=out_shape,
        scratch_shapes=[pltpu.VMEM((D // LANES, tm, LANES), F32),
                        pltpu.VMEM((tm, D), BF16),
                        pltpu.VMEM((tm, D), BF16)],
        compiler_params=_cparams(2),
        name="qkv_proj",
    )(x, mod, kvmod, wq, wkv)


def _bucket_tables():
    qi = jnp.arange(BLOCK)[:, None]
    kj = jnp.arange(2 * BLOCK)[None, :]
    dist = BLOCK + qi - kj
    max_exact = NUM_BUCKETS // 2
    tables = []
    for window, dil in DIL_GROUPS:
        n = jnp.maximum(dist, 0) * dil
        nf = jnp.maximum(n, 1).astype(F32)
        large = max_exact + (jnp.log(nf / max_exact) / math.log(MAX_DISTANCE / max_exact)
                             * (NUM_BUCKETS - max_exact)).astype(jnp.int32)
        large = jnp.minimum(large, NUM_BUCKETS - 1)
        bucket = jnp.where(n < max_exact, n, large)
        valid = (dist >= 0) & (dist <= window // dil)
        tables.append(jnp.where(valid, bucket, -1).astype(jnp.int32))
    return jnp.stack(tables)


def _bias_kernel(rb_ref, bkt_ref, o_ref):
    g = pl.program_id(0)
    bkt = bkt_ref[...]
    for h in range(HEADS):
        acc = jnp.full(bkt.shape, NEG_INF, F32)
        for b in range(NUM_BUCKETS):
            acc = jnp.where(bkt == b, rb_ref[b, g * HEADS + h] * LOG2E, acc)
        o_ref[h] = acc


def _bias_tiles(rel_bias):
    return pl.pallas_call(
        _bias_kernel,
        grid=(N_GROUPS,),
        in_specs=[pl.BlockSpec(memory_space=pltpu.SMEM),
                  pl.BlockSpec((None, BLOCK, 2 * BLOCK), lambda g: (g, 0, 0))],
        out_specs=pl.BlockSpec((None, HEADS, BLOCK, 2 * BLOCK), lambda g: (g, 0, 0, 0)),
        out_shape=jax.ShapeDtypeStruct((N_GROUPS, HEADS, BLOCK, 2 * BLOCK), F32),
        compiler_params=_cparams(1),
        name="rel_bias_tiles",
    )(rel_bias, _bucket_tables())


def _attn_kernel(q_ref, kp_ref, kc_ref, vp_ref, vc_ref, bias_ref, o_ref, lse_ref):
    n = pl.program_id(2)
    col = lax.broadcasted_iota(jnp.int32, (BLOCK, 2 * BLOCK), 1)
    no_prev = jnp.logical_and(n == 0, col < BLOCK)
    lane = lax.broadcasted_iota(jnp.int32, (BLOCK, LANES), 1)
    ones = jnp.ones((2 * BLOCK, HEAD_DIM), BF16)
    for r in range(q_ref.shape[0]):
        for j in range(q_ref.shape[1] // BLOCK):
            rows = slice(j * BLOCK, (j + 1) * BLOCK)
            window = slice((j - 1) * BLOCK, (j + 1) * BLOCK)
            m_tile = jnp.zeros((BLOCK, LANES), F32)
            l_tile = jnp.ones((BLOCK, LANES), F32)
            for h in range(HEADS):
                hs = slice(h * HEAD_DIM, (h + 1) * HEAD_DIM)
                if j == 0:
                    k = jnp.concatenate([kp_ref[r, :, hs], kc_ref[r, rows, hs]], axis=0)
                    v = jnp.concatenate([vp_ref[r, :, hs], vc_ref[r, rows, hs]], axis=0)
                else:
                    k = kc_ref[r, window, hs]
                    v = vc_ref[r, window, hs]
                s = lax.dot_general(q_ref[r, rows, hs], k, (((1,), (1,)), ((), ())),
                                    preferred_element_type=F32) + bias_ref[h]
                if j == 0:
                    s = jnp.where(no_prev, NEG_INF, s)
                m = jnp.max(s, axis=-1, keepdims=True)
                p = jnp.exp2(s - m)
                o_ext = _bdot(p.astype(BF16), jnp.concatenate([v, ones], axis=1))
                o, l = o_ext[:, :HEAD_DIM], o_ext[:, HEAD_DIM:]
                o_ref[r, rows, hs] = (o / l).astype(BF16)
                m_tile = jnp.where(lane == h, m, m_tile)
                l_tile = jnp.where(lane == h, l, l_tile)
            lse_ref[r, rows, :] = m_tile * LN2 + jnp.log(l_tile)


def _attention_group(g, q, k, v, bias):
    batch, dil, sub_len, _ = q.shape
    nq = min(ATTN_QBLOCKS, sub_len // BLOCK)
    n_sub = min(dil, ATTN_QBLOCKS // nq)
    assert sub_len % (nq * BLOCK) == 0 and dil % n_sub == 0
    cur = lambda b, r, n: (b, r, n, 0)
    prev = lambda b, r, n: (b, r, jnp.maximum(n * nq - 1, 0), 0)
    blk = (None, n_sub, nq * BLOCK, D)
    blk_prev = (None, n_sub, BLOCK, D)
    return pl.pallas_call(
        _attn_kernel,
        grid=(batch, dil // n_sub, sub_len // (nq * BLOCK)),
        in_specs=[
            pl.BlockSpec(blk, cur),
            pl.BlockSpec(blk_prev, prev),
            pl.BlockSpec(blk, cur),
            pl.BlockSpec(blk_prev, prev),
            pl.BlockSpec(blk, cur),
            pl.BlockSpec((None, HEADS, BLOCK, 2 * BLOCK), lambda b, r, n: (g, 0, 0, 0)),
        ],
        out_specs=[pl.BlockSpec(blk, cur),
                   pl.BlockSpec((None, n_sub, nq * BLOCK, LANES), cur)],
        out_shape=[jax.ShapeDtypeStruct((batch, dil, sub_len, D), BF16),
                   jax.ShapeDtypeStruct((batch, dil, sub_len, LANES), F32)],
        compiler_params=_cparams(3),
        name=f"dilated_attn_g{g}",
    )(q, k, k, v, v, bias)


def _mix_kernel(o0_ref, o1_ref, o2_ref, l0_ref, l1_ref, l2_ref, x_ref, mod_ref,
                expand_ref, wo_ref, ln_ref, out_ref, lse_scr, o_slab):
    tm = x_ref.shape[0]
    n_slabs = D // LANES
    lses = []
    for g, l_ref in enumerate((l0_ref, l1_ref, l2_ref)):
        dil = DIL_GROUPS[g][1]
        if dil == 1:
            lses.append(l_ref[0])
            continue
        for r in range(dil):
            lse_scr[g, pl.ds(r, tm // dil, stride=dil), :] = l_ref[r]
        lses.append(lse_scr[g])
    m = jnp.maximum(lses[0], jnp.maximum(lses[1], lses[2]))
    es = [jnp.exp(a - m) for a in lses]
    den = es[0] + es[1] + es[2]
    expand = expand_ref[...]
    mixed = None
    for g, o_ref in enumerate((o0_ref, o1_ref, o2_ref)):
        dil = DIL_GROUPS[g][1]
        w = _bdot((es[g] / den).astype(BF16), expand)
        if dil == 1:
            o = o_ref[0].astype(F32)
        else:
            for r in range(dil):
                for j in range(n_slabs):
                    o_slab[j, pl.ds(r, tm // dil, stride=dil), :] = (
                        o_ref[r, :, j * LANES:(j + 1) * LANES].astype(F32))
            o = jnp.concatenate([o_slab[j] for j in range(n_slabs)], axis=1)
        mixed = w * o if mixed is None else mixed + w * o
    y = _bdot(mixed.astype(BF16), wo_ref[...])
    g1 = mod_ref[...][2:3]
    ln = ln_ref[...]
    out_ref[...] = _layer_norm(ALPHA * x_ref[...] + (1.0 + g1) * y, ln[0:1], ln[1:2])


def _mix(outs, lses, x, mod, wo, ln):
    batch, seq, _ = x.shape
    tm = TOKEN_TILE
    const = lambda b, i: (0, 0)
    tok = pl.BlockSpec((None, tm, D), lambda b, i: (b, i, 0))
    dilated = lambda dil, w: pl.BlockSpec((None, dil, tm // dil, w), lambda b, i: (b, 0, i, 0))
    dils = [dil for _, dil in DIL_GROUPS]
    head_of_lane = jnp.arange(D)[None, :] // HEAD_DIM
    expand = (jnp.arange(LANES)[:, None] == head_of_lane).astype(BF16)
    return pl.pallas_call(
        _mix_kernel,
        grid=(batch, seq // tm),
        in_specs=[dilated(d, D) for d in dils] + [dilated(d, LANES) for d in dils] + [
            tok,
            pl.BlockSpec((None, SUBLANES, D), lambda b, i: (b, 0, 0)),
            pl.BlockSpec((LANES, D), const),
            pl.BlockSpec((D, D), const),
            pl.BlockSpec((2, D), const)],
        out_specs=tok,
        out_shape=jax.ShapeDtypeStruct(x.shape, F32),
        scratch_shapes=[pltpu.VMEM((N_GROUPS, tm, LANES), F32),
                        pltpu.VMEM((D // LANES, tm, LANES), F32)],
        compiler_params=_cparams(2),
        name="attn_mix",
    )(*outs, *lses, x, mod, expand, wo, ln)


def kernel(x, c, ada_w, ada_b, ln_g, ln_b, conv_w_in, conv_w, conv_w_out, kv_ada_w,
           kv_ada_b, w_kv, attn_w_q, attn_w_o, rel_bias, router_w, router_bias,
           moe_w_gate, moe_w_up, moe_w_down):
    batch, seq, _ = x.shape
    assert x.shape[2] == D and batch <= SUBLANES
    assert seq % TOKEN_TILE == 0 and seq % (BLOCK * DIL_GROUPS[-1][1]) == 0

    c8 = jnp.pad(c, ((0, SUBLANES - batch), (0, 0)))
    mods = _ada_linear(c8, ada_w, ada_b)
    kvmods = _ada_linear(c8, kv_ada_w[None], kv_ada_b[None])[0]
    mod0 = _mod_rows(mods[0], batch, 6)
    mod1 = _mod_rows(mods[1], batch, 6)
    kvmod = _mod_rows(kvmods, batch, 2)
    ln = jnp.stack([ln_g, ln_b], axis=2)

    rw = router_w.T.astype(BF16)
    rbias = router_bias.reshape(N_EXPERTS, 1)
    wgu = jnp.concatenate([moe_w_gate, moe_w_up], axis=-1).astype(BF16)
    wd = moe_w_down.astype(BF16).reshape(DEPTH, N_EXPERTS * EXPERT_FF, D)
    conv_w8 = jnp.pad(conv_w[0], ((0, SUBLANES - CONV_WIDTH), (0, 0)))

    x = _conv_mixer(x, mod0, conv_w_in[0].astype(BF16), conv_w8,
                    conv_w_out[0].astype(BF16), ln[0, 0])
    x = _moe(x, mod0, rw, rbias, wgu[0], wd[0], ln[0, 1])

    qkv = _qkv(x, mod1, kvmod, attn_w_q[0].astype(BF16), w_kv.astype(BF16))
    bias = _bias_tiles(rel_bias)
    outs, lses = zip(*[_attention_group(g, *qkv[3 * g:3 * g + 3], bias)
                       for g in range(N_GROUPS)])
    x = _mix(outs, lses, x, mod1, attn_w_o[0].astype(BF16), ln[1, 0])
    x = _moe(x, mod1, rw, rbias, wgu[1], wd[1], ln[1, 1])
    return x
```

```python
import functools
import math

import jax
import jax.numpy as jnp
from jax import lax
from jax.experimental import pallas as pl
from jax.experimental.pallas import tpu as pltpu

F32 = jnp.float32
BF16 = jnp.bfloat16

D = 1024
DEPTH = 2
CONV_WIDTH = 3
DIL_GROUPS = ((128, 1), (512, 4), (2048, 16))
N_GROUPS = len(DIL_GROUPS)
HEAD_DIM = 128
HEADS = D // HEAD_DIM
QW = N_GROUPS * D
BLOCK = 128
NUM_BUCKETS = 32
MAX_DISTANCE = 2048
N_EXPERTS = 16
EXPERTS_PER_GROUP = 4
N_EXPERT_GROUPS = N_EXPERTS // EXPERTS_PER_GROUP
EXPERT_FF = D // 4
ALPHA = (2 * DEPTH) ** 0.25
LN_EPS = 1e-5
NEG_INF = -1e30
LOG2E = math.log2(math.e)
LN2 = math.log(2.0)

LANES = 128
SUBLANES = 8
VMEM_LIMIT = 56 * 1024 * 1024
TOKEN_TILE = 512
MOE_WIN_BIG = 256
MOE_WIN_MID = 192
MOE_WIN_SMALL = 128
BF16_ROWS = 16
ATTN_QBLOCKS = 16
ADA_COL_TILE = 2048


def _cparams(n_axes):
    return pltpu.CompilerParams(
        dimension_semantics=("arbitrary",) * n_axes, vmem_limit_bytes=VMEM_LIMIT)


def _layer_norm(r, g, b):
    mu = jnp.mean(r, axis=-1, keepdims=True)
    c = r - mu
    var = jnp.mean(c * c, axis=-1, keepdims=True)
    return c * lax.rsqrt(var + LN_EPS) * g + b


def _bdot(a, b):
    return jnp.dot(a, b, preferred_element_type=F32)


def _ada_kernel(c_ref, w_ref, b_ref, o_ref):
    c = c_ref[...]
    cond = c * jax.nn.sigmoid(c)
    o_ref[...] = jnp.dot(cond, w_ref[...], preferred_element_type=F32,
                         precision=lax.Precision.HIGHEST) + b_ref[...]


def _ada_linear(c8, w, bias):
    n_layers, _, n = w.shape
    nt = ADA_COL_TILE
    return pl.pallas_call(
        _ada_kernel,
        grid=(n_layers, n // nt),
        in_specs=[pl.BlockSpec((SUBLANES, D), lambda l, j: (0, 0)),
                  pl.BlockSpec((None, D, nt), lambda l, j: (l, 0, j)),
                  pl.BlockSpec((None, 1, nt), lambda l, j: (l, 0, j))],
        out_specs=pl.BlockSpec((None, SUBLANES, nt), lambda l, j: (l, 0, j)),
        out_shape=jax.ShapeDtypeStruct((n_layers, SUBLANES, n), F32),
        compiler_params=_cparams(2),
        name="ada_linear",
    )(c8, w, bias.reshape(n_layers, 1, n))


def _mod_rows(mods, batch, n_vec):
    m = mods[:batch].reshape(batch, n_vec, D)
    return jnp.pad(m, ((0, 0), (0, SUBLANES - n_vec), (0, 0)))


def _conv_mixer_kernel(x_ref, mod_ref, w_in_ref, cw_ref, w_out_ref, ln_ref,
                       o_ref, u_scr):
    tm = x_ref.shape[0]
    i = pl.program_id(1)
    mod = mod_ref[...]
    sh1, sc1, g1 = mod[0:1], mod[1:2], mod[2:3]
    x = x_ref[...]
    h = (x * (1.0 + sc1) + sh1).astype(BF16)
    cgate = _bdot(h, w_in_ref[:, D:2 * D])
    v = _bdot(h, w_in_ref[:, 2 * D:3 * D])
    u = cgate * v

    @pl.when(i == 0)
    def _():
        u_scr[0:SUBLANES, :] = jnp.zeros((SUBLANES, D), F32)

    u_scr[SUBLANES:SUBLANES + tm, :] = u
    cw = cw_ref[...]
    conv = (cw[2:3] * u
            + cw[1:2] * u_scr[SUBLANES - 1:SUBLANES - 1 + tm, :]
            + cw[0:1] * u_scr[SUBLANES - 2:SUBLANES - 2 + tm, :])
    u_scr[0:SUBLANES, :] = u_scr[tm:tm + SUBLANES, :]
    bgate = _bdot(h, w_in_ref[:, 0:D])
    y = _bdot((bgate * conv).astype(BF16), w_out_ref[...])
    ln = ln_ref[...]
    o_ref[...] = _layer_norm(ALPHA * x + (1.0 + g1) * y, ln[0:1], ln[1:2])


def _conv_mixer(x, mod, w_in, conv_w, w_out, ln):
    batch, seq, _ = x.shape
    tm = TOKEN_TILE
    const = lambda b, i: (0, 0)
    return pl.pallas_call(
        _conv_mixer_kernel,
        grid=(batch, seq // tm),
        in_specs=[pl.BlockSpec((None, tm, D), lambda b, i: (b, i, 0)),
                  pl.BlockSpec((None, SUBLANES, D), lambda b, i: (b, 0, 0)),
                  pl.BlockSpec((D, 3 * D), const),
                  pl.BlockSpec((SUBLANES, D), const),
                  pl.BlockSpec((D, D), const),
                  pl.BlockSpec((2, D), const)],
        out_specs=pl.BlockSpec((None, tm, D), lambda b, i: (b, i, 0)),
        out_shape=jax.ShapeDtypeStruct(x.shape, F32),
        scratch_shapes=[pltpu.VMEM((tm + SUBLANES, D), F32)],
        compiler_params=_cparams(2),
        name="conv_mixer",
    )(x, mod, w_in, conv_w, w_out, ln)


def _route(logits_t, rbias_col):
    aff = jax.nn.sigmoid(logits_t)
    sel = aff + rbias_col
    rows = [sel[e:e + 1, :] for e in range(N_EXPERTS)]
    scores = []
    for g in range(N_EXPERT_GROUPS):
        s0, s1, s2, s3 = rows[EXPERTS_PER_GROUP * g:EXPERTS_PER_GROUP * (g + 1)]
        a, b = jnp.maximum(s0, s1), jnp.minimum(s0, s1)
        c, d = jnp.maximum(s2, s3), jnp.minimum(s2, s3)
        scores.append(jnp.maximum(a, c) + jnp.maximum(jnp.minimum(a, c), jnp.maximum(b, d)))
    best = scores[0]
    best_group = jnp.zeros(best.shape, jnp.int32)
    for g in range(1, N_EXPERT_GROUPS):
        upd = scores[g] > best
        best_group = jnp.where(upd, g, best_group)
        best = jnp.where(upd, scores[g], best)
    e_iota = lax.broadcasted_iota(jnp.int32, sel.shape, 0)
    masked = jnp.where((e_iota // EXPERTS_PER_GROUP) == best_group, sel, NEG_INF)
    m1 = jnp.max(masked, axis=0, keepdims=True)
    idx1 = jnp.min(jnp.where(masked == m1, e_iota, N_EXPERTS), axis=0, keepdims=True)
    masked2 = jnp.where(e_iota == idx1, -jnp.inf, masked)
    m2 = jnp.max(masked2, axis=0, keepdims=True)
    idx2 = jnp.min(jnp.where(masked2 == m2, e_iota, N_EXPERTS), axis=0, keepdims=True)
    a1 = jnp.sum(jnp.where(e_iota == idx1, aff, 0.0), axis=0, keepdims=True)
    a2 = jnp.sum(jnp.where(e_iota == idx2, aff, 0.0), axis=0, keepdims=True)
    den = a1 + a2
    comb = (jnp.where(e_iota == idx1, a1 / den, 0.0)
            + jnp.where(e_iota == idx2, a2 / den, 0.0))
    group1 = idx1 // EXPERTS_PER_GROUP
    split = group1 != idx2 // EXPERTS_PER_GROUP
    return comb, group1, split


def _moe_kernel(x_ref, xp_ref, mod_ref, modp_ref, rw_ref, rb_ref, tri_ref, wgu_ref, wd_ref,
                ln_ref, o_ref, hs_scr, cs_scr, ys_scr, pt_scr):
    tm = x_ref.shape[0]
    gff = EXPERTS_PER_GROUP * EXPERT_FF
    t = pl.program_id(0)

    @pl.when(t == 0)
    def _():
        ys_scr[...] = jnp.zeros(ys_scr.shape, F32)
        pt_scr[...] = jnp.zeros(pt_scr.shape, BF16)

    y = _bdot(pt_scr[...], ys_scr[0:tm, :].astype(BF16))
    ln = ln_ref[...]
    o_ref[...] = _layer_norm(ALPHA * xp_ref[...] + (1.0 + modp_ref[...][5:6]) * y,
                             ln[0:1], ln[1:2])

    mod = mod_ref[...]
    sh2, sc2 = mod[3:4], mod[4:5]
    h = x_ref[...] * (1.0 + sc2) + sh2
    hb = h.astype(BF16)
    logits_t = lax.dot_general(rw_ref[...], hb, (((1,), (1,)), ((), ())),
                               preferred_element_type=F32)
    comb_t, group, split = _route(logits_t, rb_ref[...])

    g_iota = lax.broadcasted_iota(jnp.int32, (SUBLANES, tm), 0)
    member = jnp.where(g_iota == group, 1.0, 0.0)
    rank = _bdot(member.astype(BF16), tri_ref[...])
    counts = jnp.sum(member, axis=1, keepdims=True)
    cnt = [counts[g, 0] for g in range(N_EXPERT_GROUPS)]
    start = [0.0]
    for g in range(N_EXPERT_GROUPS - 1):
        start.append(start[g] + cnt[g])
    dest = sum(member[g:g + 1] * (start[g] + rank[g:g + 1]) for g in range(N_EXPERT_GROUPS))
    row = lax.broadcasted_iota(jnp.int32, (tm, tm), 0).astype(F32)
    col = lax.broadcasted_iota(jnp.int32, (tm, tm), 1).astype(F32)
    perm = jnp.where(row == dest, 1.0, 0.0).astype(BF16)
    dest_col = jnp.broadcast_to(dest, (LANES, tm)).T
    dest_col = jnp.concatenate([dest_col] * (tm // LANES), axis=1)
    perm_t = jnp.where(dest_col == col, 1.0, 0.0).astype(BF16)
    pt_scr[...] = perm_t

    hs_scr[0:tm, :] = _bdot(perm, hb).astype(BF16)
    hs_scr[tm:, :] = jnp.zeros((MOE_WIN_BIG, D), BF16)
    cs_scr[tm:, :] = jnp.zeros((MOE_WIN_BIG, LANES), F32)
    comb_hi = comb_t.astype(BF16)
    comb_lo = (comb_t - comb_hi.astype(F32)).astype(BF16)
    cs = _bdot(jnp.concatenate([comb_hi, comb_lo], axis=0), perm_t)
    cs = cs[0:N_EXPERTS] + cs[N_EXPERTS:2 * N_EXPERTS]
    cs_scr[0:tm, :] = jnp.concatenate(
        [cs, jnp.zeros((LANES - N_EXPERTS, tm), F32)], axis=0).T
    ys_scr[...] = jnp.zeros(ys_scr.shape, F32)

    any_split = jnp.max(jnp.where(split, 1.0, 0.0)) > 0.0
    active = t < pl.num_programs(0) - 1
    for g in range(N_EXPERT_GROUPS):
        first = jnp.where(any_split, 0, jnp.asarray(start[g], F32).astype(jnp.int32))
        last = jnp.where(any_split, tm, (start[g] + cnt[g]).astype(jnp.int32))
        first = (first // BF16_ROWS) * BF16_ROWS
        need = jnp.where(jnp.logical_and(active, last > first), last - first, 0)
        n_full = need // MOE_WIN_BIG
        rest = need - n_full * MOE_WIN_BIG
        n_big = n_full + jnp.where(rest > MOE_WIN_MID, 1, 0)
        n_mid = jnp.where(jnp.logical_and(rest > MOE_WIN_SMALL, rest <= MOE_WIN_MID), 1, 0)
        n_small = jnp.where(jnp.logical_and(rest > 0, rest <= MOE_WIN_SMALL), 1, 0)

        def window(start_row, size, g=g):
            rows = pl.ds(pl.multiple_of(start_row, BF16_ROWS), size)
            hs = hs_scr[rows, :]
            parts = []
            for j in range(EXPERTS_PER_GROUP):
                e = g * EXPERTS_PER_GROUP + j
                gu = _bdot(hs, wgu_ref[e])
                gate, up = gu[:, :EXPERT_FF], gu[:, EXPERT_FF:]
                he = gate * jax.nn.sigmoid(gate) * up * cs_scr[rows, e:e + 1]
                parts.append(he.astype(BF16))
            he = jnp.concatenate(parts, axis=1)
            ys_scr[rows, :] += _bdot(he, wd_ref[g * gff:(g + 1) * gff, :])

        def big(w, carry, first=first, window=window):
            window(first + w * MOE_WIN_BIG, MOE_WIN_BIG)
            return carry

        def mid(w, carry, first=first, n_full=n_full, window=window):
            window(first + n_full * MOE_WIN_BIG, MOE_WIN_MID)
            return carry

        def small(w, carry, first=first, n_full=n_full, window=window):
            window(first + n_full * MOE_WIN_BIG, MOE_WIN_SMALL)
            return carry

        lax.fori_loop(0, n_big, big, 0)
        lax.fori_loop(0, n_mid, mid, 0)
        lax.fori_loop(0, n_small, small, 0)


def _moe(x, mod, rw, rbias, wgu, wd, ln):
    batch, seq, _ = x.shape
    tm = TOKEN_TILE
    per_batch = seq // tm
    n_tiles = batch * per_batch
    const2 = lambda t: (0, 0)
    cur = lambda t: jnp.minimum(t, n_tiles - 1)
    prev = lambda t: jnp.maximum(t - 1, 0)
    tile = lambda which: pl.BlockSpec(
        (None, tm, D), lambda t: (which(t) // per_batch, which(t) % per_batch, 0))
    mods = lambda which: pl.BlockSpec(
        (None, SUBLANES, D), lambda t: (which(t) // per_batch, 0, 0))
    tri = (jnp.arange(tm)[:, None] < jnp.arange(tm)[None, :]).astype(BF16)
    return pl.pallas_call(
        _moe_kernel,
        grid=(n_tiles + 1,),
        in_specs=[tile(cur), tile(prev), mods(cur), mods(prev),
                  pl.BlockSpec((N_EXPERTS, D), const2),
                  pl.BlockSpec((N_EXPERTS, 1), const2),
                  pl.BlockSpec((tm, tm), const2),
                  pl.BlockSpec((N_EXPERTS, D, 2 * EXPERT_FF), lambda t: (0, 0, 0)),
                  pl.BlockSpec((N_EXPERTS * EXPERT_FF, D), const2),
                  pl.BlockSpec((2, D), const2)],
        out_specs=tile(prev),
        out_shape=jax.ShapeDtypeStruct(x.shape, F32),
        scratch_shapes=[pltpu.VMEM((tm + MOE_WIN_BIG, D), BF16),
                        pltpu.VMEM((tm + MOE_WIN_BIG, LANES), F32),
                        pltpu.VMEM((tm + MOE_WIN_BIG, D), F32),
                        pltpu.VMEM((tm, tm), BF16)],
        compiler_params=_cparams(1),
        name="moe",
    )(x, x, mod, mod, rw, rbias, tri, wgu, wd, ln)


def _qkv_kernel(x_ref, mod_ref, kvmod_ref, wq_ref, wkv_ref, *refs):
    outs, (slab, hq_scr, hkv_scr) = refs[:N_GROUPS], refs[N_GROUPS:]
    tm = x_ref.shape[0]
    mod = mod_ref[...]
    kvmod = kvmod_ref[...]
    q_sc, q_sh = 1.0 + mod[1:2], mod[0:1]
    kv_sc, kv_sh = 1.0 + kvmod[1:2], kvmod[0:1]
    n_slabs = D // LANES
    for j in range(n_slabs):
        slab[j] = x_ref[:, j * LANES:(j + 1) * LANES]
    scale = HEAD_DIM ** -0.5 * LOG2E
    for g, (_, dil) in enumerate(DIL_GROUPS):
        n = tm // dil
        for r in range(dil):
            rows = slice(r * n, (r + 1) * n)
            for j in range(n_slabs):
                ls = slice(j * LANES, (j + 1) * LANES)
                piece = slab[j] if dil == 1 else slab[j, pl.ds(r, n, stride=dil), :]
                hq_scr[rows, ls] = (piece * q_sc[:, ls] + q_sh[:, ls]).astype(BF16)
                hkv_scr[rows, ls] = (piece * kv_sc[:, ls] + kv_sh[:, ls]).astype(BF16)
        hq = hq_scr[...]
        hkv = hkv_scr[...]
        cols = slice(g * D, (g + 1) * D)
        vcols = slice(QW + g * D, QW + (g + 1) * D)
        q = (_bdot(hq, wq_ref[:, cols]) * scale).astype(BF16)
        k = _bdot(hkv, wkv_ref[:, cols]).astype(BF16)
        v = _bdot(hkv, wkv_ref[:, vcols]).astype(BF16)
        for r in range(dil):
            rows = slice(r * n, (r + 1) * n)
            outs[g][r, :, 0:D] = q[rows]
            outs[g][r, :, D:2 * D] = k[rows]
            outs[g][r, :, 2 * D:3 * D] = v[rows]


def _qkv(x, mod, kvmod, wq, wkv):
    batch, seq, _ = x.shape
    tm = TOKEN_TILE
    const = lambda b, i: (0, 0)
    out_specs, out_shape = [], []
    for _, dil in DIL_GROUPS:
        out_specs.append(pl.BlockSpec((None, dil, tm // dil, 3 * D), lambda b, i: (b, 0, i, 0)))
        out_shape.append(jax.ShapeDtypeStruct((batch, dil, seq // dil, 3 * D), BF16))
    return pl.pallas_call(
        _qkv_kernel,
        grid=(batch, seq // tm),
        in_specs=[pl.BlockSpec((None, tm, D), lambda b, i: (b, i, 0)),
                  pl.BlockSpec((None, SUBLANES, D), lambda b, i: (b, 0, 0)),
                  pl.BlockSpec((None, SUBLANES, D), lambda b, i: (b, 0, 0)),
                  pl.BlockSpec((D, QW), const),
                  pl.BlockSpec((D, 2 * QW), const)],
        out_specs=out_specs,
        out_shape=out_shape,
        scratch_shapes=[pltpu.VMEM((D // LANES, tm, LANES), F32),
                        pltpu.VMEM((tm, D), BF16),
                        pltpu.VMEM((tm, D), BF16)],
        compiler_params=_cparams(2),
        name="qkv_proj",
    )(x, mod, kvmod, wq, wkv)


def _bucket_tables():
    qi = jnp.arange(BLOCK)[:, None]
    kj = jnp.arange(2 * BLOCK)[None, :]
    dist = BLOCK + qi - kj
    max_exact = NUM_BUCKETS // 2
    tables = []
    for window, dil in DIL_GROUPS:
        n = jnp.maximum(dist, 0) * dil
        nf = jnp.maximum(n, 1).astype(F32)
        large = max_exact + (jnp.log(nf / max_exact) / math.log(MAX_DISTANCE / max_exact)
                             * (NUM_BUCKETS - max_exact)).astype(jnp.int32)
        large = jnp.minimum(large, NUM_BUCKETS - 1)
        bucket = jnp.where(n < max_exact, n, large)
        valid = (dist >= 0) & (dist <= window // dil)
        tables.append(jnp.where(valid, bucket, -1).astype(jnp.int32))
    return jnp.stack(tables)


def _bias_kernel(rb_ref, bkt_ref, o_ref):
    g = pl.program_id(0)
    bkt = bkt_ref[...]
    for h in range(HEADS):
        acc = jnp.full(bkt.shape, NEG_INF, F32)
        for b in range(NUM_BUCKETS):
            acc = jnp.where(bkt == b, rb_ref[b, g * HEADS + h] * LOG2E, acc)
        o_ref[h] = acc


def _bias_tiles(rel_bias):
    return pl.pallas_call(
        _bias_kernel,
        grid=(N_GROUPS,),
        in_specs=[pl.BlockSpec(memory_space=pltpu.SMEM),
                  pl.BlockSpec((None, BLOCK, 2 * BLOCK), lambda g: (g, 0, 0))],
        out_specs=pl.BlockSpec((None, HEADS, BLOCK, 2 * BLOCK), lambda g: (g, 0, 0, 0)),
        out_shape=jax.ShapeDtypeStruct((N_GROUPS, HEADS, BLOCK, 2 * BLOCK), F32),
        compiler_params=_cparams(1),
        name="rel_bias_tiles",
    )(rel_bias, _bucket_tables())


def _attn_kernel(q_ref, kp_ref, kc_ref, vp_ref, vc_ref, bias_ref, o_ref, lse_ref):
    n = pl.program_id(2)
    col = lax.broadcasted_iota(jnp.int32, (BLOCK, 2 * BLOCK), 1)
    no_prev = jnp.logical_and(n == 0, col < BLOCK)
    lane = lax.broadcasted_iota(jnp.int32, (BLOCK, LANES), 1)
    ones = jnp.ones((2 * BLOCK, HEAD_DIM), BF16)
    for r in range(q_ref.shape[0]):
        for j in range(q_ref.shape[1] // BLOCK):
            rows = slice(j * BLOCK, (j + 1) * BLOCK)
            window = slice((j - 1) * BLOCK, (j + 1) * BLOCK)
            m_tile = jnp.zeros((BLOCK, LANES), F32)
            l_tile = jnp.ones((BLOCK, LANES), F32)
            for h in range(HEADS):
                hs = slice(h * HEAD_DIM, (h + 1) * HEAD_DIM)
                if j == 0:
                    k = jnp.concatenate([kp_ref[r, :, hs], kc_ref[r, rows, hs]], axis=0)
                    v = jnp.concatenate([vp_ref[r, :, hs], vc_ref[r, rows, hs]], axis=0)
                else:
                    k = kc_ref[r, window, hs]
                    v = vc_ref[r, window, hs]
                s = lax.dot_general(q_ref[r, rows, hs], k, (((1,), (1,)), ((), ())),
                                    preferred_element_type=F32) + bias_ref[h]
                if j == 0:
                    s = jnp.where(no_prev, NEG_INF, s)
                m = jnp.max(s, axis=-1, keepdims=True)
                p = jnp.exp2(s - m)
                o_ext = _bdot(p.astype(BF16), jnp.concatenate([v, ones], axis=1))
                o, l = o_ext[:, :HEAD_DIM], o_ext[:, HEAD_DIM:]
                o_ref[r, rows, hs] = (o / l).astype(BF16)
                m_tile = jnp.where(lane == h, m, m_tile)
                l_tile = jnp.where(lane == h, l, l_tile)
            lse_ref[r, rows, :] = m_tile * LN2 + jnp.log(l_tile)


def _attention_group(g, qkv, bias):
    batch, dil, sub_len, _ = qkv.shape
    nq = min(ATTN_QBLOCKS, sub_len // BLOCK)
    n_sub = min(dil, ATTN_QBLOCKS // nq)
    assert sub_len % (nq * BLOCK) == 0 and dil % n_sub == 0
    cur = lambda b, r, n: (b, r, n, 0)
    col = lambda c: (lambda b, r, n: (b, r, n, c))
    prev = lambda c: (lambda b, r, n: (b, r, jnp.maximum(n * nq - 1, 0), c))
    blk = (None, n_sub, nq * BLOCK, D)
    blk_prev = (None, n_sub, BLOCK, D)
    return pl.pallas_call(
        _attn_kernel,
        grid=(batch, dil // n_sub, sub_len // (nq * BLOCK)),
        in_specs=[
            pl.BlockSpec(blk, col(0)),
            pl.BlockSpec(blk_prev, prev(1)),
            pl.BlockSpec(blk, col(1)),
            pl.BlockSpec(blk_prev, prev(2)),
            pl.BlockSpec(blk, col(2)),
            pl.BlockSpec((None, HEADS, BLOCK, 2 * BLOCK), lambda b, r, n: (g, 0, 0, 0)),
        ],
        out_specs=[pl.BlockSpec(blk, cur),
                   pl.BlockSpec((None, n_sub, nq * BLOCK, LANES), cur)],
        out_shape=[jax.ShapeDtypeStruct((batch, dil, sub_len, D), BF16),
                   jax.ShapeDtypeStruct((batch, dil, sub_len, LANES), F32)],
        compiler_params=_cparams(3),
        name=f"dilated_attn_g{g}",
    )(qkv, qkv, qkv, qkv, qkv, bias)


def _mix_kernel(o0_ref, o1_ref, o2_ref, l0_ref, l1_ref, l2_ref, x_ref, mod_ref,
                expand_ref, wo_ref, ln_ref, out_ref, lse_scr, o_slab):
    tm = x_ref.shape[0]
    n_slabs = D // LANES
    lses = []
    for g, l_ref in enumerate((l0_ref, l1_ref, l2_ref)):
        dil = DIL_GROUPS[g][1]
        if dil == 1:
            lses.append(l_ref[0])
            continue
        for r in range(dil):
            lse_scr[g, pl.ds(r, tm // dil, stride=dil), :] = l_ref[r]
        lses.append(lse_scr[g])
    m = jnp.maximum(lses[0], jnp.maximum(lses[1], lses[2]))
    es = [jnp.exp(a - m) for a in lses]
    den = es[0] + es[1] + es[2]
    expand = expand_ref[...]
    mixed = None
    for g, o_ref in enumerate((o0_ref, o1_ref, o2_ref)):
        dil = DIL_GROUPS[g][1]
        w = _bdot((es[g] / den).astype(BF16), expand)
        if dil == 1:
            o = o_ref[0].astype(F32)
        else:
            for r in range(dil):
                for j in range(n_slabs):
                    o_slab[j, pl.ds(r, tm // dil, stride=dil), :] = (
                        o_ref[r, :, j * LANES:(j + 1) * LANES].astype(F32))
            o = jnp.concatenate([o_slab[j] for j in range(n_slabs)], axis=1)
        mixed = w * o if mixed is None else mixed + w * o
    y = _bdot(mixed.astype(BF16), wo_ref[...])
    g1 = mod_ref[...][2:3]
    ln = ln_ref[...]
    out_ref[...] = _layer_norm(ALPHA * x_ref[...] + (1.0 + g1) * y, ln[0:1], ln[1:2])


def _mix(outs, lses, x, mod, wo, ln):
    batch, seq, _ = x.shape
    tm = TOKEN_TILE
    const = lambda b, i: (0, 0)
    tok = pl.BlockSpec((None, tm, D), lambda b, i: (b, i, 0))
    dilated = lambda dil, w: pl.BlockSpec((None, dil, tm // dil, w), lambda b, i: (b, 0, i, 0))
    dils = [dil for _, dil in DIL_GROUPS]
    head_of_lane = jnp.arange(D)[None, :] // HEAD_DIM
    expand = (jnp.arange(LANES)[:, None] == head_of_lane).astype(BF16)
    return pl.pallas_call(
        _mix_kernel,
        grid=(batch, seq // tm),
        in_specs=[dilated(d, D) for d in dils] + [dilated(d, LANES) for d in dils] + [
            tok,
            pl.BlockSpec((None, SUBLANES, D), lambda b, i: (b, 0, 0)),
            pl.BlockSpec((LANES, D), const),
            pl.BlockSpec((D, D), const),
            pl.BlockSpec((2, D), const)],
        out_specs=tok,
        out_shape=jax.ShapeDtypeStruct(x.shape, F32),
        scratch_shapes=[pltpu.VMEM((N_GROUPS, tm, LANES), F32),
                        pltpu.VMEM((D // LANES, tm, LANES), F32)],
        compiler_params=_cparams(2),
        name="attn_mix",
    )(*outs, *lses, x, mod, expand, wo, ln)


def kernel(x, c, ada_w, ada_b, ln_g, ln_b, conv_w_in, conv_w, conv_w_out, kv_ada_w,
           kv_ada_b, w_kv, attn_w_q, attn_w_o, rel_bias, router_w, router_bias,
           moe_w_gate, moe_w_up, moe_w_down):
    batch, seq, _ = x.shape
    assert x.shape[2] == D and batch <= SUBLANES
    assert seq % TOKEN_TILE == 0 and seq % (BLOCK * DIL_GROUPS[-1][1]) == 0

    c8 = jnp.pad(c, ((0, SUBLANES - batch), (0, 0)))
    mods = _ada_linear(c8, ada_w, ada_b)
    kvmods = _ada_linear(c8, kv_ada_w[None], kv_ada_b[None])[0]
    mod0 = _mod_rows(mods[0], batch, 6)
    mod1 = _mod_rows(mods[1], batch, 6)
    kvmod = _mod_rows(kvmods, batch, 2)
    ln = jnp.stack([ln_g, ln_b], axis=2)

    rw = router_w.T.astype(BF16)
    rbias = router_bias.reshape(N_EXPERTS, 1)
    wgu = jnp.concatenate([moe_w_gate, moe_w_up], axis=-1).astype(BF16)
    wd = moe_w_down.astype(BF16).reshape(DEPTH, N_EXPERTS * EXPERT_FF, D)
    conv_w8 = jnp.pad(conv_w[0], ((0, SUBLANES - CONV_WIDTH), (0, 0)))

    x = _conv_mixer(x, mod0, conv_w_in[0].astype(BF16), conv_w8,
                    conv_w_out[0].astype(BF16), ln[0, 0])
    x = _moe(x, mod0, rw, rbias, wgu[0], wd[0], ln[0, 1])

    qkv = _qkv(x, mod1, kvmod, attn_w_q[0].astype(BF16), w_kv.astype(BF16))
    bias = _bias_tiles(rel_bias)
    outs, lses = zip(*[_attention_group(g, qkv[g], bias) for g in range(N_GROUPS)])
    x = _mix(outs, lses, x, mod1, attn_w_o[0].astype(BF16), ln[1, 0])
    x = _moe(x, mod1, rw, rbias, wgu[1], wd[1], ln[1, 1])
    return x
```

```python
import functools
import math

import jax
import jax.numpy as jnp
from jax import lax
from jax.experimental import pallas as pl
from jax.experimental.pallas import tpu as pltpu

F32 = jnp.float32
BF16 = jnp.bfloat16

D = 1024
DEPTH = 2
CONV_WIDTH = 3
DIL_GROUPS = ((128, 1), (512, 4), (2048, 16))
N_GROUPS = len(DIL_GROUPS)
HEAD_DIM = 128
HEADS = D // HEAD_DIM
QW = N_GROUPS * D
BLOCK = 128
NUM_BUCKETS = 32
MAX_DISTANCE = 2048
N_EXPERTS = 16
EXPERTS_PER_GROUP = 4
N_EXPERT_GROUPS = N_EXPERTS // EXPERTS_PER_GROUP
EXPERT_FF = D // 4
ALPHA = (2 * DEPTH) ** 0.25
LN_EPS = 1e-5
NEG_INF = -1e30
LOG2E = math.log2(math.e)
LN2 = math.log(2.0)

LANES = 128
SUBLANES = 8
VMEM_LIMIT = 56 * 1024 * 1024
TOKEN_TILE = 512
MOE_WIN_BIG = 256
MOE_WIN_MID = 192
MOE_WIN_SMALL = 128
BF16_ROWS = 16
ATTN_QBLOCKS = 16
ADA_COL_TILE = 2048


def _cparams(n_axes):
    return pltpu.CompilerParams(
        dimension_semantics=("arbitrary",) * n_axes, vmem_limit_bytes=VMEM_LIMIT)


def _layer_norm(r, g, b):
    mu = jnp.mean(r, axis=-1, keepdims=True)
    c = r - mu
    var = jnp.mean(c * c, axis=-1, keepdims=True)
    return c * lax.rsqrt(var + LN_EPS) * g + b


def _bdot(a, b):
    return jnp.dot(a, b, preferred_element_type=F32)


def _ada_kernel(c_ref, w_ref, b_ref, o_ref):
    c = c_ref[...]
    cond = c * jax.nn.sigmoid(c)
    o_ref[...] = _bdot(cond.astype(BF16), w_ref[...].astype(BF16)) + b_ref[...]


def _ada_linear(c8, w, bias):
    n_layers, _, n = w.shape
    nt = ADA_COL_TILE
    return pl.pallas_call(
        _ada_kernel,
        grid=(n_layers, n // nt),
        in_specs=[pl.BlockSpec((SUBLANES, D), lambda l, j: (0, 0)),
                  pl.BlockSpec((None, D, nt), lambda l, j: (l, 0, j)),
                  pl.BlockSpec((None, 1, nt), lambda l, j: (l, 0, j))],
        out_specs=pl.BlockSpec((None, SUBLANES, nt), lambda l, j: (l, 0, j)),
        out_shape=jax.ShapeDtypeStruct((n_layers, SUBLANES, n), F32),
        compiler_params=_cparams(2),
        name="ada_linear",
    )(c8, w, bias.reshape(n_layers, 1, n))


def _mod_rows(mods, batch, n_vec):
    m = mods[:batch].reshape(batch, n_vec, D)
    return jnp.pad(m, ((0, 0), (0, SUBLANES - n_vec), (0, 0)))


def _conv_mixer_kernel(x_ref, mod_ref, w_in_ref, cw_ref, w_out_ref, ln_ref,
                       o_ref, u_scr):
    tm = x_ref.shape[0]
    i = pl.program_id(1)
    mod = mod_ref[...]
    sh1, sc1, g1 = mod[0:1], mod[1:2], mod[2:3]
    x = x_ref[...]
    h = (x * (1.0 + sc1) + sh1).astype(BF16)
    cgate = _bdot(h, w_in_ref[:, D:2 * D])
    v = _bdot(h, w_in_ref[:, 2 * D:3 * D])
    u = cgate * v

    @pl.when(i == 0)
    def _():
        u_scr[0:SUBLANES, :] = jnp.zeros((SUBLANES, D), F32)

    u_scr[SUBLANES:SUBLANES + tm, :] = u
    cw = cw_ref[...]
    conv = (cw[2:3] * u
            + cw[1:2] * u_scr[SUBLANES - 1:SUBLANES - 1 + tm, :]
            + cw[0:1] * u_scr[SUBLANES - 2:SUBLANES - 2 + tm, :])
    u_scr[0:SUBLANES, :] = u_scr[tm:tm + SUBLANES, :]
    bgate = _bdot(h, w_in_ref[:, 0:D])
    y = _bdot((bgate * conv).astype(BF16), w_out_ref[...])
    ln = ln_ref[...]
    o_ref[...] = _layer_norm(ALPHA * x + (1.0 + g1) * y, ln[0:1], ln[1:2])


def _conv_mixer(x, mod, w_in, conv_w, w_out, ln):
    batch, seq, _ = x.shape
    tm = TOKEN_TILE
    const = lambda b, i: (0, 0)
    return pl.pallas_call(
        _conv_mixer_kernel,
        grid=(batch, seq // tm),
        in_specs=[pl.BlockSpec((None, tm, D), lambda b, i: (b, i, 0)),
                  pl.BlockSpec((None, SUBLANES, D), lambda b, i: (b, 0, 0)),
                  pl.BlockSpec((D, 3 * D), const),
                  pl.BlockSpec((SUBLANES, D), const),
                  pl.BlockSpec((D, D), const),
                  pl.BlockSpec((2, D), const)],
        out_specs=pl.BlockSpec((None, tm, D), lambda b, i: (b, i, 0)),
        out_shape=jax.ShapeDtypeStruct(x.shape, F32),
        scratch_shapes=[pltpu.VMEM((tm + SUBLANES, D), F32)],
        compiler_params=_cparams(2),
        name="conv_mixer",
    )(x, mod, w_in, conv_w, w_out, ln)


def _route(logits_t, rbias_col):
    aff = jax.nn.sigmoid(logits_t)
    sel = aff + rbias_col
    rows = [sel[e:e + 1, :] for e in range(N_EXPERTS)]
    scores = []
    for g in range(N_EXPERT_GROUPS):
        s0, s1, s2, s3 = rows[EXPERTS_PER_GROUP * g:EXPERTS_PER_GROUP * (g + 1)]
        a, b = jnp.maximum(s0, s1), jnp.minimum(s0, s1)
        c, d = jnp.maximum(s2, s3), jnp.minimum(s2, s3)
        scores.append(jnp.maximum(a, c) + jnp.maximum(jnp.minimum(a, c), jnp.maximum(b, d)))
    best = scores[0]
    best_group = jnp.zeros(best.shape, jnp.int32)
    for g in range(1, N_EXPERT_GROUPS):
        upd = scores[g] > best
        best_group = jnp.where(upd, g, best_group)
        best = jnp.where(upd, scores[g], best)
    e_iota = lax.broadcasted_iota(jnp.int32, sel.shape, 0)
    masked = jnp.where((e_iota // EXPERTS_PER_GROUP) == best_group, sel, NEG_INF)
    m1 = jnp.max(masked, axis=0, keepdims=True)
    idx1 = jnp.min(jnp.where(masked == m1, e_iota, N_EXPERTS), axis=0, keepdims=True)
    masked2 = jnp.where(e_iota == idx1, -jnp.inf, masked)
    m2 = jnp.max(masked2, axis=0, keepdims=True)
    idx2 = jnp.min(jnp.where(masked2 == m2, e_iota, N_EXPERTS), axis=0, keepdims=True)
    a1 = jnp.sum(jnp.where(e_iota == idx1, aff, 0.0), axis=0, keepdims=True)
    a2 = jnp.sum(jnp.where(e_iota == idx2, aff, 0.0), axis=0, keepdims=True)
    den = a1 + a2
    comb = (jnp.where(e_iota == idx1, a1 / den, 0.0)
            + jnp.where(e_iota == idx2, a2 / den, 0.0))
    group1 = idx1 // EXPERTS_PER_GROUP
    split = group1 != idx2 // EXPERTS_PER_GROUP
    return comb, group1, split


def _moe_kernel(x_ref, xp_ref, mod_ref, modp_ref, rw_ref, rb_ref, tri_ref, wg_ref, wu_ref, wd_ref,
                ln_ref, o_ref, hs_scr, cs_scr, ys_scr, pt_scr):
    tm = x_ref.shape[0]
    gff = EXPERTS_PER_GROUP * EXPERT_FF
    t = pl.program_id(0)

    @pl.when(t == 0)
    def _():
        ys_scr[...] = jnp.zeros(ys_scr.shape, F32)
        pt_scr[...] = jnp.zeros(pt_scr.shape, BF16)

    y = _bdot(pt_scr[...], ys_scr[0:tm, :].astype(BF16))
    ln = ln_ref[...]
    o_ref[...] = _layer_norm(ALPHA * xp_ref[...] + (1.0 + modp_ref[...][5:6]) * y,
                             ln[0:1], ln[1:2])

    mod = mod_ref[...]
    sh2, sc2 = mod[3:4], mod[4:5]
    h = x_ref[...] * (1.0 + sc2) + sh2
    hb = h.astype(BF16)
    logits_t = lax.dot_general(rw_ref[...], hb, (((1,), (1,)), ((), ())),
                               preferred_element_type=F32)
    comb_t, group, split = _route(logits_t, rb_ref[...])

    g_iota = lax.broadcasted_iota(jnp.int32, (SUBLANES, tm), 0)
    member = jnp.where(g_iota == group, 1.0, 0.0)
    rank = _bdot(member.astype(BF16), tri_ref[...])
    counts = jnp.sum(member, axis=1, keepdims=True)
    cnt = [counts[g, 0] for g in range(N_EXPERT_GROUPS)]
    start = [0.0]
    for g in range(N_EXPERT_GROUPS - 1):
        start.append(start[g] + cnt[g])
    dest = sum(member[g:g + 1] * (start[g] + rank[g:g + 1]) for g in range(N_EXPERT_GROUPS))
    row = lax.broadcasted_iota(jnp.int32, (tm, tm), 0).astype(F32)
    col = lax.broadcasted_iota(jnp.int32, (tm, tm), 1).astype(F32)
    perm = jnp.where(row == dest, 1.0, 0.0).astype(BF16)
    dest_col = jnp.broadcast_to(dest, (LANES, tm)).T
    dest_col = jnp.concatenate([dest_col] * (tm // LANES), axis=1)
    perm_t = jnp.where(dest_col == col, 1.0, 0.0).astype(BF16)
    pt_scr[...] = perm_t

    hs_scr[0:tm, :] = _bdot(perm, hb).astype(BF16)
    hs_scr[tm:, :] = jnp.zeros((MOE_WIN_BIG, D), BF16)
    cs_scr[tm:, :] = jnp.zeros((MOE_WIN_BIG, LANES), F32)
    comb_hi = comb_t.astype(BF16)
    comb_lo = (comb_t - comb_hi.astype(F32)).astype(BF16)
    cs = _bdot(jnp.concatenate([comb_hi, comb_lo], axis=0), perm_t)
    cs = cs[0:N_EXPERTS] + cs[N_EXPERTS:2 * N_EXPERTS]
    cs_scr[0:tm, :] = jnp.concatenate(
        [cs, jnp.zeros((LANES - N_EXPERTS, tm), F32)], axis=0).T
    ys_scr[...] = jnp.zeros(ys_scr.shape, F32)

    any_split = jnp.max(jnp.where(split, 1.0, 0.0)) > 0.0
    active = t < pl.num_programs(0) - 1
    for g in range(N_EXPERT_GROUPS):
        first = jnp.where(any_split, 0, jnp.asarray(start[g], F32).astype(jnp.int32))
        last = jnp.where(any_split, tm, (start[g] + cnt[g]).astype(jnp.int32))
        first = (first // BF16_ROWS) * BF16_ROWS
        need = jnp.where(jnp.logical_and(active, last > first), last - first, 0)
        n_full = need // MOE_WIN_BIG
        rest = need - n_full * MOE_WIN_BIG
        n_big = n_full + jnp.where(rest > MOE_WIN_MID, 1, 0)
        n_mid = jnp.where(jnp.logical_and(rest > MOE_WIN_SMALL, rest <= MOE_WIN_MID), 1, 0)
        n_small = jnp.where(jnp.logical_and(rest > 0, rest <= MOE_WIN_SMALL), 1, 0)

        def window(start_row, size, g=g):
            rows = pl.ds(pl.multiple_of(start_row, BF16_ROWS), size)
            hs = hs_scr[rows, :]
            parts = []
            for j in range(EXPERTS_PER_GROUP):
                e = g * EXPERTS_PER_GROUP + j
                gate = _bdot(hs, wg_ref[e])
                up = _bdot(hs, wu_ref[e])
                he = gate * jax.nn.sigmoid(gate) * up * cs_scr[rows, e:e + 1]
                parts.append(he.astype(BF16))
            he = jnp.concatenate(parts, axis=1)
            ys_scr[rows, :] += _bdot(he, wd_ref[g * gff:(g + 1) * gff, :])

        def big(w, carry, first=first, window=window):
            window(first + w * MOE_WIN_BIG, MOE_WIN_BIG)
            return carry

        def mid(w, carry, first=first, n_full=n_full, window=window):
            window(first + n_full * MOE_WIN_BIG, MOE_WIN_MID)
            return carry

        def small(w, carry, first=first, n_full=n_full, window=window):
            window(first + n_full * MOE_WIN_BIG, MOE_WIN_SMALL)
            return carry

        lax.fori_loop(0, n_big, big, 0)
        lax.fori_loop(0, n_mid, mid, 0)
        lax.fori_loop(0, n_small, small, 0)


def _moe(x, mod, rw, rbias, wg, wu, wd, ln):
    batch, seq, _ = x.shape
    tm = TOKEN_TILE
    per_batch = seq // tm
    n_tiles = batch * per_batch
    const2 = lambda t: (0, 0)
    cur = lambda t: jnp.minimum(t, n_tiles - 1)
    prev = lambda t: jnp.maximum(t - 1, 0)
    tile = lambda which: pl.BlockSpec(
        (None, tm, D), lambda t: (which(t) // per_batch, which(t) % per_batch, 0))
    mods = lambda which: pl.BlockSpec(
        (None, SUBLANES, D), lambda t: (which(t) // per_batch, 0, 0))
    tri = (jnp.arange(tm)[:, None] < jnp.arange(tm)[None, :]).astype(BF16)
    return pl.pallas_call(
        _moe_kernel,
        grid=(n_tiles + 1,),
        in_specs=[tile(cur), tile(prev), mods(cur), mods(prev),
                  pl.BlockSpec((N_EXPERTS, D), const2),
                  pl.BlockSpec((N_EXPERTS, 1), const2),
                  pl.BlockSpec((tm, tm), const2),
                  pl.BlockSpec((N_EXPERTS, D, EXPERT_FF), lambda t: (0, 0, 0)),
                  pl.BlockSpec((N_EXPERTS, D, EXPERT_FF), lambda t: (0, 0, 0)),
                  pl.BlockSpec((N_EXPERTS * EXPERT_FF, D), const2),
                  pl.BlockSpec((2, D), const2)],
        out_specs=tile(prev),
        out_shape=jax.ShapeDtypeStruct(x.shape, F32),
        scratch_shapes=[pltpu.VMEM((tm + MOE_WIN_BIG, D), BF16),
                        pltpu.VMEM((tm + MOE_WIN_BIG, LANES), F32),
                        pltpu.VMEM((tm + MOE_WIN_BIG, D), F32),
                        pltpu.VMEM((tm, tm), BF16)],
        compiler_params=_cparams(1),
        name="moe",
    )(x, x, mod, mod, rw, rbias, tri, wg, wu, wd, ln)


def _qkv_kernel(x_ref, mod_ref, kvmod_ref, wq_ref, wkv_ref, *refs):
    outs, (slab, hq_scr, hkv_scr) = refs[:3 * N_GROUPS], refs[3 * N_GROUPS:]
    tm = x_ref.shape[0]
    mod = mod_ref[...]
    kvmod = kvmod_ref[...]
    q_sc, q_sh = 1.0 + mod[1:2], mod[0:1]
    kv_sc, kv_sh = 1.0 + kvmod[1:2], kvmod[0:1]
    n_slabs = D // LANES
    for j in range(n_slabs):
        slab[j] = x_ref[:, j * LANES:(j + 1) * LANES]
    scale = HEAD_DIM ** -0.5 * LOG2E
    for g, (_, dil) in enumerate(DIL_GROUPS):
        n = tm // dil
        for r in range(dil):
            rows = slice(r * n, (r + 1) * n)
            for j in range(n_slabs):
                ls = slice(j * LANES, (j + 1) * LANES)
                piece = slab[j] if dil == 1 else slab[j, pl.ds(r, n, stride=dil), :]
                hq_scr[rows, ls] = (piece * q_sc[:, ls] + q_sh[:, ls]).astype(BF16)
                hkv_scr[rows, ls] = (piece * kv_sc[:, ls] + kv_sh[:, ls]).astype(BF16)
        hq = hq_scr[...]
        hkv = hkv_scr[...]
        cols = slice(g * D, (g + 1) * D)
        vcols = slice(QW + g * D, QW + (g + 1) * D)
        q = (_bdot(hq, wq_ref[:, cols]) * scale).astype(BF16)
        k = _bdot(hkv, wkv_ref[:, cols]).astype(BF16)
        v = _bdot(hkv, wkv_ref[:, vcols]).astype(BF16)
        for r in range(dil):
            rows = slice(r * n, (r + 1) * n)
            outs[3 * g][r] = q[rows]
            outs[3 * g + 1][r] = k[rows]
            outs[3 * g + 2][r] = v[rows]


def _qkv(x, mod, kvmod, wq, wkv):
    batch, seq, _ = x.shape
    tm = TOKEN_TILE
    const = lambda b, i: (0, 0)
    out_specs, out_shape = [], []
    for _, dil in DIL_GROUPS:
        for _ in range(3):
            out_specs.append(pl.BlockSpec((None, dil, tm // dil, D), lambda b, i: (b, 0, i, 0)))
            out_shape.append(jax.ShapeDtypeStruct((batch, dil, seq // dil, D), BF16))
    return pl.pallas_call(
        _qkv_kernel,
        grid=(batch, seq // tm),
        in_specs=[pl.BlockSpec((None, tm, D), lambda b, i: (b, i, 0)),
                  pl.BlockSpec((None, SUBLANES, D), lambda b, i: (b, 0, 0)),
                  pl.BlockSpec((None, SUBLANES, D), lambda b, i: (b, 0, 0)),
                  pl.BlockSpec((D, QW), const),
                  pl.BlockSpec((D, 2 * QW), const)],
        out_specs=out_specs,
        out_shape=out_shape,
        scratch_shapes=[pltpu.VMEM((D // LANES, tm, LANES), F32),
                        pltpu.VMEM((tm, D), BF16),
                        pltpu.VMEM((tm, D), BF16)],
        compiler_params=_cparams(2),
        name="qkv_proj",
    )(x, mod, kvmod, wq, wkv)


def _bucket_tables():
    qi = jnp.arange(BLOCK)[:, None]
    kj = jnp.arange(2 * BLOCK)[None, :]
    dist = BLOCK + qi - kj
    max_exact = NUM_BUCKETS // 2
    tables = []
    for window, dil in DIL_GROUPS:
        n = jnp.maximum(dist, 0) * dil
        nf = jnp.maximum(n, 1).astype(F32)
        large = max_exact + (jnp.log(nf / max_exact) / math.log(MAX_DISTANCE / max_exact)
                             * (NUM_BUCKETS - max_exact)).astype(jnp.int32)
        large = jnp.minimum(large, NUM_BUCKETS - 1)
        bucket = jnp.where(n < max_exact, n, large)
        valid = (dist >= 0) & (dist <= window // dil)
        tables.append(jnp.where(valid, bucket, -1).astype(jnp.int32))
    return jnp.stack(tables)


def _bias_kernel(rb_ref, bkt_ref, o_ref):
    g = pl.program_id(0)
    bkt = bkt_ref[...]
    for h in range(HEADS):
        acc = jnp.full(bkt.shape, NEG_INF, F32)
        for b in range(NUM_BUCKETS):
            acc = jnp.where(bkt == b, rb_ref[b, g * HEADS + h] * LOG2E, acc)
        o_ref[h] = acc


def _bias_tiles(rel_bias):
    return pl.pallas_call(
        _bias_kernel,
        grid=(N_GROUPS,),
        in_specs=[pl.BlockSpec(memory_space=pltpu.SMEM),
                  pl.BlockSpec((None, BLOCK, 2 * BLOCK), lambda g: (g, 0, 0))],
        out_specs=pl.BlockSpec((None, HEADS, BLOCK, 2 * BLOCK), lambda g: (g, 0, 0, 0)),
        out_shape=jax.ShapeDtypeStruct((N_GROUPS, HEADS, BLOCK, 2 * BLOCK), F32),
        compiler_params=_cparams(1),
        name="rel_bias_tiles",
    )(rel_bias, _bucket_tables())


def _attn_kernel(q_ref, kp_ref, kc_ref, vp_ref, vc_ref, bias_ref, o_ref, lse_ref):
    n = pl.program_id(2)
    col = lax.broadcasted_iota(jnp.int32, (BLOCK, 2 * BLOCK), 1)
    no_prev = jnp.logical_and(n == 0, col < BLOCK)
    lane = lax.broadcasted_iota(jnp.int32, (BLOCK, LANES), 1)
    ones = jnp.ones((2 * BLOCK, HEAD_DIM), BF16)
    for r in range(q_ref.shape[0]):
        for j in range(q_ref.shape[1] // BLOCK):
            rows = slice(j * BLOCK, (j + 1) * BLOCK)
            window = slice((j - 1) * BLOCK, (j + 1) * BLOCK)
            m_tile = jnp.zeros((BLOCK, LANES), F32)
            l_tile = jnp.ones((BLOCK, LANES), F32)
            for h in range(HEADS):
                hs = slice(h * HEAD_DIM, (h + 1) * HEAD_DIM)
                if j == 0:
                    k = jnp.concatenate([kp_ref[r, :, hs], kc_ref[r, rows, hs]], axis=0)
                    v = jnp.concatenate([vp_ref[r, :, hs], vc_ref[r, rows, hs]], axis=0)
                else:
                    k = kc_ref[r, window, hs]
                    v = vc_ref[r, window, hs]
                s = lax.dot_general(q_ref[r, rows, hs], k, (((1,), (1,)), ((), ())),
                                    preferred_element_type=F32) + bias_ref[h]
                if j == 0:
                    s = jnp.where(no_prev, NEG_INF, s)
                m = jnp.max(s, axis=-1, keepdims=True)
                p = jnp.exp2(s - m)
                o_ext = _bdot(p.astype(BF16), jnp.concatenate([v, ones], axis=1))
                o, l = o_ext[:, :HEAD_DIM], o_ext[:, HEAD_DIM:]
                o_ref[r, rows, hs] = (o / l).astype(BF16)
                m_tile = jnp.where(lane == h, m, m_tile)
                l_tile = jnp.where(lane == h, l, l_tile)
            lse_ref[r, rows, :] = m_tile * LN2 + jnp.log(l_tile)


def _attention_group(g, q, k, v, bias):
    batch, dil, sub_len, _ = q.shape
    nq = min(ATTN_QBLOCKS, sub_len // BLOCK)
    n_sub = min(dil, ATTN_QBLOCKS // nq)
    assert sub_len % (nq * BLOCK) == 0 and dil % n_sub == 0
    cur = lambda b, r, n: (b, r, n, 0)
    prev = lambda b, r, n: (b, r, jnp.maximum(n * nq - 1, 0), 0)
    blk = (None, n_sub, nq * BLOCK, D)
    blk_prev = (None, n_sub, BLOCK, D)
    return pl.pallas_call(
        _attn_kernel,
        grid=(batch, dil // n_sub, sub_len // (nq * BLOCK)),
        in_specs=[
            pl.BlockSpec(blk, cur),
            pl.BlockSpec(blk_prev, prev),
            pl.BlockSpec(blk, cur),
            pl.BlockSpec(blk_prev, prev),
            pl.BlockSpec(blk, cur),
            pl.BlockSpec((None, HEADS, BLOCK, 2 * BLOCK), lambda b, r, n: (g, 0, 0, 0)),
        ],
        out_specs=[pl.BlockSpec(blk, cur),
                   pl.BlockSpec((None, n_sub, nq * BLOCK, LANES), cur)],
        out_shape=[jax.ShapeDtypeStruct((batch, dil, sub_len, D), BF16),
                   jax.ShapeDtypeStruct((batch, dil, sub_len, LANES), F32)],
        compiler_params=_cparams(3),
        name=f"dilated_attn_g{g}",
    )(q, k, k, v, v, bias)


def _mix_kernel(o0_ref, o1_ref, o2_ref, l0_ref, l1_ref, l2_ref, x_ref, mod_ref,
                expand_ref, wo_ref, ln_ref, out_ref, lse_scr, o_slab):
    tm = x_ref.shape[0]
    n_slabs = D // LANES
    lses = []
    for g, l_ref in enumerate((l0_ref, l1_ref, l2_ref)):
        dil = DIL_GROUPS[g][1]
        if dil == 1:
            lses.append(l_ref[0])
            continue
        for r in range(dil):
            lse_scr[g, pl.ds(r, tm // dil, stride=dil), :] = l_ref[r]
        lses.append(lse_scr[g])
    m = jnp.maximum(lses[0], jnp.maximum(lses[1], lses[2]))
    es = [jnp.exp(a - m) for a in lses]
    den = es[0] + es[1] + es[2]
    expand = expand_ref[...]
    mixed = None
    for g, o_ref in enumerate((o0_ref, o1_ref, o2_ref)):
        dil = DIL_GROUPS[g][1]
        w = _bdot((es[g] / den).astype(BF16), expand)
        if dil == 1:
            o = o_ref[0].astype(F32)
        else:
            for r in range(dil):
                for j in range(n_slabs):
                    o_slab[j, pl.ds(r, tm // dil, stride=dil), :] = (
                        o_ref[r, :, j * LANES:(j + 1) * LANES].astype(F32))
            o = jnp.concatenate([o_slab[j] for j in range(n_slabs)], axis=1)
        mixed = w * o if mixed is None else mixed + w * o
    y = _bdot(mixed.astype(BF16), wo_ref[...])
    g1 = mod_ref[...][2:3]
    ln = ln_ref[...]
    out_ref[...] = _layer_norm(ALPHA * x_ref[...] + (1.0 + g1) * y, ln[0:1], ln[1:2])


def _mix(outs, lses, x, mod, wo, ln):
    batch, seq, _ = x.shape
    tm = TOKEN_TILE
    const = lambda b, i: (0, 0)
    tok = pl.BlockSpec((None, tm, D), lambda b, i: (b, i, 0))
    dilated = lambda dil, w: pl.BlockSpec((None, dil, tm // dil, w), lambda b, i: (b, 0, i, 0))
    dils = [dil for _, dil in DIL_GROUPS]
    head_of_lane = jnp.arange(D)[None, :] // HEAD_DIM
    expand = (jnp.arange(LANES)[:, None] == head_of_lane).astype(BF16)
    return pl.pallas_call(
        _mix_kernel,
        grid=(batch, seq // tm),
        in_specs=[dilated(d, D) for d in dils] + [dilated(d, LANES) for d in dils] + [
            tok,
            pl.BlockSpec((None, SUBLANES, D), lambda b, i: (b, 0, 0)),
            pl.BlockSpec((LANES, D), const),
            pl.BlockSpec((D, D), const),
            pl.BlockSpec((2, D), const)],
        out_specs=tok,
        out_shape=jax.ShapeDtypeStruct(x.shape, F32),
        scratch_shapes=[pltpu.VMEM((N_GROUPS, tm, LANES), F32),
                        pltpu.VMEM((D // LANES, tm, LANES), F32)],
        compiler_params=_cparams(2),
        name="attn_mix",
    )(*outs, *lses, x, mod, expand, wo, ln)


def kernel(x, c, ada_w, ada_b, ln_g, ln_b, conv_w_in, conv_w, conv_w_out, kv_ada_w,
           kv_ada_b, w_kv, attn_w_q, attn_w_o, rel_bias, router_w, router_bias,
           moe_w_gate, moe_w_up, moe_w_down):
    batch, seq, _ = x.shape
    assert x.shape[2] == D and batch <= SUBLANES
    assert seq % TOKEN_TILE == 0 and seq % (BLOCK * DIL_GROUPS[-1][1]) == 0

    c8 = jnp.pad(c, ((0, SUBLANES - batch), (0, 0)))
    mods = _ada_linear(c8, ada_w, ada_b)
    kvmods = _ada_linear(c8, kv_ada_w[None], kv_ada_b[None])[0]
    mod0 = _mod_rows(mods[0], batch, 6)
    mod1 = _mod_rows(mods[1], batch, 6)
    kvmod = _mod_rows(kvmods, batch, 2)
    ln = jnp.stack([ln_g, ln_b], axis=2)

    rw = router_w.T.astype(BF16)
    rbias = router_bias.reshape(N_EXPERTS, 1)
    wg = moe_w_gate.astype(BF16)
    wu = moe_w_up.astype(BF16)
    wd = moe_w_down.astype(BF16).reshape(DEPTH, N_EXPERTS * EXPERT_FF, D)
    conv_w8 = jnp.pad(conv_w[0], ((0, SUBLANES - CONV_WIDTH), (0, 0)))

    x = _conv_mixer(x, mod0, conv_w_in[0].astype(BF16), conv_w8,
                    conv_w_out[0].astype(BF16), ln[0, 0])
    x = _moe(x, mod0, rw, rbias, wg[0], wu[0], wd[0], ln[0, 1])

    qkv = _qkv(x, mod1, kvmod, attn_w_q[0].astype(BF16), w_kv.astype(BF16))
    bias = _bias_tiles(rel_bias)
    outs, lses = zip(*[_attention_group(g, *qkv[3 * g:3 * g + 3], bias)
                       for g in range(N_GROUPS)])
    x = _mix(outs, lses, x, mod1, attn_w_o[0].astype(BF16), ln[1, 0])
    x = _moe(x, mod1, rw, rbias, wg[1], wu[1], wd[1], ln[1, 1])
    return x
```

```python
import functools
import math

import jax
import jax.numpy as jnp
from jax import lax
from jax.experimental import pallas as pl
from jax.experimental.pallas import tpu as pltpu

F32 = jnp.float32
BF16 = jnp.bfloat16

D = 1024
DEPTH = 2
CONV_WIDTH = 3
DIL_GROUPS = ((128, 1), (512, 4), (2048, 16))
N_GROUPS = len(DIL_GROUPS)
HEAD_DIM = 128
HEADS = D // HEAD_DIM
QW = N_GROUPS * D
BLOCK = 128
NUM_BUCKETS = 32
MAX_DISTANCE = 2048
N_EXPERTS = 16
EXPERTS_PER_GROUP = 4
N_EXPERT_GROUPS = N_EXPERTS // EXPERTS_PER_GROUP
EXPERT_FF = D // 4
ALPHA = (2 * DEPTH) ** 0.25
LN_EPS = 1e-5
NEG_INF = -1e30
LOG2E = math.log2(math.e)
LN2 = math.log(2.0)

LANES = 128
SUBLANES = 8
VMEM_LIMIT = 56 * 1024 * 1024
TOKEN_TILE = 512
MOE_WIN_BIG = 256
MOE_WIN_MID = 192
MOE_WIN_SMALL = 128
BF16_ROWS = 16
ATTN_QBLOCKS = 16
ADA_COL_TILE = 2048


def _cparams(n_axes):
    return pltpu.CompilerParams(
        dimension_semantics=("arbitrary",) * n_axes, vmem_limit_bytes=VMEM_LIMIT)


def _layer_norm(r, g, b):
    mu = jnp.mean(r, axis=-1, keepdims=True)
    c = r - mu
    var = jnp.mean(c * c, axis=-1, keepdims=True)
    return c * lax.rsqrt(var + LN_EPS) * g + b


def _bdot(a, b):
    return jnp.dot(a, b, preferred_element_type=F32)


def _ada_kernel(c_ref, w_ref, b_ref, o_ref):
    c = c_ref[...]
    cond = c * jax.nn.sigmoid(c)
    o_ref[...] = _bdot(cond.astype(BF16), w_ref[...].astype(BF16)) + b_ref[...]


def _ada_linear(c8, w, bias):
    n_layers, _, n = w.shape
    nt = ADA_COL_TILE
    return pl.pallas_call(
        _ada_kernel,
        grid=(n_layers, n // nt),
        in_specs=[pl.BlockSpec((SUBLANES, D), lambda l, j: (0, 0)),
                  pl.BlockSpec((None, D, nt), lambda l, j: (l, 0, j)),
                  pl.BlockSpec((None, 1, nt), lambda l, j: (l, 0, j))],
        out_specs=pl.BlockSpec((None, SUBLANES, nt), lambda l, j: (l, 0, j)),
        out_shape=jax.ShapeDtypeStruct((n_layers, SUBLANES, n), F32),
        compiler_params=_cparams(2),
        name="ada_linear",
    )(c8, w, bias.reshape(n_layers, 1, n))


def _mod_rows(mods, batch, n_vec):
    m = mods[:batch].reshape(batch, n_vec, D)
    return jnp.pad(m, ((0, 0), (0, SUBLANES - n_vec), (0, 0)))


def _conv_mixer_kernel(x_ref, mod_ref, w_in_ref, cw_ref, w_out_ref, ln_ref,
                       o_ref, u_scr):
    tm = x_ref.shape[0]
    i = pl.program_id(1)
    mod = mod_ref[...]
    sh1, sc1, g1 = mod[0:1], mod[1:2], mod[2:3]
    x = x_ref[...]
    h = (x * (1.0 + sc1) + sh1).astype(BF16)
    cgate = _bdot(h, w_in_ref[:, D:2 * D])
    v = _bdot(h, w_in_ref[:, 2 * D:3 * D])
    u = cgate * v

    @pl.when(i == 0)
    def _():
        u_scr[0:SUBLANES, :] = jnp.zeros((SUBLANES, D), F32)

    u_scr[SUBLANES:SUBLANES + tm, :] = u
    cw = cw_ref[...]
    conv = (cw[2:3] * u
            + cw[1:2] * u_scr[SUBLANES - 1:SUBLANES - 1 + tm, :]
            + cw[0:1] * u_scr[SUBLANES - 2:SUBLANES - 2 + tm, :])
    u_scr[0:SUBLANES, :] = u_scr[tm:tm + SUBLANES, :]
    bgate = _bdot(h, w_in_ref[:, 0:D])
    y = _bdot((bgate * conv).astype(BF16), w_out_ref[...])
    ln = ln_ref[...]
    o_ref[...] = _layer_norm(ALPHA * x + (1.0 + g1) * y, ln[0:1], ln[1:2])


def _conv_mixer(x, mod, w_in, conv_w, w_out, ln):
    batch, seq, _ = x.shape
    tm = TOKEN_TILE
    const = lambda b, i: (0, 0)
    return pl.pallas_call(
        _conv_mixer_kernel,
        grid=(batch, seq // tm),
        in_specs=[pl.BlockSpec((None, tm, D), lambda b, i: (b, i, 0)),
                  pl.BlockSpec((None, SUBLANES, D), lambda b, i: (b, 0, 0)),
                  pl.BlockSpec((D, 3 * D), const),
                  pl.BlockSpec((SUBLANES, D), const),
                  pl.BlockSpec((D, D), const),
                  pl.BlockSpec((2, D), const)],
        out_specs=pl.BlockSpec((None, tm, D), lambda b, i: (b, i, 0)),
        out_shape=jax.ShapeDtypeStruct(x.shape, F32),
        scratch_shapes=[pltpu.VMEM((tm + SUBLANES, D), F32)],
        compiler_params=_cparams(2),
        name="conv_mixer",
    )(x, mod, w_in, conv_w, w_out, ln)


def _route(logits_t, rbias_col):
    aff = jax.nn.sigmoid(logits_t)
    sel = aff + rbias_col
    rows = [sel[e:e + 1, :] for e in range(N_EXPERTS)]
    scores = []
    for g in range(N_EXPERT_GROUPS):
        s0, s1, s2, s3 = rows[EXPERTS_PER_GROUP * g:EXPERTS_PER_GROUP * (g + 1)]
        a, b = jnp.maximum(s0, s1), jnp.minimum(s0, s1)
        c, d = jnp.maximum(s2, s3), jnp.minimum(s2, s3)
        scores.append(jnp.maximum(a, c) + jnp.maximum(jnp.minimum(a, c), jnp.maximum(b, d)))
    best = scores[0]
    best_group = jnp.zeros(best.shape, jnp.int32)
    for g in range(1, N_EXPERT_GROUPS):
        upd = scores[g] > best
        best_group = jnp.where(upd, g, best_group)
        best = jnp.where(upd, scores[g], best)
    e_iota = lax.broadcasted_iota(jnp.int32, sel.shape, 0)
    masked = jnp.where((e_iota // EXPERTS_PER_GROUP) == best_group, sel, NEG_INF)
    m1 = jnp.max(masked, axis=0, keepdims=True)
    idx1 = jnp.min(jnp.where(masked == m1, e_iota, N_EXPERTS), axis=0, keepdims=True)
    masked2 = jnp.where(e_iota == idx1, -jnp.inf, masked)
    m2 = jnp.max(masked2, axis=0, keepdims=True)
    idx2 = jnp.min(jnp.where(masked2 == m2, e_iota, N_EXPERTS), axis=0, keepdims=True)
    a1 = jnp.sum(jnp.where(e_iota == idx1, aff, 0.0), axis=0, keepdims=True)
    a2 = jnp.sum(jnp.where(e_iota == idx2, aff, 0.0), axis=0, keepdims=True)
    den = a1 + a2
    comb = (jnp.where(e_iota == idx1, a1 / den, 0.0)
            + jnp.where(e_iota == idx2, a2 / den, 0.0))
    group1 = idx1 // EXPERTS_PER_GROUP
    split = group1 != idx2 // EXPERTS_PER_GROUP
    return comb, group1, split


def _moe_kernel(x_ref, xp_ref, mod_ref, modp_ref, rw_ref, rb_ref, tri_ref, wg_ref, wu_ref, wd_ref,
                ln_ref, o_ref, hs_scr, cs_scr, ys_scr, pt_scr):
    tm = x_ref.shape[0]
    gff = EXPERTS_PER_GROUP * EXPERT_FF
    t = pl.program_id(0)

    @pl.when(t == 0)
    def _():
        ys_scr[...] = jnp.zeros(ys_scr.shape, F32)
        pt_scr[...] = jnp.zeros(pt_scr.shape, BF16)

    y = _bdot(pt_scr[...], ys_scr[0:tm, :].astype(BF16))
    ln = ln_ref[...]
    o_ref[...] = _layer_norm(ALPHA * xp_ref[...] + (1.0 + modp_ref[...][5:6]) * y,
                             ln[0:1], ln[1:2])

    mod = mod_ref[...]
    sh2, sc2 = mod[3:4], mod[4:5]
    h = x_ref[...] * (1.0 + sc2) + sh2
    hb = h.astype(BF16)
    logits_t = lax.dot_general(rw_ref[...], hb, (((1,), (1,)), ((), ())),
                               preferred_element_type=F32)
    comb_t, group, split = _route(logits_t, rb_ref[...])

    g_iota = lax.broadcasted_iota(jnp.int32, (SUBLANES, tm), 0)
    member = jnp.where(g_iota == group, 1.0, 0.0)
    rank = _bdot(member.astype(BF16), tri_ref[...])
    counts = jnp.sum(member, axis=1, keepdims=True)
    cnt = [counts[g, 0] for g in range(N_EXPERT_GROUPS)]
    start = [0.0]
    for g in range(N_EXPERT_GROUPS - 1):
        start.append(start[g] + cnt[g])
    dest = sum(member[g:g + 1] * (start[g] + rank[g:g + 1]) for g in range(N_EXPERT_GROUPS))
    row = lax.broadcasted_iota(jnp.int32, (tm, tm), 0).astype(F32)
    col = lax.broadcasted_iota(jnp.int32, (tm, tm), 1).astype(F32)
    perm = jnp.where(row == dest, 1.0, 0.0).astype(BF16)
    dest_col = jnp.broadcast_to(dest, (LANES, tm)).T
    dest_col = jnp.concatenate([dest_col] * (tm // LANES), axis=1)
    perm_t = jnp.where(dest_col == col, 1.0, 0.0).astype(BF16)
    pt_scr[...] = perm_t

    hs_scr[0:tm, :] = _bdot(perm, hb).astype(BF16)
    hs_scr[tm:, :] = jnp.zeros((MOE_WIN_BIG, D), BF16)
    cs_scr[tm:, :] = jnp.zeros((MOE_WIN_BIG, LANES), F32)
    comb_hi = comb_t.astype(BF16)
    comb_lo = (comb_t - comb_hi.astype(F32)).astype(BF16)
    cs = _bdot(jnp.concatenate([comb_hi, comb_lo], axis=0), perm_t)
    cs = cs[0:N_EXPERTS] + cs[N_EXPERTS:2 * N_EXPERTS]
    cs_scr[0:tm, :] = jnp.concatenate(
        [cs, jnp.zeros((LANES - N_EXPERTS, tm), F32)], axis=0).T
    ys_scr[...] = jnp.zeros(ys_scr.shape, F32)

    any_split = jnp.max(jnp.where(split, 1.0, 0.0)) > 0.0
    active = t < pl.num_programs(0) - 1
    for g in range(N_EXPERT_GROUPS):
        first = jnp.where(any_split, 0, jnp.asarray(start[g], F32).astype(jnp.int32))
        last = jnp.where(any_split, tm, (start[g] + cnt[g]).astype(jnp.int32))
        first = (first // BF16_ROWS) * BF16_ROWS
        need = jnp.where(jnp.logical_and(active, last > first), last - first, 0)
        n_full = need // MOE_WIN_BIG
        rest = need - n_full * MOE_WIN_BIG
        n_big = n_full + jnp.where(rest > MOE_WIN_MID, 1, 0)
        n_mid = jnp.where(jnp.logical_and(rest > MOE_WIN_SMALL, rest <= MOE_WIN_MID), 1, 0)
        n_small = jnp.where(jnp.logical_and(rest > 0, rest <= MOE_WIN_SMALL), 1, 0)

        def window(start_row, size, g=g):
            rows = pl.ds(pl.multiple_of(start_row, BF16_ROWS), size)
            hs = hs_scr[rows, :]
            parts = []
            for j in range(EXPERTS_PER_GROUP):
                e = g * EXPERTS_PER_GROUP + j
                gate = _bdot(hs, wg_ref[e])
                up = _bdot(hs, wu_ref[e])
                he = gate * jax.nn.sigmoid(gate) * up * cs_scr[rows, e:e + 1]
                parts.append(he.astype(BF16))
            he = jnp.concatenate(parts, axis=1)
            ys_scr[rows, :] += _bdot(he, wd_ref[g * gff:(g + 1) * gff, :])

        def big(w, carry, first=first, window=window):
            window(first + w * MOE_WIN_BIG, MOE_WIN_BIG)
            return carry

        def mid(w, carry, first=first, n_full=n_full, window=window):
            window(first + n_full * MOE_WIN_BIG, MOE_WIN_MID)
            return carry

        def small(w, carry, first=first, n_full=n_full, window=window):
            window(first + n_full * MOE_WIN_BIG, MOE_WIN_SMALL)
            return carry

        lax.fori_loop(0, n_big, big, 0)
        lax.fori_loop(0, n_mid, mid, 0)
        lax.fori_loop(0, n_small, small, 0)


def _moe(x, mod, rw, rbias, wg, wu, wd, ln):
    batch, seq, _ = x.shape
    tm = TOKEN_TILE
    per_batch = seq // tm
    n_tiles = batch * per_batch
    const2 = lambda t: (0, 0)
    cur = lambda t: jnp.minimum(t, n_tiles - 1)
    prev = lambda t: jnp.maximum(t - 1, 0)
    tile = lambda which: pl.BlockSpec(
        (None, tm, D), lambda t: (which(t) // per_batch, which(t) % per_batch, 0))
    mods = lambda which: pl.BlockSpec(
        (None, SUBLANES, D), lambda t: (which(t) // per_batch, 0, 0))
    tri = (jnp.arange(tm)[:, None] < jnp.arange(tm)[None, :]).astype(BF16)
    return pl.pallas_call(
        _moe_kernel,
        grid=(n_tiles + 1,),
        in_specs=[tile(cur), tile(prev), mods(cur), mods(prev),
                  pl.BlockSpec((N_EXPERTS, D), const2),
                  pl.BlockSpec((N_EXPERTS, 1), const2),
                  pl.BlockSpec((tm, tm), const2),
                  pl.BlockSpec((N_EXPERTS, D, EXPERT_FF), lambda t: (0, 0, 0)),
                  pl.BlockSpec((N_EXPERTS, D, EXPERT_FF), lambda t: (0, 0, 0)),
                  pl.BlockSpec((N_EXPERTS * EXPERT_FF, D), const2),
                  pl.BlockSpec((2, D), const2)],
        out_specs=tile(prev),
        out_shape=jax.ShapeDtypeStruct(x.shape, F32),
        scratch_shapes=[pltpu.VMEM((tm + MOE_WIN_BIG, D), BF16),
                        pltpu.VMEM((tm + MOE_WIN_BIG, LANES), F32),
                        pltpu.VMEM((tm + MOE_WIN_BIG, D), F32),
                        pltpu.VMEM((tm, tm), BF16)],
        compiler_params=_cparams(1),
        name="moe",
    )(x, x, mod, mod, rw, rbias, tri, wg, wu, wd, ln)


def _qkv_kernel(x_ref, mod_ref, kvmod_ref, wq_ref, wkv_ref, *refs):
    outs, (slab, hq_scr, hkv_scr) = refs[:3 * N_GROUPS], refs[3 * N_GROUPS:]
    tm = x_ref.shape[0]
    mod = mod_ref[...]
    kvmod = kvmod_ref[...]
    q_sc, q_sh = 1.0 + mod[1:2], mod[0:1]
    kv_sc, kv_sh = 1.0 + kvmod[1:2], kvmod[0:1]
    n_slabs = D // LANES
    for j in range(n_slabs):
        slab[j] = x_ref[:, j * LANES:(j + 1) * LANES]
    scale = HEAD_DIM ** -0.5 * LOG2E
    for g, (_, dil) in enumerate(DIL_GROUPS):
        n = tm // dil
        for r in range(dil):
            rows = slice(r * n, (r + 1) * n)
            for j in range(n_slabs):
                ls = slice(j * LANES, (j + 1) * LANES)
                piece = slab[j] if dil == 1 else slab[j, pl.ds(r, n, stride=dil), :]
                hq_scr[rows, ls] = (piece * q_sc[:, ls] + q_sh[:, ls]).astype(BF16)
                hkv_scr[rows, ls] = (piece * kv_sc[:, ls] + kv_sh[:, ls]).astype(BF16)
        hq = hq_scr[...]
        hkv = hkv_scr[...]
        cols = slice(g * D, (g + 1) * D)
        vcols = slice(QW + g * D, QW + (g + 1) * D)
        q = (_bdot(hq, wq_ref[:, cols]) * scale).astype(BF16)
        k = _bdot(hkv, wkv_ref[:, cols]).astype(BF16)
        v = _bdot(hkv, wkv_ref[:, vcols]).astype(BF16)
        for r in range(dil):
            rows = slice(r * n, (r + 1) * n)
            outs[3 * g][r] = q[rows]
            outs[3 * g + 1][r] = k[rows]
            outs[3 * g + 2][r] = v[rows]


def _qkv(x, mod, kvmod, wq, wkv):
    batch, seq, _ = x.shape
    tm = TOKEN_TILE
    const = lambda b, i: (0, 0)
    out_specs, out_shape = [], []
    for _, dil in DIL_GROUPS:
        for _ in range(3):
            out_specs.append(pl.BlockSpec((None, dil, tm // dil, D), lambda b, i: (b, 0, i, 0)))
            out_shape.append(jax.ShapeDtypeStruct((batch, dil, seq // dil, D), BF16))
    return pl.pallas_call(
        _qkv_kernel,
        grid=(batch, seq // tm),
        in_specs=[pl.BlockSpec((None, tm, D), lambda b, i: (b, i, 0)),
                  pl.BlockSpec((None, SUBLANES, D), lambda b, i: (b, 0, 0)),
                  pl.BlockSpec((None, SUBLANES, D), lambda b, i: (b, 0, 0)),
                  pl.BlockSpec((D, QW), const),
                  pl.BlockSpec((D, 2 * QW), const)],
        out_specs=out_specs,
        out_shape=out_shape,
        scratch_shapes=[pltpu.VMEM((D // LANES, tm, LANES), F32),
                        pltpu.VMEM((tm, D), BF16),
                        pltpu.VMEM((tm, D), BF16)],
        compiler_params=_cparams(2),
        name="qkv_proj",
    )(x, mod, kvmod, wq, wkv)


def _bucket_tables():
    qi = jnp.arange(BLOCK)[:, None]
    kj = jnp.arange(2 * BLOCK)[None, :]
    dist = BLOCK + qi - kj
    max_exact = NUM_BUCKETS // 2
    tables = []
    for window, dil in DIL_GROUPS:
        n = jnp.maximum(dist, 0) * dil
        nf = jnp.maximum(n, 1).astype(F32)
        large = max_exact + (jnp.log(nf / max_exact) / math.log(MAX_DISTANCE / max_exact)
                             * (NUM_BUCKETS - max_exact)).astype(jnp.int32)
        large = jnp.minimum(large, NUM_BUCKETS - 1)
        bucket = jnp.where(n < max_exact, n, large)
        valid = (dist >= 0) & (dist <= window // dil)
        tables.append(jnp.where(valid, bucket, -1).astype(jnp.int32))
    return jnp.stack(tables)


def _bias_kernel(rb_ref, bkt_ref, o_ref):
    g = pl.program_id(0)
    bkt = bkt_ref[...]
    for h in range(HEADS):
        acc = jnp.full(bkt.shape, NEG_INF, F32)
        for b in range(NUM_BUCKETS):
            acc = jnp.where(bkt == b, rb_ref[b, g * HEADS + h] * LOG2E, acc)
        o_ref[h] = acc


def _bias_tiles(rel_bias):
    return pl.pallas_call(
        _bias_kernel,
        grid=(N_GROUPS,),
        in_specs=[pl.BlockSpec(memory_space=pltpu.SMEM),
                  pl.BlockSpec((None, BLOCK, 2 * BLOCK), lambda g: (g, 0, 0))],
        out_specs=pl.BlockSpec((None, HEADS, BLOCK, 2 * BLOCK), lambda g: (g, 0, 0, 0)),
        out_shape=jax.ShapeDtypeStruct((N_GROUPS, HEADS, BLOCK, 2 * BLOCK), F32),
        compiler_params=_cparams(1),
        name="rel_bias_tiles",
    )(rel_bias, _bucket_tables())


def _attn_kernel(q_ref, kp_ref, kc_ref, vp_ref, vc_ref, bias_ref, o_ref, lse_ref):
    n = pl.program_id(2)
    col = lax.broadcasted_iota(jnp.int32, (BLOCK, 2 * BLOCK), 1)
    no_prev = jnp.logical_and(n == 0, col < BLOCK)
    lane = lax.broadcasted_iota(jnp.int32, (BLOCK, LANES), 1)
    ones = jnp.ones((2 * BLOCK, HEAD_DIM), BF16)
    for r in range(q_ref.shape[0]):
        for j in range(q_ref.shape[1] // BLOCK):
            rows = slice(j * BLOCK, (j + 1) * BLOCK)
            window = slice((j - 1) * BLOCK, (j + 1) * BLOCK)
            m_tile = jnp.zeros((BLOCK, LANES), F32)
            l_tile = jnp.ones((BLOCK, LANES), F32)
            for h in range(HEADS):
                hs = slice(h * HEAD_DIM, (h + 1) * HEAD_DIM)
                if j == 0:
                    k = jnp.concatenate([kp_ref[r, :, hs], kc_ref[r, rows, hs]], axis=0)
                    v = jnp.concatenate([vp_ref[r, :, hs], vc_ref[r, rows, hs]], axis=0)
                else:
                    k = kc_ref[r, window, hs]
                    v = vc_ref[r, window, hs]
                s = lax.dot_general(q_ref[r, rows, hs], k, (((1,), (1,)), ((), ())),
                                    preferred_element_type=F32) + bias_ref[h]
                if j == 0:
                    s = jnp.where(no_prev, NEG_INF, s)
                m = jnp.max(s, axis=-1, keepdims=True)
                p = jnp.exp2(s - m)
                o_ext = _bdot(p.astype(BF16), jnp.concatenate([v, ones], axis=1))
                o, l = o_ext[:, :HEAD_DIM], o_ext[:, HEAD_DIM:]
                o_ref[r, rows, hs] = (o / l).astype(BF16)
                m_tile = jnp.where(lane == h, m, m_tile)
                l_tile = jnp.where(lane == h, l, l_tile)
            lse_ref[r, rows, :] = m_tile * LN2 + jnp.log(l_tile)


def _attention_group(g, q, k, v, bias):
    batch, dil, sub_len, _ = q.shape
    nq = min(ATTN_QBLOCKS, sub_len // BLOCK)
    n_sub = min(dil, ATTN_QBLOCKS // nq)
    assert sub_len % (nq * BLOCK) == 0 and dil % n_sub == 0
    cur = lambda b, r, n: (b, r, n, 0)
    if sub_len == nq * BLOCK:
        prev = lambda b, r, n: (0, 0, 0, 0)
    else:
        prev = lambda b, r, n: (b, r, jnp.maximum(n * nq - 1, 0), 0)
    blk = (None, n_sub, nq * BLOCK, D)
    blk_prev = (None, n_sub, BLOCK, D)
    return pl.pallas_call(
        _attn_kernel,
        grid=(batch, dil // n_sub, sub_len // (nq * BLOCK)),
        in_specs=[
            pl.BlockSpec(blk, cur),
            pl.BlockSpec(blk_prev, prev),
            pl.BlockSpec(blk, cur),
            pl.BlockSpec(blk_prev, prev),
            pl.BlockSpec(blk, cur),
            pl.BlockSpec((None, HEADS, BLOCK, 2 * BLOCK), lambda b, r, n: (g, 0, 0, 0)),
        ],
        out_specs=[pl.BlockSpec(blk, cur),
                   pl.BlockSpec((None, n_sub, nq * BLOCK, LANES), cur)],
        out_shape=[jax.ShapeDtypeStruct((batch, dil, sub_len, D), BF16),
                   jax.ShapeDtypeStruct((batch, dil, sub_len, LANES), F32)],
        compiler_params=_cparams(3),
        name=f"dilated_attn_g{g}",
    )(q, k, k, v, v, bias)


def _mix_kernel(o0_ref, o1_ref, o2_ref, l0_ref, l1_ref, l2_ref, x_ref, mod_ref,
                expand_ref, wo_ref, ln_ref, out_ref, lse_scr, o_slab):
    tm = x_ref.shape[0]
    n_slabs = D // LANES
    lses = []
    for g, l_ref in enumerate((l0_ref, l1_ref, l2_ref)):
        dil = DIL_GROUPS[g][1]
        if dil == 1:
            lses.append(l_ref[0])
            continue
        for r in range(dil):
            lse_scr[g, pl.ds(r, tm // dil, stride=dil), :] = l_ref[r]
        lses.append(lse_scr[g])
    m = jnp.maximum(lses[0], jnp.maximum(lses[1], lses[2]))
    es = [jnp.exp(a - m) for a in lses]
    den = es[0] + es[1] + es[2]
    expand = expand_ref[...]
    mixed = None
    for g, o_ref in enumerate((o0_ref, o1_ref, o2_ref)):
        dil = DIL_GROUPS[g][1]
        w = _bdot((es[g] / den).astype(BF16), expand)
        if dil == 1:
            o = o_ref[0].astype(F32)
        else:
            for r in range(dil):
                for j in range(n_slabs):
                    o_slab[j, pl.ds(r, tm // dil, stride=dil), :] = (
                        o_ref[r, :, j * LANES:(j + 1) * LANES].astype(F32))
            o = jnp.concatenate([o_slab[j] for j in range(n_slabs)], axis=1)
        mixed = w * o if mixed is None else mixed + w * o
    y = _bdot(mixed.astype(BF16), wo_ref[...])
    g1 = mod_ref[...][2:3]
    ln = ln_ref[...]
    out_ref[...] = _layer_norm(ALPHA * x_ref[...] + (1.0 + g1) * y, ln[0:1], ln[1:2])


def _mix(outs, lses, x, mod, wo, ln):
    batch, seq, _ = x.shape
    tm = TOKEN_TILE
    const = lambda b, i: (0, 0)
    tok = pl.BlockSpec((None, tm, D), lambda b, i: (b, i, 0))
    dilated = lambda dil, w: pl.BlockSpec((None, dil, tm // dil, w), lambda b, i: (b, 0, i, 0))
    dils = [dil for _, dil in DIL_GROUPS]
    head_of_lane = jnp.arange(D)[None, :] // HEAD_DIM
    expand = (jnp.arange(LANES)[:, None] == head_of_lane).astype(BF16)
    return pl.pallas_call(
        _mix_kernel,
        grid=(batch, seq // tm),
        in_specs=[dilated(d, D) for d in dils] + [dilated(d, LANES) for d in dils] + [
            tok,
            pl.BlockSpec((None, SUBLANES, D), lambda b, i: (b, 0, 0)),
            pl.BlockSpec((LANES, D), const),
            pl.BlockSpec((D, D), const),
            pl.BlockSpec((2, D), const)],
        out_specs=tok,
        out_shape=jax.ShapeDtypeStruct(x.shape, F32),
        scratch_shapes=[pltpu.VMEM((N_GROUPS, tm, LANES), F32),
                        pltpu.VMEM((D // LANES, tm, LANES), F32)],
        compiler_params=_cparams(2),
        name="attn_mix",
    )(*outs, *lses, x, mod, expand, wo, ln)


def kernel(x, c, ada_w, ada_b, ln_g, ln_b, conv_w_in, conv_w, conv_w_out, kv_ada_w,
           kv_ada_b, w_kv, attn_w_q, attn_w_o, rel_bias, router_w, router_bias,
           moe_w_gate, moe_w_up, moe_w_down):
    batch, seq, _ = x.shape
    assert x.shape[2] == D and batch <= SUBLANES
    assert seq % TOKEN_TILE == 0 and seq % (BLOCK * DIL_GROUPS[-1][1]) == 0

    c8 = jnp.pad(c, ((0, SUBLANES - batch), (0, 0)))
    mods = _ada_linear(c8, ada_w, ada_b)
    kvmods = _ada_linear(c8, kv_ada_w[None], kv_ada_b[None])[0]
    mod0 = _mod_rows(mods[0], batch, 6)
    mod1 = _mod_rows(mods[1], batch, 6)
    kvmod = _mod_rows(kvmods, batch, 2)
    ln = jnp.stack([ln_g, ln_b], axis=2)

    rw = router_w.T.astype(BF16)
    rbias = router_bias.reshape(N_EXPERTS, 1)
    experts = lambda l: (moe_w_gate[l].astype(BF16), moe_w_up[l].astype(BF16),
                         moe_w_down[l].astype(BF16).reshape(N_EXPERTS * EXPERT_FF, D))
    conv_w8 = jnp.pad(conv_w[0], ((0, SUBLANES - CONV_WIDTH), (0, 0)))

    x = _conv_mixer(x, mod0, conv_w_in[0].astype(BF16), conv_w8,
                    conv_w_out[0].astype(BF16), ln[0, 0])
    x = _moe(x, mod0, rw, rbias, *experts(0), ln[0, 1])

    qkv = _qkv(x, mod1, kvmod, attn_w_q[0].astype(BF16), w_kv.astype(BF16))
    bias = _bias_tiles(rel_bias)
    outs, lses = zip(*[_attention_group(g, *qkv[3 * g:3 * g + 3], bias)
                       for g in range(N_GROUPS)])
    x = _mix(outs, lses, x, mod1, attn_w_o[0].astype(BF16), ln[1, 0])
    x = _moe(x, mod1, rw, rbias, *experts(1), ln[1, 1])
    return x
```

```python
import functools
import math

import jax
import jax.numpy as jnp
from jax import lax
from jax.experimental import pallas as pl
from jax.experimental.pallas import tpu as pltpu

F32 = jnp.float32
BF16 = jnp.bfloat16

D = 1024
DEPTH = 2
CONV_WIDTH = 3
DIL_GROUPS = ((128, 1), (512, 4), (2048, 16))
N_GROUPS = len(DIL_GROUPS)
HEAD_DIM = 128
HEADS = D // HEAD_DIM
QW = N_GROUPS * D
BLOCK = 128
NUM_BUCKETS = 32
MAX_DISTANCE = 2048
N_EXPERTS = 16
EXPERTS_PER_GROUP = 4
N_EXPERT_GROUPS = N_EXPERTS // EXPERTS_PER_GROUP
EXPERT_FF = D // 4
ALPHA = (2 * DEPTH) ** 0.25
LN_EPS = 1e-5
NEG_INF = -1e30
LOG2E = math.log2(math.e)
LN2 = math.log(2.0)

LANES = 128
SUBLANES = 8
VMEM_LIMIT = 56 * 1024 * 1024
TOKEN_TILE = 512
MOE_WIN_BIG = 256
MOE_WIN_MID = 192
MOE_WIN_SMALL = 128
BF16_ROWS = 16
ATTN_QBLOCKS = 16
ADA_COL_TILE = 2048


def _cparams(n_axes):
    return pltpu.CompilerParams(
        dimension_semantics=("arbitrary",) * n_axes, vmem_limit_bytes=VMEM_LIMIT)


def _layer_norm(r, g, b):
    mu = jnp.mean(r, axis=-1, keepdims=True)
    c = r - mu
    var = jnp.mean(c * c, axis=-1, keepdims=True)
    return c * lax.rsqrt(var + LN_EPS) * g + b


def _bdot(a, b):
    return jnp.dot(a, b, preferred_element_type=F32)


def _ada_kernel(c_ref, w_ref, b_ref, o_ref):
    c = c_ref[...]
    cond = c * jax.nn.sigmoid(c)
    o_ref[...] = _bdot(cond.astype(BF16), w_ref[...].astype(BF16)) + b_ref[...]


def _ada_linear(c8, w, bias):
    n_layers, _, n = w.shape
    nt = ADA_COL_TILE
    return pl.pallas_call(
        _ada_kernel,
        grid=(n_layers, n // nt),
        in_specs=[pl.BlockSpec((SUBLANES, D), lambda l, j: (0, 0)),
                  pl.BlockSpec((None, D, nt), lambda l, j: (l, 0, j)),
                  pl.BlockSpec((None, 1, nt), lambda l, j: (l, 0, j))],
        out_specs=pl.BlockSpec((None, SUBLANES, nt), lambda l, j: (l, 0, j)),
        out_shape=jax.ShapeDtypeStruct((n_layers, SUBLANES, n), F32),
        compiler_params=_cparams(2),
        name="ada_linear",
    )(c8, w, bias.reshape(n_layers, 1, n))


def _mod_rows(mods, batch, n_vec):
    m = mods[:batch].reshape(batch, n_vec, D)
    return jnp.pad(m, ((0, 0), (0, SUBLANES - n_vec), (0, 0)))


def _conv_mixer_kernel(x_ref, mod_ref, w_in_ref, cw_ref, w_out_ref, ln_ref,
                       o_ref, u_scr):
    tm = x_ref.shape[0]
    i = pl.program_id(1)
    mod = mod_ref[...]
    sh1, sc1, g1 = mod[0:1], mod[1:2], mod[2:3]
    x = x_ref[...]
    h = (x * (1.0 + sc1) + sh1).astype(BF16)
    cgate = _bdot(h, w_in_ref[:, D:2 * D])
    v = _bdot(h, w_in_ref[:, 2 * D:3 * D])
    u = cgate * v

    @pl.when(i == 0)
    def _():
        u_scr[0:SUBLANES, :] = jnp.zeros((SUBLANES, D), F32)

    u_scr[SUBLANES:SUBLANES + tm, :] = u
    cw = cw_ref[...]
    conv = (cw[2:3] * u
            + cw[1:2] * u_scr[SUBLANES - 1:SUBLANES - 1 + tm, :]
            + cw[0:1] * u_scr[SUBLANES - 2:SUBLANES - 2 + tm, :])
    u_scr[0:SUBLANES, :] = u_scr[tm:tm + SUBLANES, :]
    bgate = _bdot(h, w_in_ref[:, 0:D])
    y = _bdot((bgate * conv).astype(BF16), w_out_ref[...])
    ln = ln_ref[...]
    o_ref[...] = _layer_norm(ALPHA * x + (1.0 + g1) * y, ln[0:1], ln[1:2])


def _conv_mixer(x, mod, w_in, conv_w, w_out, ln):
    batch, seq, _ = x.shape
    tm = TOKEN_TILE
    const = lambda b, i: (0, 0)
    return pl.pallas_call(
        _conv_mixer_kernel,
        grid=(batch, seq // tm),
        in_specs=[pl.BlockSpec((None, tm, D), lambda b, i: (b, i, 0)),
                  pl.BlockSpec((None, SUBLANES, D), lambda b, i: (b, 0, 0)),
                  pl.BlockSpec((D, 3 * D), const),
                  pl.BlockSpec((SUBLANES, D), const),
                  pl.BlockSpec((D, D), const),
                  pl.BlockSpec((2, D), const)],
        out_specs=pl.BlockSpec((None, tm, D), lambda b, i: (b, i, 0)),
        out_shape=jax.ShapeDtypeStruct(x.shape, F32),
        scratch_shapes=[pltpu.VMEM((tm + SUBLANES, D), F32)],
        compiler_params=_cparams(2),
        name="conv_mixer",
    )(x, mod, w_in, conv_w, w_out, ln)


def _route(logits_t, rbias_col):
    aff = jax.nn.sigmoid(logits_t)
    sel = aff + rbias_col
    rows = [sel[e:e + 1, :] for e in range(N_EXPERTS)]
    scores = []
    for g in range(N_EXPERT_GROUPS):
        s0, s1, s2, s3 = rows[EXPERTS_PER_GROUP * g:EXPERTS_PER_GROUP * (g + 1)]
        a, b = jnp.maximum(s0, s1), jnp.minimum(s0, s1)
        c, d = jnp.maximum(s2, s3), jnp.minimum(s2, s3)
        scores.append(jnp.maximum(a, c) + jnp.maximum(jnp.minimum(a, c), jnp.maximum(b, d)))
    best = scores[0]
    best_group = jnp.zeros(best.shape, jnp.int32)
    for g in range(1, N_EXPERT_GROUPS):
        upd = scores[g] > best
        best_group = jnp.where(upd, g, best_group)
        best = jnp.where(upd, scores[g], best)
    e_iota = lax.broadcasted_iota(jnp.int32, sel.shape, 0)
    masked = jnp.where((e_iota // EXPERTS_PER_GROUP) == best_group, sel, NEG_INF)
    m1 = jnp.max(masked, axis=0, keepdims=True)
    idx1 = jnp.min(jnp.where(masked == m1, e_iota, N_EXPERTS), axis=0, keepdims=True)
    masked2 = jnp.where(e_iota == idx1, -jnp.inf, masked)
    m2 = jnp.max(masked2, axis=0, keepdims=True)
    idx2 = jnp.min(jnp.where(masked2 == m2, e_iota, N_EXPERTS), axis=0, keepdims=True)
    a1 = jnp.sum(jnp.where(e_iota == idx1, aff, 0.0), axis=0, keepdims=True)
    a2 = jnp.sum(jnp.where(e_iota == idx2, aff, 0.0), axis=0, keepdims=True)
    den = a1 + a2
    comb = (jnp.where(e_iota == idx1, a1 / den, 0.0)
            + jnp.where(e_iota == idx2, a2 / den, 0.0))
    group1 = idx1 // EXPERTS_PER_GROUP
    split = group1 != idx2 // EXPERTS_PER_GROUP
    return comb, group1, split


def _moe_kernel(x_ref, xp_ref, mod_ref, modp_ref, rw_ref, rb_ref, tri_ref, wg_ref, wu_ref, wd_ref,
                ln_ref, o_ref, hs_scr, cs_scr, ys_scr, pt_scr):
    tm = x_ref.shape[0]
    gff = EXPERTS_PER_GROUP * EXPERT_FF
    t = pl.program_id(0)

    @pl.when(t == 0)
    def _():
        ys_scr[...] = jnp.zeros(ys_scr.shape, F32)
        pt_scr[...] = jnp.zeros(pt_scr.shape, BF16)

    y = _bdot(pt_scr[...], ys_scr[0:tm, :].astype(BF16))
    ln = ln_ref[...]
    o_ref[...] = _layer_norm(ALPHA * xp_ref[...] + (1.0 + modp_ref[...][5:6]) * y,
                             ln[0:1], ln[1:2])

    mod = mod_ref[...]
    sh2, sc2 = mod[3:4], mod[4:5]
    h = x_ref[...] * (1.0 + sc2) + sh2
    hb = h.astype(BF16)
    logits_t = lax.dot_general(rw_ref[...], hb, (((1,), (1,)), ((), ())),
                               preferred_element_type=F32)
    comb_t, group, split = _route(logits_t, rb_ref[...])

    g_iota = lax.broadcasted_iota(jnp.int32, (SUBLANES, tm), 0)
    member = jnp.where(g_iota == group, 1.0, 0.0)
    rank = _bdot(member.astype(BF16), tri_ref[...])
    counts = jnp.sum(member, axis=1, keepdims=True)
    cnt = [counts[g, 0] for g in range(N_EXPERT_GROUPS)]
    start = [0.0]
    for g in range(N_EXPERT_GROUPS - 1):
        start.append(start[g] + cnt[g])
    dest = sum(member[g:g + 1] * (start[g] + rank[g:g + 1]) for g in range(N_EXPERT_GROUPS))
    row = lax.broadcasted_iota(jnp.int32, (tm, tm), 0).astype(F32)
    col = lax.broadcasted_iota(jnp.int32, (tm, tm), 1).astype(F32)
    perm = jnp.where(row == dest, 1.0, 0.0).astype(BF16)
    dest_col = jnp.broadcast_to(dest, (LANES, tm)).T
    dest_col = jnp.concatenate([dest_col] * (tm // LANES), axis=1)
    perm_t = jnp.where(dest_col == col, 1.0, 0.0).astype(BF16)
    pt_scr[...] = perm_t

    hs_scr[0:tm, :] = _bdot(perm, hb).astype(BF16)
    hs_scr[tm:, :] = jnp.zeros((MOE_WIN_BIG, D), BF16)
    cs_scr[tm:, :] = jnp.zeros((MOE_WIN_BIG, LANES), F32)
    comb_hi = comb_t.astype(BF16)
    comb_lo = (comb_t - comb_hi.astype(F32)).astype(BF16)
    cs = _bdot(jnp.concatenate([comb_hi, comb_lo], axis=0), perm_t)
    cs = cs[0:N_EXPERTS] + cs[N_EXPERTS:2 * N_EXPERTS]
    cs_scr[0:tm, :] = jnp.concatenate(
        [cs, jnp.zeros((LANES - N_EXPERTS, tm), F32)], axis=0).T
    ys_scr[...] = jnp.zeros(ys_scr.shape, F32)

    any_split = jnp.max(jnp.where(split, 1.0, 0.0)) > 0.0
    active = t < pl.num_programs(0) - 1
    for g in range(N_EXPERT_GROUPS):
        first = jnp.where(any_split, 0, jnp.asarray(start[g], F32).astype(jnp.int32))
        last = jnp.where(any_split, tm, (start[g] + cnt[g]).astype(jnp.int32))
        first = (first // BF16_ROWS) * BF16_ROWS
        need = jnp.where(jnp.logical_and(active, last > first), last - first, 0)
        n_full = need // MOE_WIN_BIG
        rest = need - n_full * MOE_WIN_BIG
        n_big = n_full + jnp.where(rest > MOE_WIN_MID, 1, 0)
        n_mid = jnp.where(jnp.logical_and(rest > MOE_WIN_SMALL, rest <= MOE_WIN_MID), 1, 0)
        n_small = jnp.where(jnp.logical_and(rest > 0, rest <= MOE_WIN_SMALL), 1, 0)

        def window(start_row, size, g=g):
            rows = pl.ds(pl.multiple_of(start_row, BF16_ROWS), size)
            hs = hs_scr[rows, :]
            parts = []
            for j in range(EXPERTS_PER_GROUP):
                e = g * EXPERTS_PER_GROUP + j
                gate = _bdot(hs, wg_ref[e])
                up = _bdot(hs, wu_ref[e])
                he = gate * jax.nn.sigmoid(gate) * up * cs_scr[rows, e:e + 1]
                parts.append(he.astype(BF16))
            he = jnp.concatenate(parts, axis=1)
            ys_scr[rows, :] += _bdot(he, wd_ref[g * gff:(g + 1) * gff, :])

        def big(w, carry, first=first, window=window):
            window(first + w * MOE_WIN_BIG, MOE_WIN_BIG)
            return carry

        def mid(w, carry, first=first, n_full=n_full, window=window):
            window(first + n_full * MOE_WIN_BIG, MOE_WIN_MID)
            return carry

        def small(w, carry, first=first, n_full=n_full, window=window):
            window(first + n_full * MOE_WIN_BIG, MOE_WIN_SMALL)
            return carry

        lax.fori_loop(0, n_big, big, 0)
        lax.fori_loop(0, n_mid, mid, 0)
        lax.fori_loop(0, n_small, small, 0)


def _moe(layer, x, mod, rw, rbias, wg, wu, wd, ln):
    batch, seq, _ = x.shape
    tm = TOKEN_TILE
    layer_weights = lambda shape: pl.BlockSpec(
        (None,) + shape, lambda t: (layer,) + (0,) * len(shape), pipeline_mode=pl.Buffered(1))
    per_batch = seq // tm
    n_tiles = batch * per_batch
    const2 = lambda t: (0, 0)
    cur = lambda t: jnp.minimum(t, n_tiles - 1)
    prev = lambda t: jnp.maximum(t - 1, 0)
    tile = lambda which: pl.BlockSpec(
        (None, tm, D), lambda t: (which(t) // per_batch, which(t) % per_batch, 0))
    mods = lambda which: pl.BlockSpec(
        (None, SUBLANES, D), lambda t: (which(t) // per_batch, 0, 0))
    tri = (jnp.arange(tm)[:, None] < jnp.arange(tm)[None, :]).astype(BF16)
    return pl.pallas_call(
        _moe_kernel,
        grid=(n_tiles + 1,),
        in_specs=[tile(cur), tile(prev), mods(cur), mods(prev),
                  pl.BlockSpec((N_EXPERTS, D), const2),
                  pl.BlockSpec((N_EXPERTS, 1), const2),
                  pl.BlockSpec((tm, tm), const2),
                  layer_weights((N_EXPERTS, D, EXPERT_FF)),
                  layer_weights((N_EXPERTS, D, EXPERT_FF)),
                  layer_weights((N_EXPERTS * EXPERT_FF, D)),
                  pl.BlockSpec((2, D), const2)],
        out_specs=tile(prev),
        out_shape=jax.ShapeDtypeStruct(x.shape, F32),
        scratch_shapes=[pltpu.VMEM((tm + MOE_WIN_BIG, D), BF16),
                        pltpu.VMEM((tm + MOE_WIN_BIG, LANES), F32),
                        pltpu.VMEM((tm + MOE_WIN_BIG, D), F32),
                        pltpu.VMEM((tm, tm), BF16)],
        compiler_params=_cparams(1),
        name="moe",
    )(x, x, mod, mod, rw, rbias, tri, wg, wu, wd, ln)


def _qkv_kernel(x_ref, mod_ref, kvmod_ref, wq_ref, wkv_ref, *refs):
    outs, (slab, hq_scr, hkv_scr) = refs[:3 * N_GROUPS], refs[3 * N_GROUPS:]
    tm = x_ref.shape[0]
    mod = mod_ref[...]
    kvmod = kvmod_ref[...]
    q_sc, q_sh = 1.0 + mod[1:2], mod[0:1]
    kv_sc, kv_sh = 1.0 + kvmod[1:2], kvmod[0:1]
    n_slabs = D // LANES
    for j in range(n_slabs):
        slab[j] = x_ref[:, j * LANES:(j + 1) * LANES]
    scale = HEAD_DIM ** -0.5 * LOG2E
    for g, (_, dil) in enumerate(DIL_GROUPS):
        n = tm // dil
        for r in range(dil):
            rows = slice(r * n, (r + 1) * n)
            for j in range(n_slabs):
                ls = slice(j * LANES, (j + 1) * LANES)
                piece = slab[j] if dil == 1 else slab[j, pl.ds(r, n, stride=dil), :]
                hq_scr[rows, ls] = (piece * q_sc[:, ls] + q_sh[:, ls]).astype(BF16)
                hkv_scr[rows, ls] = (piece * kv_sc[:, ls] + kv_sh[:, ls]).astype(BF16)
        hq = hq_scr[...]
        hkv = hkv_scr[...]
        cols = slice(g * D, (g + 1) * D)
        vcols = slice(QW + g * D, QW + (g + 1) * D)
        q = (_bdot(hq, wq_ref[:, cols]) * scale).astype(BF16)
        k = _bdot(hkv, wkv_ref[:, cols]).astype(BF16)
        v = _bdot(hkv, wkv_ref[:, vcols]).astype(BF16)
        for r in range(dil):
            rows = slice(r * n, (r + 1) * n)
            outs[3 * g][r] = q[rows]
            outs[3 * g + 1][r] = k[rows]
            outs[3 * g + 2][r] = v[rows]


def _qkv(x, mod, kvmod, wq, wkv):
    batch, seq, _ = x.shape
    tm = TOKEN_TILE
    const = lambda b, i: (0, 0)
    out_specs, out_shape = [], []
    for _, dil in DIL_GROUPS:
        for _ in range(3):
            out_specs.append(pl.BlockSpec((None, dil, tm // dil, D), lambda b, i: (b, 0, i, 0)))
            out_shape.append(jax.ShapeDtypeStruct((batch, dil, seq // dil, D), BF16))
    return pl.pallas_call(
        _qkv_kernel,
        grid=(batch, seq // tm),
        in_specs=[pl.BlockSpec((None, tm, D), lambda b, i: (b, i, 0)),
                  pl.BlockSpec((None, SUBLANES, D), lambda b, i: (b, 0, 0)),
                  pl.BlockSpec((None, SUBLANES, D), lambda b, i: (b, 0, 0)),
                  pl.BlockSpec((D, QW), const),
                  pl.BlockSpec((D, 2 * QW), const)],
        out_specs=out_specs,
        out_shape=out_shape,
        scratch_shapes=[pltpu.VMEM((D // LANES, tm, LANES), F32),
                        pltpu.VMEM((tm, D), BF16),
                        pltpu.VMEM((tm, D), BF16)],
        compiler_params=_cparams(2),
        name="qkv_proj",
    )(x, mod, kvmod, wq, wkv)


def _bucket_tables():
    qi = jnp.arange(BLOCK)[:, None]
    kj = jnp.arange(2 * BLOCK)[None, :]
    dist = BLOCK + qi - kj
    max_exact = NUM_BUCKETS // 2
    tables = []
    for window, dil in DIL_GROUPS:
        n = jnp.maximum(dist, 0) * dil
        nf = jnp.maximum(n, 1).astype(F32)
        large = max_exact + (jnp.log(nf / max_exact) / math.log(MAX_DISTANCE / max_exact)
                             * (NUM_BUCKETS - max_exact)).astype(jnp.int32)
        large = jnp.minimum(large, NUM_BUCKETS - 1)
        bucket = jnp.where(n < max_exact, n, large)
        valid = (dist >= 0) & (dist <= window // dil)
        tables.append(jnp.where(valid, bucket, -1).astype(jnp.int32))
    return jnp.stack(tables)


def _bias_kernel(rb_ref, bkt_ref, o_ref):
    g = pl.program_id(0)
    bkt = bkt_ref[...]
    for h in range(HEADS):
        acc = jnp.full(bkt.shape, NEG_INF, F32)
        for b in range(NUM_BUCKETS):
            acc = jnp.where(bkt == b, rb_ref[b, g * HEADS + h] * LOG2E, acc)
        o_ref[h] = acc


def _bias_tiles(rel_bias):
    return pl.pallas_call(
        _bias_kernel,
        grid=(N_GROUPS,),
        in_specs=[pl.BlockSpec(memory_space=pltpu.SMEM),
                  pl.BlockSpec((None, BLOCK, 2 * BLOCK), lambda g: (g, 0, 0))],
        out_specs=pl.BlockSpec((None, HEADS, BLOCK, 2 * BLOCK), lambda g: (g, 0, 0, 0)),
        out_shape=jax.ShapeDtypeStruct((N_GROUPS, HEADS, BLOCK, 2 * BLOCK), F32),
        compiler_params=_cparams(1),
        name="rel_bias_tiles",
    )(rel_bias, _bucket_tables())


def _attn_kernel(q_ref, kp_ref, kc_ref, vp_ref, vc_ref, bias_ref, o_ref, lse_ref):
    n = pl.program_id(2)
    col = lax.broadcasted_iota(jnp.int32, (BLOCK, 2 * BLOCK), 1)
    no_prev = jnp.logical_and(n == 0, col < BLOCK)
    lane = lax.broadcasted_iota(jnp.int32, (BLOCK, LANES), 1)
    ones = jnp.ones((2 * BLOCK, HEAD_DIM), BF16)
    for r in range(q_ref.shape[0]):
        for j in range(q_ref.shape[1] // BLOCK):
            rows = slice(j * BLOCK, (j + 1) * BLOCK)
            window = slice((j - 1) * BLOCK, (j + 1) * BLOCK)
            m_tile = jnp.zeros((BLOCK, LANES), F32)
            l_tile = jnp.ones((BLOCK, LANES), F32)
            for h in range(HEADS):
                hs = slice(h * HEAD_DIM, (h + 1) * HEAD_DIM)
                if j == 0:
                    k = jnp.concatenate([kp_ref[r, :, hs], kc_ref[r, rows, hs]], axis=0)
                    v = jnp.concatenate([vp_ref[r, :, hs], vc_ref[r, rows, hs]], axis=0)
                else:
                    k = kc_ref[r, window, hs]
                    v = vc_ref[r, window, hs]
                s = lax.dot_general(q_ref[r, rows, hs], k, (((1,), (1,)), ((), ())),
                                    preferred_element_type=F32) + bias_ref[h]
                if j == 0:
                    s = jnp.where(no_prev, NEG_INF, s)
                m = jnp.max(s, axis=-1, keepdims=True)
                p = jnp.exp2(s - m)
                o_ext = _bdot(p.astype(BF16), jnp.concatenate([v, ones], axis=1))
                o, l = o_ext[:, :HEAD_DIM], o_ext[:, HEAD_DIM:]
                o_ref[r, rows, hs] = (o / l).astype(BF16)
                m_tile = jnp.where(lane == h, m, m_tile)
                l_tile = jnp.where(lane == h, l, l_tile)
            lse_ref[r, rows, :] = m_tile * LN2 + jnp.log(l_tile)


def _attention_group(g, q, k, v, bias):
    batch, dil, sub_len, _ = q.shape
    nq = min(ATTN_QBLOCKS, sub_len // BLOCK)
    n_sub = min(dil, ATTN_QBLOCKS // nq)
    assert sub_len % (nq * BLOCK) == 0 and dil % n_sub == 0
    cur = lambda b, r, n: (b, r, n, 0)
    if sub_len == nq * BLOCK:
        prev = lambda b, r, n: (0, 0, 0, 0)
    else:
        prev = lambda b, r, n: (b, r, jnp.maximum(n * nq - 1, 0), 0)
    blk = (None, n_sub, nq * BLOCK, D)
    blk_prev = (None, n_sub, BLOCK, D)
    return pl.pallas_call(
        _attn_kernel,
        grid=(batch, dil // n_sub, sub_len // (nq * BLOCK)),
        in_specs=[
            pl.BlockSpec(blk, cur),
            pl.BlockSpec(blk_prev, prev),
            pl.BlockSpec(blk, cur),
            pl.BlockSpec(blk_prev, prev),
            pl.BlockSpec(blk, cur),
            pl.BlockSpec((None, HEADS, BLOCK, 2 * BLOCK), lambda b, r, n: (g, 0, 0, 0)),
        ],
        out_specs=[pl.BlockSpec(blk, cur),
                   pl.BlockSpec((None, n_sub, nq * BLOCK, LANES), cur)],
        out_shape=[jax.ShapeDtypeStruct((batch, dil, sub_len, D), BF16),
                   jax.ShapeDtypeStruct((batch, dil, sub_len, LANES), F32)],
        compiler_params=_cparams(3),
        name=f"dilated_attn_g{g}",
    )(q, k, k, v, v, bias)


def _mix_kernel(o0_ref, o1_ref, o2_ref, l0_ref, l1_ref, l2_ref, x_ref, mod_ref,
                expand_ref, wo_ref, ln_ref, out_ref, lse_scr, o_slab):
    tm = x_ref.shape[0]
    n_slabs = D // LANES
    lses = []
    for g, l_ref in enumerate((l0_ref, l1_ref, l2_ref)):
        dil = DIL_GROUPS[g][1]
        if dil == 1:
            lses.append(l_ref[0])
            continue
        for r in range(dil):
            lse_scr[g, pl.ds(r, tm // dil, stride=dil), :] = l_ref[r]
        lses.append(lse_scr[g])
    m = jnp.maximum(lses[0], jnp.maximum(lses[1], lses[2]))
    es = [jnp.exp(a - m) for a in lses]
    den = es[0] + es[1] + es[2]
    expand = expand_ref[...]
    mixed = None
    for g, o_ref in enumerate((o0_ref, o1_ref, o2_ref)):
        dil = DIL_GROUPS[g][1]
        w = _bdot((es[g] / den).astype(BF16), expand)
        if dil == 1:
            o = o_ref[0].astype(F32)
        else:
            for r in range(dil):
                for j in range(n_slabs):
                    o_slab[j, pl.ds(r, tm // dil, stride=dil), :] = (
                        o_ref[r, :, j * LANES:(j + 1) * LANES].astype(F32))
            o = jnp.concatenate([o_slab[j] for j in range(n_slabs)], axis=1)
        mixed = w * o if mixed is None else mixed + w * o
    y = _bdot(mixed.astype(BF16), wo_ref[...])
    g1 = mod_ref[...][2:3]
    ln = ln_ref[...]
    out_ref[...] = _layer_norm(ALPHA * x_ref[...] + (1.0 + g1) * y, ln[0:1], ln[1:2])


def _mix(outs, lses, x, mod, wo, ln):
    batch, seq, _ = x.shape
    tm = TOKEN_TILE
    const = lambda b, i: (0, 0)
    tok = pl.BlockSpec((None, tm, D), lambda b, i: (b, i, 0))
    dilated = lambda dil, w: pl.BlockSpec((None, dil, tm // dil, w), lambda b, i: (b, 0, i, 0))
    dils = [dil for _, dil in DIL_GROUPS]
    head_of_lane = jnp.arange(D)[None, :] // HEAD_DIM
    expand = (jnp.arange(LANES)[:, None] == head_of_lane).astype(BF16)
    return pl.pallas_call(
        _mix_kernel,
        grid=(batch, seq // tm),
        in_specs=[dilated(d, D) for d in dils] + [dilated(d, LANES) for d in dils] + [
            tok,
            pl.BlockSpec((None, SUBLANES, D), lambda b, i: (b, 0, 0)),
            pl.BlockSpec((LANES, D), const),
            pl.BlockSpec((D, D), const),
            pl.BlockSpec((2, D), const)],
        out_specs=tok,
        out_shape=jax.ShapeDtypeStruct(x.shape, F32),
        scratch_shapes=[pltpu.VMEM((N_GROUPS, tm, LANES), F32),
                        pltpu.VMEM((D // LANES, tm, LANES), F32)],
        compiler_params=_cparams(2),
        name="attn_mix",
    )(*outs, *lses, x, mod, expand, wo, ln)


def kernel(x, c, ada_w, ada_b, ln_g, ln_b, conv_w_in, conv_w, conv_w_out, kv_ada_w,
           kv_ada_b, w_kv, attn_w_q, attn_w_o, rel_bias, router_w, router_bias,
           moe_w_gate, moe_w_up, moe_w_down):
    batch, seq, _ = x.shape
    assert x.shape[2] == D and batch <= SUBLANES
    assert seq % TOKEN_TILE == 0 and seq % (BLOCK * DIL_GROUPS[-1][1]) == 0

    c8 = jnp.pad(c, ((0, SUBLANES - batch), (0, 0)))
    mods = _ada_linear(c8, ada_w, ada_b)
    kvmods = _ada_linear(c8, kv_ada_w[None], kv_ada_b[None])[0]
    mod0 = _mod_rows(mods[0], batch, 6)
    mod1 = _mod_rows(mods[1], batch, 6)
    kvmod = _mod_rows(kvmods, batch, 2)
    ln = jnp.stack([ln_g, ln_b], axis=2)

    rw = router_w.T.astype(BF16)
    rbias = router_bias.reshape(N_EXPERTS, 1)
    experts = (moe_w_gate.astype(BF16), moe_w_up.astype(BF16),
               moe_w_down.astype(BF16).reshape(DEPTH, N_EXPERTS * EXPERT_FF, D))
    conv_w8 = jnp.pad(conv_w[0], ((0, SUBLANES - CONV_WIDTH), (0, 0)))

    x = _conv_mixer(x, mod0, conv_w_in[0].astype(BF16), conv_w8,
                    conv_w_out[0].astype(BF16), ln[0, 0])
    x = _moe(0, x, mod0, rw, rbias, *experts, ln[0, 1])

    qkv = _qkv(x, mod1, kvmod, attn_w_q[0].astype(BF16), w_kv.astype(BF16))
    bias = _bias_tiles(rel_bias)
    outs, lses = zip(*[_attention_group(g, *qkv[3 * g:3 * g + 3], bias)
                       for g in range(N_GROUPS)])
    x = _mix(outs, lses, x, mod1, attn_w_o[0].astype(BF16), ln[1, 0])
    x = _moe(1, x, mod1, rw, rbias, *experts, ln[1, 1])
    return x
```

```python
import functools
import math

import jax
import jax.numpy as jnp
from jax import lax
from jax.experimental import pallas as pl
from jax.experimental.pallas import tpu as pltpu

F32 = jnp.float32
BF16 = jnp.bfloat16

D = 1024
DEPTH = 2
CONV_WIDTH = 3
DIL_GROUPS = ((128, 1), (512, 4), (2048, 16))
N_GROUPS = len(DIL_GROUPS)
HEAD_DIM = 128
HEADS = D // HEAD_DIM
QW = N_GROUPS * D
BLOCK = 128
NUM_BUCKETS = 32
MAX_DISTANCE = 2048
N_EXPERTS = 16
EXPERTS_PER_GROUP = 4
N_EXPERT_GROUPS = N_EXPERTS // EXPERTS_PER_GROUP
EXPERT_FF = D // 4
ALPHA = (2 * DEPTH) ** 0.25
LN_EPS = 1e-5
NEG_INF = -1e30
LOG2E = math.log2(math.e)
LN2 = math.log(2.0)

LANES = 128
SUBLANES = 8
VMEM_LIMIT = 56 * 1024 * 1024
TOKEN_TILE = 512
WIDE_TOKEN_TILE = 1024
MOE_WIN_BIG = 256
MOE_WIN_MID = 192
MOE_WIN_SMALL = 128
BF16_ROWS = 16
ATTN_QBLOCKS = 16
ADA_COL_TILE = 2048


def _cparams(n_axes):
    return pltpu.CompilerParams(
        dimension_semantics=("arbitrary",) * n_axes, vmem_limit_bytes=VMEM_LIMIT)


def _layer_norm(r, g, b):
    mu = jnp.mean(r, axis=-1, keepdims=True)
    c = r - mu
    var = jnp.mean(c * c, axis=-1, keepdims=True)
    return c * lax.rsqrt(var + LN_EPS) * g + b


def _bdot(a, b):
    return jnp.dot(a, b, preferred_element_type=F32)


def _ada_kernel(c_ref, w_ref, b_ref, o_ref):
    c = c_ref[...]
    cond = c * jax.nn.sigmoid(c)
    o_ref[...] = _bdot(cond.astype(BF16), w_ref[...].astype(BF16)) + b_ref[...]


def _ada_linear(c8, w, bias):
    n_layers, _, n = w.shape
    nt = ADA_COL_TILE
    return pl.pallas_call(
        _ada_kernel,
        grid=(n_layers, n // nt),
        in_specs=[pl.BlockSpec((SUBLANES, D), lambda l, j: (0, 0)),
                  pl.BlockSpec((None, D, nt), lambda l, j: (l, 0, j)),
                  pl.BlockSpec((None, 1, nt), lambda l, j: (l, 0, j))],
        out_specs=pl.BlockSpec((None, SUBLANES, nt), lambda l, j: (l, 0, j)),
        out_shape=jax.ShapeDtypeStruct((n_layers, SUBLANES, n), F32),
        compiler_params=_cparams(2),
        name="ada_linear",
    )(c8, w, bias.reshape(n_layers, 1, n))


def _mod_rows(mods, batch, n_vec):
    m = mods[:batch].reshape(batch, n_vec, D)
    return jnp.pad(m, ((0, 0), (0, SUBLANES - n_vec), (0, 0)))


def _conv_mixer_kernel(x_ref, mod_ref, w_in_ref, cw_ref, w_out_ref, ln_ref,
                       o_ref, u_scr):
    tm = x_ref.shape[0]
    i = pl.program_id(1)
    mod = mod_ref[...]
    sh1, sc1, g1 = mod[0:1], mod[1:2], mod[2:3]
    x = x_ref[...]
    h = (x * (1.0 + sc1) + sh1).astype(BF16)
    cgate = _bdot(h, w_in_ref[:, D:2 * D])
    v = _bdot(h, w_in_ref[:, 2 * D:3 * D])
    u = cgate * v

    @pl.when(i == 0)
    def _():
        u_scr[0:SUBLANES, :] = jnp.zeros((SUBLANES, D), F32)

    u_scr[SUBLANES:SUBLANES + tm, :] = u
    cw = cw_ref[...]
    conv = (cw[2:3] * u
            + cw[1:2] * u_scr[SUBLANES - 1:SUBLANES - 1 + tm, :]
            + cw[0:1] * u_scr[SUBLANES - 2:SUBLANES - 2 + tm, :])
    u_scr[0:SUBLANES, :] = u_scr[tm:tm + SUBLANES, :]
    bgate = _bdot(h, w_in_ref[:, 0:D])
    y = _bdot((bgate * conv).astype(BF16), w_out_ref[...])
    ln = ln_ref[...]
    o_ref[...] = _layer_norm(ALPHA * x + (1.0 + g1) * y, ln[0:1], ln[1:2])


def _conv_mixer(x, mod, w_in, conv_w, w_out, ln):
    batch, seq, _ = x.shape
    tm = WIDE_TOKEN_TILE
    const = lambda b, i: (0, 0)
    return pl.pallas_call(
        _conv_mixer_kernel,
        grid=(batch, seq // tm),
        in_specs=[pl.BlockSpec((None, tm, D), lambda b, i: (b, i, 0)),
                  pl.BlockSpec((None, SUBLANES, D), lambda b, i: (b, 0, 0)),
                  pl.BlockSpec((D, 3 * D), const),
                  pl.BlockSpec((SUBLANES, D), const),
                  pl.BlockSpec((D, D), const),
                  pl.BlockSpec((2, D), const)],
        out_specs=pl.BlockSpec((None, tm, D), lambda b, i: (b, i, 0)),
        out_shape=jax.ShapeDtypeStruct(x.shape, F32),
        scratch_shapes=[pltpu.VMEM((tm + SUBLANES, D), F32)],
        compiler_params=_cparams(2),
        name="conv_mixer",
    )(x, mod, w_in, conv_w, w_out, ln)


def _route(logits_t, rbias_col):
    aff = jax.nn.sigmoid(logits_t)
    sel = aff + rbias_col
    rows = [sel[e:e + 1, :] for e in range(N_EXPERTS)]
    scores = []
    for g in range(N_EXPERT_GROUPS):
        s0, s1, s2, s3 = rows[EXPERTS_PER_GROUP * g:EXPERTS_PER_GROUP * (g + 1)]
        a, b = jnp.maximum(s0, s1), jnp.minimum(s0, s1)
        c, d = jnp.maximum(s2, s3), jnp.minimum(s2, s3)
        scores.append(jnp.maximum(a, c) + jnp.maximum(jnp.minimum(a, c), jnp.maximum(b, d)))
    best = scores[0]
    best_group = jnp.zeros(best.shape, jnp.int32)
    for g in range(1, N_EXPERT_GROUPS):
        upd = scores[g] > best
        best_group = jnp.where(upd, g, best_group)
        best = jnp.where(upd, scores[g], best)
    e_iota = lax.broadcasted_iota(jnp.int32, sel.shape, 0)
    masked = jnp.where((e_iota // EXPERTS_PER_GROUP) == best_group, sel, NEG_INF)
    m1 = jnp.max(masked, axis=0, keepdims=True)
    idx1 = jnp.min(jnp.where(masked == m1, e_iota, N_EXPERTS), axis=0, keepdims=True)
    masked2 = jnp.where(e_iota == idx1, -jnp.inf, masked)
    m2 = jnp.max(masked2, axis=0, keepdims=True)
    idx2 = jnp.min(jnp.where(masked2 == m2, e_iota, N_EXPERTS), axis=0, keepdims=True)
    a1 = jnp.sum(jnp.where(e_iota == idx1, aff, 0.0), axis=0, keepdims=True)
    a2 = jnp.sum(jnp.where(e_iota == idx2, aff, 0.0), axis=0, keepdims=True)
    den = a1 + a2
    comb = (jnp.where(e_iota == idx1, a1 / den, 0.0)
            + jnp.where(e_iota == idx2, a2 / den, 0.0))
    group1 = idx1 // EXPERTS_PER_GROUP
    split = group1 != idx2 // EXPERTS_PER_GROUP
    return comb, group1, split


def _moe_kernel(x_ref, xp_ref, mod_ref, modp_ref, rw_ref, rb_ref, tri_ref, wg_ref, wu_ref, wd_ref,
                ln_ref, o_ref, hs_scr, cs_scr, ys_scr, pt_scr):
    tm = x_ref.shape[0]
    gff = EXPERTS_PER_GROUP * EXPERT_FF
    t = pl.program_id(0)

    @pl.when(t == 0)
    def _():
        ys_scr[...] = jnp.zeros(ys_scr.shape, F32)
        pt_scr[...] = jnp.zeros(pt_scr.shape, BF16)

    y = _bdot(pt_scr[...], ys_scr[0:tm, :].astype(BF16))
    ln = ln_ref[...]
    o_ref[...] = _layer_norm(ALPHA * xp_ref[...] + (1.0 + modp_ref[...][5:6]) * y,
                             ln[0:1], ln[1:2])

    mod = mod_ref[...]
    sh2, sc2 = mod[3:4], mod[4:5]
    h = x_ref[...] * (1.0 + sc2) + sh2
    hb = h.astype(BF16)
    logits_t = lax.dot_general(rw_ref[...], hb, (((1,), (1,)), ((), ())),
                               preferred_element_type=F32)
    comb_t, group, split = _route(logits_t, rb_ref[...])

    g_iota = lax.broadcasted_iota(jnp.int32, (SUBLANES, tm), 0)
    member = jnp.where(g_iota == group, 1.0, 0.0)
    rank = _bdot(member.astype(BF16), tri_ref[...])
    counts = jnp.sum(member, axis=1, keepdims=True)
    cnt = [counts[g, 0] for g in range(N_EXPERT_GROUPS)]
    start = [0.0]
    for g in range(N_EXPERT_GROUPS - 1):
        start.append(start[g] + cnt[g])
    dest = sum(member[g:g + 1] * (start[g] + rank[g:g + 1]) for g in range(N_EXPERT_GROUPS))
    row = lax.broadcasted_iota(jnp.int32, (tm, tm), 0).astype(F32)
    col = lax.broadcasted_iota(jnp.int32, (tm, tm), 1).astype(F32)
    perm = jnp.where(row == dest, 1.0, 0.0).astype(BF16)
    dest_col = jnp.broadcast_to(dest, (LANES, tm)).T
    dest_col = jnp.concatenate([dest_col] * (tm // LANES), axis=1)
    perm_t = jnp.where(dest_col == col, 1.0, 0.0).astype(BF16)
    pt_scr[...] = perm_t

    hs_scr[0:tm, :] = _bdot(perm, hb).astype(BF16)
    hs_scr[tm:, :] = jnp.zeros((MOE_WIN_BIG, D), BF16)
    cs_scr[tm:, :] = jnp.zeros((MOE_WIN_BIG, LANES), F32)
    comb_hi = comb_t.astype(BF16)
    comb_lo = (comb_t - comb_hi.astype(F32)).astype(BF16)
    cs = _bdot(jnp.concatenate([comb_hi, comb_lo], axis=0), perm_t)
    cs = cs[0:N_EXPERTS] + cs[N_EXPERTS:2 * N_EXPERTS]
    cs_scr[0:tm, :] = jnp.concatenate(
        [cs, jnp.zeros((LANES - N_EXPERTS, tm), F32)], axis=0).T
    ys_scr[...] = jnp.zeros(ys_scr.shape, F32)

    any_split = jnp.max(jnp.where(split, 1.0, 0.0)) > 0.0
    active = t < pl.num_programs(0) - 1
    for g in range(N_EXPERT_GROUPS):
        first = jnp.where(any_split, 0, jnp.asarray(start[g], F32).astype(jnp.int32))
        last = jnp.where(any_split, tm, (start[g] + cnt[g]).astype(jnp.int32))
        first = (first // BF16_ROWS) * BF16_ROWS
        need = jnp.where(jnp.logical_and(active, last > first), last - first, 0)
        n_full = need // MOE_WIN_BIG
        rest = need - n_full * MOE_WIN_BIG
        n_big = n_full + jnp.where(rest > MOE_WIN_MID, 1, 0)
        n_mid = jnp.where(jnp.logical_and(rest > MOE_WIN_SMALL, rest <= MOE_WIN_MID), 1, 0)
        n_small = jnp.where(jnp.logical_and(rest > 0, rest <= MOE_WIN_SMALL), 1, 0)

        def window(start_row, size, g=g):
            rows = pl.ds(pl.multiple_of(start_row, BF16_ROWS), size)
            hs = hs_scr[rows, :]
            parts = []
            for j in range(EXPERTS_PER_GROUP):
                e = g * EXPERTS_PER_GROUP + j
                gate = _bdot(hs, wg_ref[e])
                up = _bdot(hs, wu_ref[e])
                he = gate * jax.nn.sigmoid(gate) * up * cs_scr[rows, e:e + 1]
                parts.append(he.astype(BF16))
            he = jnp.concatenate(parts, axis=1)
            ys_scr[rows, :] += _bdot(he, wd_ref[g * gff:(g + 1) * gff, :])

        def big(w, carry, first=first, window=window):
            window(first + w * MOE_WIN_BIG, MOE_WIN_BIG)
            return carry

        def mid(w, carry, first=first, n_full=n_full, window=window):
            window(first + n_full * MOE_WIN_BIG, MOE_WIN_MID)
            return carry

        def small(w, carry, first=first, n_full=n_full, window=window):
            window(first + n_full * MOE_WIN_BIG, MOE_WIN_SMALL)
            return carry

        lax.fori_loop(0, n_big, big, 0)
        lax.fori_loop(0, n_mid, mid, 0)
        lax.fori_loop(0, n_small, small, 0)


def _moe(layer, x, mod, rw, rbias, wg, wu, wd, ln):
    batch, seq, _ = x.shape
    tm = TOKEN_TILE
    layer_weights = lambda shape: pl.BlockSpec(
        (None,) + shape, lambda t: (layer,) + (0,) * len(shape), pipeline_mode=pl.Buffered(1))
    per_batch = seq // tm
    n_tiles = batch * per_batch
    const2 = lambda t: (0, 0)
    cur = lambda t: jnp.minimum(t, n_tiles - 1)
    prev = lambda t: jnp.maximum(t - 1, 0)
    tile = lambda which: pl.BlockSpec(
        (None, tm, D), lambda t: (which(t) // per_batch, which(t) % per_batch, 0))
    mods = lambda which: pl.BlockSpec(
        (None, SUBLANES, D), lambda t: (which(t) // per_batch, 0, 0))
    tri = (jnp.arange(tm)[:, None] < jnp.arange(tm)[None, :]).astype(BF16)
    return pl.pallas_call(
        _moe_kernel,
        grid=(n_tiles + 1,),
        in_specs=[tile(cur), tile(prev), mods(cur), mods(prev),
                  pl.BlockSpec((N_EXPERTS, D), const2),
                  pl.BlockSpec((N_EXPERTS, 1), const2),
                  pl.BlockSpec((tm, tm), const2),
                  layer_weights((N_EXPERTS, D, EXPERT_FF)),
                  layer_weights((N_EXPERTS, D, EXPERT_FF)),
                  layer_weights((N_EXPERTS * EXPERT_FF, D)),
                  pl.BlockSpec((2, D), const2)],
        out_specs=tile(prev),
        out_shape=jax.ShapeDtypeStruct(x.shape, F32),
        scratch_shapes=[pltpu.VMEM((tm + MOE_WIN_BIG, D), BF16),
                        pltpu.VMEM((tm + MOE_WIN_BIG, LANES), F32),
                        pltpu.VMEM((tm + MOE_WIN_BIG, D), F32),
                        pltpu.VMEM((tm, tm), BF16)],
        compiler_params=_cparams(1),
        name="moe",
    )(x, x, mod, mod, rw, rbias, tri, wg, wu, wd, ln)


def _qkv_kernel(x_ref, mod_ref, kvmod_ref, wq_ref, wkv_ref, *refs):
    outs, (slab, hq_scr, hkv_scr) = refs[:3 * N_GROUPS], refs[3 * N_GROUPS:]
    tm = x_ref.shape[0]
    mod = mod_ref[...]
    kvmod = kvmod_ref[...]
    q_sc, q_sh = 1.0 + mod[1:2], mod[0:1]
    kv_sc, kv_sh = 1.0 + kvmod[1:2], kvmod[0:1]
    n_slabs = D // LANES
    for j in range(n_slabs):
        slab[j] = x_ref[:, j * LANES:(j + 1) * LANES]
    scale = HEAD_DIM ** -0.5 * LOG2E
    for g, (_, dil) in enumerate(DIL_GROUPS):
        n = tm // dil
        for r in range(dil):
            rows = slice(r * n, (r + 1) * n)
            for j in range(n_slabs):
                ls = slice(j * LANES, (j + 1) * LANES)
                piece = slab[j] if dil == 1 else slab[j, pl.ds(r, n, stride=dil), :]
                hq_scr[rows, ls] = (piece * q_sc[:, ls] + q_sh[:, ls]).astype(BF16)
                hkv_scr[rows, ls] = (piece * kv_sc[:, ls] + kv_sh[:, ls]).astype(BF16)
        hq = hq_scr[...]
        hkv = hkv_scr[...]
        cols = slice(g * D, (g + 1) * D)
        vcols = slice(QW + g * D, QW + (g + 1) * D)
        q = (_bdot(hq, wq_ref[:, cols]) * scale).astype(BF16)
        k = _bdot(hkv, wkv_ref[:, cols]).astype(BF16)
        v = _bdot(hkv, wkv_ref[:, vcols]).astype(BF16)
        for r in range(dil):
            rows = slice(r * n, (r + 1) * n)
            outs[3 * g][r] = q[rows]
            outs[3 * g + 1][r] = k[rows]
            outs[3 * g + 2][r] = v[rows]


def _qkv(x, mod, kvmod, wq, wkv):
    batch, seq, _ = x.shape
    tm = TOKEN_TILE
    const = lambda b, i: (0, 0)
    out_specs, out_shape = [], []
    for _, dil in DIL_GROUPS:
        for _ in range(3):
            out_specs.append(pl.BlockSpec((None, dil, tm // dil, D), lambda b, i: (b, 0, i, 0)))
            out_shape.append(jax.ShapeDtypeStruct((batch, dil, seq // dil, D), BF16))
    return pl.pallas_call(
        _qkv_kernel,
        grid=(batch, seq // tm),
        in_specs=[pl.BlockSpec((None, tm, D), lambda b, i: (b, i, 0)),
                  pl.BlockSpec((None, SUBLANES, D), lambda b, i: (b, 0, 0)),
                  pl.BlockSpec((None, SUBLANES, D), lambda b, i: (b, 0, 0)),
                  pl.BlockSpec((D, QW), const),
                  pl.BlockSpec((D, 2 * QW), const)],
        out_specs=out_specs,
        out_shape=out_shape,
        scratch_shapes=[pltpu.VMEM((D // LANES, tm, LANES), F32),
                        pltpu.VMEM((tm, D), BF16),
                        pltpu.VMEM((tm, D), BF16)],
        compiler_params=_cparams(2),
        name="qkv_proj",
    )(x, mod, kvmod, wq, wkv)


def _bucket_tables():
    qi = jnp.arange(BLOCK)[:, None]
    kj = jnp.arange(2 * BLOCK)[None, :]
    dist = BLOCK + qi - kj
    max_exact = NUM_BUCKETS // 2
    tables = []
    for window, dil in DIL_GROUPS:
        n = jnp.maximum(dist, 0) * dil
        nf = jnp.maximum(n, 1).astype(F32)
        large = max_exact + (jnp.log(nf / max_exact) / math.log(MAX_DISTANCE / max_exact)
                             * (NUM_BUCKETS - max_exact)).astype(jnp.int32)
        large = jnp.minimum(large, NUM_BUCKETS - 1)
        bucket = jnp.where(n < max_exact, n, large)
        valid = (dist >= 0) & (dist <= window // dil)
        tables.append(jnp.where(valid, bucket, -1).astype(jnp.int32))
    return jnp.stack(tables)


def _bias_kernel(rb_ref, bkt_ref, o_ref):
    g = pl.program_id(0)
    bkt = bkt_ref[...]
    for h in range(HEADS):
        acc = jnp.full(bkt.shape, NEG_INF, F32)
        for b in range(NUM_BUCKETS):
            acc = jnp.where(bkt == b, rb_ref[b, g * HEADS + h] * LOG2E, acc)
        o_ref[h] = acc


def _bias_tiles(rel_bias):
    return pl.pallas_call(
        _bias_kernel,
        grid=(N_GROUPS,),
        in_specs=[pl.BlockSpec(memory_space=pltpu.SMEM),
                  pl.BlockSpec((None, BLOCK, 2 * BLOCK), lambda g: (g, 0, 0))],
        out_specs=pl.BlockSpec((None, HEADS, BLOCK, 2 * BLOCK), lambda g: (g, 0, 0, 0)),
        out_shape=jax.ShapeDtypeStruct((N_GROUPS, HEADS, BLOCK, 2 * BLOCK), F32),
        compiler_params=_cparams(1),
        name="rel_bias_tiles",
    )(rel_bias, _bucket_tables())


def _attn_kernel(q_ref, kp_ref, kc_ref, vp_ref, vc_ref, bias_ref, o_ref, lse_ref):
    n = pl.program_id(2)
    col = lax.broadcasted_iota(jnp.int32, (BLOCK, 2 * BLOCK), 1)
    no_prev = jnp.logical_and(n == 0, col < BLOCK)
    lane = lax.broadcasted_iota(jnp.int32, (BLOCK, LANES), 1)
    ones = jnp.ones((2 * BLOCK, HEAD_DIM), BF16)
    for r in range(q_ref.shape[0]):
        for j in range(q_ref.shape[1] // BLOCK):
            rows = slice(j * BLOCK, (j + 1) * BLOCK)
            window = slice((j - 1) * BLOCK, (j + 1) * BLOCK)
            m_tile = jnp.zeros((BLOCK, LANES), F32)
            l_tile = jnp.ones((BLOCK, LANES), F32)
            for h in range(HEADS):
                hs = slice(h * HEAD_DIM, (h + 1) * HEAD_DIM)
                if j == 0:
                    k = jnp.concatenate([kp_ref[r, :, hs], kc_ref[r, rows, hs]], axis=0)
                    v = jnp.concatenate([vp_ref[r, :, hs], vc_ref[r, rows, hs]], axis=0)
                else:
                    k = kc_ref[r, window, hs]
                    v = vc_ref[r, window, hs]
                s = lax.dot_general(q_ref[r, rows, hs], k, (((1,), (1,)), ((), ())),
                                    preferred_element_type=F32) + bias_ref[h]
                if j == 0:
                    s = jnp.where(no_prev, NEG_INF, s)
                m = jnp.max(s, axis=-1, keepdims=True)
                p = jnp.exp2(s - m)
                o_ext = _bdot(p.astype(BF16), jnp.concatenate([v, ones], axis=1))
                o, l = o_ext[:, :HEAD_DIM], o_ext[:, HEAD_DIM:]
                o_ref[r, rows, hs] = (o / l).astype(BF16)
                m_tile = jnp.where(lane == h, m, m_tile)
                l_tile = jnp.where(lane == h, l, l_tile)
            lse_ref[r, rows, :] = m_tile * LN2 + jnp.log(l_tile)


def _attention_group(g, q, k, v, bias):
    batch, dil, sub_len, _ = q.shape
    nq = min(ATTN_QBLOCKS, sub_len // BLOCK)
    n_sub = min(dil, ATTN_QBLOCKS // nq)
    assert sub_len % (nq * BLOCK) == 0 and dil % n_sub == 0
    cur = lambda b, r, n: (b, r, n, 0)
    if sub_len == nq * BLOCK:
        prev = lambda b, r, n: (0, 0, 0, 0)
    else:
        prev = lambda b, r, n: (b, r, jnp.maximum(n * nq - 1, 0), 0)
    blk = (None, n_sub, nq * BLOCK, D)
    blk_prev = (None, n_sub, BLOCK, D)
    return pl.pallas_call(
        _attn_kernel,
        grid=(batch, dil // n_sub, sub_len // (nq * BLOCK)),
        in_specs=[
            pl.BlockSpec(blk, cur),
            pl.BlockSpec(blk_prev, prev),
            pl.BlockSpec(blk, cur),
            pl.BlockSpec(blk_prev, prev),
            pl.BlockSpec(blk, cur),
            pl.BlockSpec((None, HEADS, BLOCK, 2 * BLOCK), lambda b, r, n: (g, 0, 0, 0)),
        ],
        out_specs=[pl.BlockSpec(blk, cur),
                   pl.BlockSpec((None, n_sub, nq * BLOCK, LANES), cur)],
        out_shape=[jax.ShapeDtypeStruct((batch, dil, sub_len, D), BF16),
                   jax.ShapeDtypeStruct((batch, dil, sub_len, LANES), F32)],
        compiler_params=_cparams(3),
        name=f"dilated_attn_g{g}",
    )(q, k, k, v, v, bias)


def _mix_kernel(o0_ref, o1_ref, o2_ref, l0_ref, l1_ref, l2_ref, x_ref, mod_ref,
                expand_ref, wo_ref, ln_ref, out_ref, lse_scr, o_slab):
    tm = x_ref.shape[0]
    n_slabs = D // LANES
    lses = []
    for g, l_ref in enumerate((l0_ref, l1_ref, l2_ref)):
        dil = DIL_GROUPS[g][1]
        if dil == 1:
            lses.append(l_ref[0])
            continue
        for r in range(dil):
            lse_scr[g, pl.ds(r, tm // dil, stride=dil), :] = l_ref[r]
        lses.append(lse_scr[g])
    m = jnp.maximum(lses[0], jnp.maximum(lses[1], lses[2]))
    es = [jnp.exp(a - m) for a in lses]
    den = es[0] + es[1] + es[2]
    expand = expand_ref[...]
    mixed = None
    for g, o_ref in enumerate((o0_ref, o1_ref, o2_ref)):
        dil = DIL_GROUPS[g][1]
        w = _bdot((es[g] / den).astype(BF16), expand)
        if dil == 1:
            o = o_ref[0].astype(F32)
        else:
            for r in range(dil):
                for j in range(n_slabs):
                    o_slab[j, pl.ds(r, tm // dil, stride=dil), :] = (
                        o_ref[r, :, j * LANES:(j + 1) * LANES].astype(F32))
            o = jnp.concatenate([o_slab[j] for j in range(n_slabs)], axis=1)
        mixed = w * o if mixed is None else mixed + w * o
    y = _bdot(mixed.astype(BF16), wo_ref[...])
    g1 = mod_ref[...][2:3]
    ln = ln_ref[...]
    out_ref[...] = _layer_norm(ALPHA * x_ref[...] + (1.0 + g1) * y, ln[0:1], ln[1:2])


def _mix(outs, lses, x, mod, wo, ln):
    batch, seq, _ = x.shape
    tm = WIDE_TOKEN_TILE
    const = lambda b, i: (0, 0)
    tok = pl.BlockSpec((None, tm, D), lambda b, i: (b, i, 0))
    dilated = lambda dil, w: pl.BlockSpec((None, dil, tm // dil, w), lambda b, i: (b, 0, i, 0))
    dils = [dil for _, dil in DIL_GROUPS]
    head_of_lane = jnp.arange(D)[None, :] // HEAD_DIM
    expand = (jnp.arange(LANES)[:, None] == head_of_lane).astype(BF16)
    return pl.pallas_call(
        _mix_kernel,
        grid=(batch, seq // tm),
        in_specs=[dilated(d, D) for d in dils] + [dilated(d, LANES) for d in dils] + [
            tok,
            pl.BlockSpec((None, SUBLANES, D), lambda b, i: (b, 0, 0)),
            pl.BlockSpec((LANES, D), const),
            pl.BlockSpec((D, D), const),
            pl.BlockSpec((2, D), const)],
        out_specs=tok,
        out_shape=jax.ShapeDtypeStruct(x.shape, F32),
        scratch_shapes=[pltpu.VMEM((N_GROUPS, tm, LANES), F32),
                        pltpu.VMEM((D // LANES, tm, LANES), F32)],
        compiler_params=_cparams(2),
        name="attn_mix",
    )(*outs, *lses, x, mod, expand, wo, ln)


def kernel(x, c, ada_w, ada_b, ln_g, ln_b, conv_w_in, conv_w, conv_w_out, kv_ada_w,
           kv_ada_b, w_kv, attn_w_q, attn_w_o, rel_bias, router_w, router_bias,
           moe_w_gate, moe_w_up, moe_w_down):
    batch, seq, _ = x.shape
    assert x.shape[2] == D and batch <= SUBLANES
    assert seq % WIDE_TOKEN_TILE == 0 and seq % (BLOCK * DIL_GROUPS[-1][1]) == 0

    c8 = jnp.pad(c, ((0, SUBLANES - batch), (0, 0)))
    mods = _ada_linear(c8, ada_w, ada_b)
    kvmods = _ada_linear(c8, kv_ada_w[None], kv_ada_b[None])[0]
    mod0 = _mod_rows(mods[0], batch, 6)
    mod1 = _mod_rows(mods[1], batch, 6)
    kvmod = _mod_rows(kvmods, batch, 2)
    ln = jnp.stack([ln_g, ln_b], axis=2)

    rw = router_w.T.astype(BF16)
    rbias = router_bias.reshape(N_EXPERTS, 1)
    experts = (moe_w_gate.astype(BF16), moe_w_up.astype(BF16),
               moe_w_down.astype(BF16).reshape(DEPTH, N_EXPERTS * EXPERT_FF, D))
    conv_w8 = jnp.pad(conv_w[0], ((0, SUBLANES - CONV_WIDTH), (0, 0)))

    x = _conv_mixer(x, mod0, conv_w_in[0].astype(BF16), conv_w8,
                    conv_w_out[0].astype(BF16), ln[0, 0])
    x = _moe(0, x, mod0, rw, rbias, *experts, ln[0, 1])

    qkv = _qkv(x, mod1, kvmod, attn_w_q[0].astype(BF16), w_kv.astype(BF16))
    bias = _bias_tiles(rel_bias)
    outs, lses = zip(*[_attention_group(g, *qkv[3 * g:3 * g + 3], bias)
                       for g in range(N_GROUPS)])
    x = _mix(outs, lses, x, mod1, attn_w_o[0].astype(BF16), ln[1, 0])
    x = _moe(1, x, mod1, rw, rbias, *experts, ln[1, 1])
    return x
```

```python
import math

import jax
import jax.numpy as jnp
from jax import lax
from jax.experimental import pallas as pl
from jax.experimental.pallas import tpu as pltpu

F32 = jnp.float32
BF16 = jnp.bfloat16

D = 1024
DEPTH = 2
CONV_WIDTH = 3
DIL_GROUPS = ((128, 1), (512, 4), (2048, 16))
N_GROUPS = len(DIL_GROUPS)
HEAD_DIM = 128
HEADS = D // HEAD_DIM
QW = N_GROUPS * D
BLOCK = 128
NUM_BUCKETS = 32
MAX_DISTANCE = 2048
N_EXPERTS = 16
EXPERTS_PER_GROUP = 4
N_EXPERT_GROUPS = N_EXPERTS // EXPERTS_PER_GROUP
EXPERT_FF = D // 4
ALPHA = (2 * DEPTH) ** 0.25
LN_EPS = 1e-5
NEG_INF = -1e30
LOG2E = math.log2(math.e)
LN2 = math.log(2.0)

LANES = 128
SUBLANES = 8
VMEM_LIMIT = 56 * 1024 * 1024
TOKEN_TILE = 512
WIDE_TOKEN_TILE = 1024
MOE_WIN_BIG = 256
MOE_WIN_MID = 192
MOE_WIN_SMALL = 128
BF16_ROWS = 16
ATTN_QBLOCKS = 16
ADA_COL_TILE = 2048


def _cparams(n_axes):
    return pltpu.CompilerParams(
        dimension_semantics=("arbitrary",) * n_axes, vmem_limit_bytes=VMEM_LIMIT)


def _layer_norm(r, g, b):
    mu = jnp.mean(r, axis=-1, keepdims=True)
    c = r - mu
    var = jnp.mean(c * c, axis=-1, keepdims=True)
    return c * lax.rsqrt(var + LN_EPS) * g + b


def _bdot(a, b):
    return jnp.dot(a, b, preferred_element_type=F32)


def _ada_kernel(c_ref, w_ref, b_ref, o_ref):
    c = c_ref[...]
    cond = c * jax.nn.sigmoid(c)
    o_ref[...] = _bdot(cond.astype(BF16), w_ref[...].astype(BF16)) + b_ref[...]


def _ada_linear(c8, w, bias):
    n_layers, _, n = w.shape
    nt = ADA_COL_TILE
    return pl.pallas_call(
        _ada_kernel,
        grid=(n_layers, n // nt),
        in_specs=[pl.BlockSpec((SUBLANES, D), lambda l, j: (0, 0)),
                  pl.BlockSpec((None, D, nt), lambda l, j: (l, 0, j)),
                  pl.BlockSpec((None, 1, nt), lambda l, j: (l, 0, j))],
        out_specs=pl.BlockSpec((None, SUBLANES, nt), lambda l, j: (l, 0, j)),
        out_shape=jax.ShapeDtypeStruct((n_layers, SUBLANES, n), F32),
        compiler_params=_cparams(2),
        name="ada_linear",
    )(c8, w, bias.reshape(n_layers, 1, n))


def _mod_rows(mods, batch, n_vec):
    m = mods[:batch].reshape(batch, n_vec, D)
    return jnp.pad(m, ((0, 0), (0, SUBLANES - n_vec), (0, 0)))


def _conv_mixer_kernel(x_ref, mod_ref, w_in_ref, cw_ref, w_out_ref, ln_ref,
                       o_ref, u_scr):
    tm = x_ref.shape[0]
    i = pl.program_id(1)
    mod = mod_ref[...]
    sh1, sc1, g1 = mod[0:1], mod[1:2], mod[2:3]
    x = x_ref[...]
    h = (x * (1.0 + sc1) + sh1).astype(BF16)
    cgate = _bdot(h, w_in_ref[:, D:2 * D])
    v = _bdot(h, w_in_ref[:, 2 * D:3 * D])
    u = cgate * v

    @pl.when(i == 0)
    def _():
        u_scr[0:SUBLANES, :] = jnp.zeros((SUBLANES, D), F32)

    u_scr[SUBLANES:SUBLANES + tm, :] = u
    cw = cw_ref[...]
    conv = (cw[2:3] * u
            + cw[1:2] * u_scr[SUBLANES - 1:SUBLANES - 1 + tm, :]
            + cw[0:1] * u_scr[SUBLANES - 2:SUBLANES - 2 + tm, :])
    u_scr[0:SUBLANES, :] = u_scr[tm:tm + SUBLANES, :]
    bgate = _bdot(h, w_in_ref[:, 0:D])
    y = _bdot((bgate * conv).astype(BF16), w_out_ref[...])
    ln = ln_ref[...]
    o_ref[...] = _layer_norm(ALPHA * x + (1.0 + g1) * y, ln[0:1], ln[1:2])


def _conv_mixer(x, mod, w_in, conv_w, w_out, ln):
    batch, seq, _ = x.shape
    tm = WIDE_TOKEN_TILE
    const = lambda b, i: (0, 0)
    return pl.pallas_call(
        _conv_mixer_kernel,
        grid=(batch, seq // tm),
        in_specs=[pl.BlockSpec((None, tm, D), lambda b, i: (b, i, 0)),
                  pl.BlockSpec((None, SUBLANES, D), lambda b, i: (b, 0, 0)),
                  pl.BlockSpec((D, 3 * D), const),
                  pl.BlockSpec((SUBLANES, D), const),
                  pl.BlockSpec((D, D), const),
                  pl.BlockSpec((2, D), const)],
        out_specs=pl.BlockSpec((None, tm, D), lambda b, i: (b, i, 0)),
        out_shape=jax.ShapeDtypeStruct(x.shape, F32),
        scratch_shapes=[pltpu.VMEM((tm + SUBLANES, D), F32)],
        compiler_params=_cparams(2),
        name="conv_mixer",
    )(x, mod, w_in, conv_w, w_out, ln)


def _route(logits_t, rbias_col):
    aff = jax.nn.sigmoid(logits_t)
    sel = aff + rbias_col
    rows = [sel[e:e + 1, :] for e in range(N_EXPERTS)]
    scores = []
    for g in range(N_EXPERT_GROUPS):
        s0, s1, s2, s3 = rows[EXPERTS_PER_GROUP * g:EXPERTS_PER_GROUP * (g + 1)]
        a, b = jnp.maximum(s0, s1), jnp.minimum(s0, s1)
        c, d = jnp.maximum(s2, s3), jnp.minimum(s2, s3)
        scores.append(jnp.maximum(a, c) + jnp.maximum(jnp.minimum(a, c), jnp.maximum(b, d)))
    best = scores[0]
    best_group = jnp.zeros(best.shape, jnp.int32)
    for g in range(1, N_EXPERT_GROUPS):
        upd = scores[g] > best
        best_group = jnp.where(upd, g, best_group)
        best = jnp.where(upd, scores[g], best)
    e_iota = lax.broadcasted_iota(jnp.int32, sel.shape, 0)
    masked = jnp.where((e_iota // EXPERTS_PER_GROUP) == best_group, sel, NEG_INF)
    m1 = jnp.max(masked, axis=0, keepdims=True)
    idx1 = jnp.min(jnp.where(masked == m1, e_iota, N_EXPERTS), axis=0, keepdims=True)
    masked2 = jnp.where(e_iota == idx1, -jnp.inf, masked)
    m2 = jnp.max(masked2, axis=0, keepdims=True)
    idx2 = jnp.min(jnp.where(masked2 == m2, e_iota, N_EXPERTS), axis=0, keepdims=True)
    a1 = jnp.sum(jnp.where(e_iota == idx1, aff, 0.0), axis=0, keepdims=True)
    a2 = jnp.sum(jnp.where(e_iota == idx2, aff, 0.0), axis=0, keepdims=True)
    den = a1 + a2
    comb = (jnp.where(e_iota == idx1, a1 / den, 0.0)
            + jnp.where(e_iota == idx2, a2 / den, 0.0))
    group1 = idx1 // EXPERTS_PER_GROUP
    split = group1 != idx2 // EXPERTS_PER_GROUP
    return comb, group1, split


def _moe_kernel(x_ref, xp_ref, mod_ref, modp_ref, rw_ref, rb_ref, tri_ref, wg_ref, wu_ref, wd_ref,
                ln_ref, o_ref, hs_scr, cs_scr, ys_scr, pt_scr):
    tm = x_ref.shape[0]
    gff = EXPERTS_PER_GROUP * EXPERT_FF
    t = pl.program_id(0)

    @pl.when(t == 0)
    def _():
        ys_scr[...] = jnp.zeros(ys_scr.shape, F32)
        pt_scr[...] = jnp.zeros(pt_scr.shape, BF16)
        hs_scr[tm:, :] = jnp.zeros((MOE_WIN_BIG, D), BF16)
        cs_scr[tm:, :] = jnp.zeros((MOE_WIN_BIG, LANES), F32)

    y = _bdot(pt_scr[...], ys_scr[0:tm, :].astype(BF16))
    ln = ln_ref[...]
    o_ref[...] = _layer_norm(ALPHA * xp_ref[...] + (1.0 + modp_ref[...][5:6]) * y,
                             ln[0:1], ln[1:2])

    mod = mod_ref[...]
    sh2, sc2 = mod[3:4], mod[4:5]
    h = x_ref[...] * (1.0 + sc2) + sh2
    hb = h.astype(BF16)
    logits_t = lax.dot_general(rw_ref[...], hb, (((1,), (1,)), ((), ())),
                               preferred_element_type=F32)
    comb_t, group, split = _route(logits_t, rb_ref[...])

    g_iota = lax.broadcasted_iota(jnp.int32, (SUBLANES, tm), 0)
    member = jnp.where(g_iota == group, 1.0, 0.0)
    rank = _bdot(member.astype(BF16), tri_ref[...])
    counts = jnp.sum(member, axis=1, keepdims=True)
    cnt = [counts[g, 0] for g in range(N_EXPERT_GROUPS)]
    start = [0.0]
    for g in range(N_EXPERT_GROUPS - 1):
        start.append(start[g] + cnt[g])
    dest = sum(member[g:g + 1] * (start[g] + rank[g:g + 1]) for g in range(N_EXPERT_GROUPS))
    row = lax.broadcasted_iota(jnp.int32, (tm, tm), 0).astype(F32)
    col = lax.broadcasted_iota(jnp.int32, (tm, tm), 1).astype(F32)
    perm = jnp.where(row == dest, 1.0, 0.0).astype(BF16)
    dest_col = jnp.broadcast_to(dest, (LANES, tm)).T
    dest_col = jnp.concatenate([dest_col] * (tm // LANES), axis=1)
    perm_t = jnp.where(dest_col == col, 1.0, 0.0).astype(BF16)
    pt_scr[...] = perm_t

    hs_scr[0:tm, :] = _bdot(perm, hb).astype(BF16)
    comb_hi = comb_t.astype(BF16)
    comb_lo = (comb_t - comb_hi.astype(F32)).astype(BF16)
    cs = _bdot(jnp.concatenate([comb_hi, comb_lo], axis=0), perm_t)
    cs = cs[0:N_EXPERTS] + cs[N_EXPERTS:2 * N_EXPERTS]
    cs_scr[0:tm, :] = jnp.concatenate(
        [cs, jnp.zeros((LANES - N_EXPERTS, tm), F32)], axis=0).T
    ys_scr[0:tm, :] = jnp.zeros((tm, D), F32)

    any_split = jnp.max(jnp.where(split, 1.0, 0.0)) > 0.0
    active = t < pl.num_programs(0) - 1
    for g in range(N_EXPERT_GROUPS):
        first = jnp.where(any_split, 0, jnp.asarray(start[g], F32).astype(jnp.int32))
        last = jnp.where(any_split, tm, (start[g] + cnt[g]).astype(jnp.int32))
        first = (first // BF16_ROWS) * BF16_ROWS
        need = jnp.where(jnp.logical_and(active, last > first), last - first, 0)
        n_full = need // MOE_WIN_BIG
        rest = need - n_full * MOE_WIN_BIG
        n_big = n_full + jnp.where(rest > MOE_WIN_MID, 1, 0)
        n_mid = jnp.where(jnp.logical_and(rest > MOE_WIN_SMALL, rest <= MOE_WIN_MID), 1, 0)
        n_small = jnp.where(jnp.logical_and(rest > 0, rest <= MOE_WIN_SMALL), 1, 0)

        def window(start_row, size, g=g):
            rows = pl.ds(pl.multiple_of(start_row, BF16_ROWS), size)
            hs = hs_scr[rows, :]
            parts = []
            for j in range(EXPERTS_PER_GROUP):
                e = g * EXPERTS_PER_GROUP + j
                gate = _bdot(hs, wg_ref[e])
                up = _bdot(hs, wu_ref[e])
                he = gate * jax.nn.sigmoid(gate) * up * cs_scr[rows, e:e + 1]
                parts.append(he.astype(BF16))
            he = jnp.concatenate(parts, axis=1)
            ys_scr[rows, :] += _bdot(he, wd_ref[g * gff:(g + 1) * gff, :])

        def big(w, carry, first=first, window=window):
            window(first + w * MOE_WIN_BIG, MOE_WIN_BIG)
            return carry

        def mid(w, carry, first=first, n_full=n_full, window=window):
            window(first + n_full * MOE_WIN_BIG, MOE_WIN_MID)
            return carry

        def small(w, carry, first=first, n_full=n_full, window=window):
            window(first + n_full * MOE_WIN_BIG, MOE_WIN_SMALL)
            return carry

        lax.fori_loop(0, n_big, big, 0)
        lax.fori_loop(0, n_mid, mid, 0)
        lax.fori_loop(0, n_small, small, 0)


def _moe(layer, x, mod, rw, rbias, wg, wu, wd, ln):
    batch, seq, _ = x.shape
    tm = TOKEN_TILE
    layer_weights = lambda shape: pl.BlockSpec(
        (None,) + shape, lambda t: (layer,) + (0,) * len(shape), pipeline_mode=pl.Buffered(1))
    per_batch = seq // tm
    n_tiles = batch * per_batch
    const2 = lambda t: (0, 0)
    cur = lambda t: jnp.minimum(t, n_tiles - 1)
    prev = lambda t: jnp.maximum(t - 1, 0)
    tile = lambda which: pl.BlockSpec(
        (None, tm, D), lambda t: (which(t) // per_batch, which(t) % per_batch, 0))
    mods = lambda which: pl.BlockSpec(
        (None, SUBLANES, D), lambda t: (which(t) // per_batch, 0, 0))
    tri = (jnp.arange(tm)[:, None] < jnp.arange(tm)[None, :]).astype(BF16)
    return pl.pallas_call(
        _moe_kernel,
        grid=(n_tiles + 1,),
        in_specs=[tile(cur), tile(prev), mods(cur), mods(prev),
                  pl.BlockSpec((N_EXPERTS, D), const2),
                  pl.BlockSpec((N_EXPERTS, 1), const2),
                  pl.BlockSpec((tm, tm), const2),
                  layer_weights((N_EXPERTS, D, EXPERT_FF)),
                  layer_weights((N_EXPERTS, D, EXPERT_FF)),
                  layer_weights((N_EXPERTS * EXPERT_FF, D)),
                  pl.BlockSpec((2, D), const2)],
        out_specs=tile(prev),
        out_shape=jax.ShapeDtypeStruct(x.shape, F32),
        scratch_shapes=[pltpu.VMEM((tm + MOE_WIN_BIG, D), BF16),
                        pltpu.VMEM((tm + MOE_WIN_BIG, LANES), F32),
                        pltpu.VMEM((tm + MOE_WIN_BIG, D), F32),
                        pltpu.VMEM((tm, tm), BF16)],
        compiler_params=_cparams(1),
        name="moe",
    )(x, x, mod, mod, rw, rbias, tri, wg, wu, wd, ln)


def _qkv_kernel(x_ref, mod_ref, kvmod_ref, wq_ref, wkv_ref, *refs):
    outs, (slab, hq_scr, hkv_scr) = refs[:3 * N_GROUPS], refs[3 * N_GROUPS:]
    tm = x_ref.shape[0]
    mod = mod_ref[...]
    kvmod = kvmod_ref[...]
    q_sc, q_sh = 1.0 + mod[1:2], mod[0:1]
    kv_sc, kv_sh = 1.0 + kvmod[1:2], kvmod[0:1]
    n_slabs = D // LANES
    for j in range(n_slabs):
        slab[j] = x_ref[:, j * LANES:(j + 1) * LANES]
    scale = HEAD_DIM ** -0.5 * LOG2E
    for g, (_, dil) in enumerate(DIL_GROUPS):
        n = tm // dil
        for r in range(dil):
            rows = slice(r * n, (r + 1) * n)
            for j in range(n_slabs):
                ls = slice(j * LANES, (j + 1) * LANES)
                piece = slab[j] if dil == 1 else slab[j, pl.ds(r, n, stride=dil), :]
                hq_scr[rows, ls] = (piece * q_sc[:, ls] + q_sh[:, ls]).astype(BF16)
                hkv_scr[rows, ls] = (piece * kv_sc[:, ls] + kv_sh[:, ls]).astype(BF16)
        hq = hq_scr[...]
        hkv = hkv_scr[...]
        cols = slice(g * D, (g + 1) * D)
        vcols = slice(QW + g * D, QW + (g + 1) * D)
        q = (_bdot(hq, wq_ref[:, cols]) * scale).astype(BF16)
        k = _bdot(hkv, wkv_ref[:, cols]).astype(BF16)
        v = _bdot(hkv, wkv_ref[:, vcols]).astype(BF16)
        for r in range(dil):
            rows = slice(r * n, (r + 1) * n)
            outs[3 * g][r] = q[rows]
            outs[3 * g + 1][r] = k[rows]
            outs[3 * g + 2][r] = v[rows]


def _qkv(x, mod, kvmod, wq, wkv):
    batch, seq, _ = x.shape
    tm = TOKEN_TILE
    const = lambda b, i: (0, 0)
    out_specs, out_shape = [], []
    for _, dil in DIL_GROUPS:
        for _ in range(3):
            out_specs.append(pl.BlockSpec((None, dil, tm // dil, D), lambda b, i: (b, 0, i, 0)))
            out_shape.append(jax.ShapeDtypeStruct((batch, dil, seq // dil, D), BF16))
    return pl.pallas_call(
        _qkv_kernel,
        grid=(batch, seq // tm),
        in_specs=[pl.BlockSpec((None, tm, D), lambda b, i: (b, i, 0)),
                  pl.BlockSpec((None, SUBLANES, D), lambda b, i: (b, 0, 0)),
                  pl.BlockSpec((None, SUBLANES, D), lambda b, i: (b, 0, 0)),
                  pl.BlockSpec((D, QW), const),
                  pl.BlockSpec((D, 2 * QW), const)],
        out_specs=out_specs,
        out_shape=out_shape,
        scratch_shapes=[pltpu.VMEM((D // LANES, tm, LANES), F32),
                        pltpu.VMEM((tm, D), BF16),
                        pltpu.VMEM((tm, D), BF16)],
        compiler_params=_cparams(2),
        name="qkv_proj",
    )(x, mod, kvmod, wq, wkv)


def _bucket_tables():
    qi = jnp.arange(BLOCK)[:, None]
    kj = jnp.arange(2 * BLOCK)[None, :]
    dist = BLOCK + qi - kj
    max_exact = NUM_BUCKETS // 2
    tables = []
    for window, dil in DIL_GROUPS:
        n = jnp.maximum(dist, 0) * dil
        nf = jnp.maximum(n, 1).astype(F32)
        large = max_exact + (jnp.log(nf / max_exact) / math.log(MAX_DISTANCE / max_exact)
                             * (NUM_BUCKETS - max_exact)).astype(jnp.int32)
        large = jnp.minimum(large, NUM_BUCKETS - 1)
        bucket = jnp.where(n < max_exact, n, large)
        valid = (dist >= 0) & (dist <= window // dil)
        tables.append(jnp.where(valid, bucket, -1).astype(jnp.int32))
    return jnp.stack(tables)


def _bias_kernel(rb_ref, bkt_ref, o_ref):
    g = pl.program_id(0)
    bkt = bkt_ref[...]
    for h in range(HEADS):
        acc = jnp.full(bkt.shape, NEG_INF, F32)
        for b in range(NUM_BUCKETS):
            acc = jnp.where(bkt == b, rb_ref[b, g * HEADS + h] * LOG2E, acc)
        o_ref[h] = acc


def _bias_tiles(rel_bias):
    return pl.pallas_call(
        _bias_kernel,
        grid=(N_GROUPS,),
        in_specs=[pl.BlockSpec(memory_space=pltpu.SMEM),
                  pl.BlockSpec((None, BLOCK, 2 * BLOCK), lambda g: (g, 0, 0))],
        out_specs=pl.BlockSpec((None, HEADS, BLOCK, 2 * BLOCK), lambda g: (g, 0, 0, 0)),
        out_shape=jax.ShapeDtypeStruct((N_GROUPS, HEADS, BLOCK, 2 * BLOCK), F32),
        compiler_params=_cparams(1),
        name="rel_bias_tiles",
    )(rel_bias, _bucket_tables())


def _attn_kernel(q_ref, kp_ref, kc_ref, vp_ref, vc_ref, bias_ref, o_ref, lse_ref):
    n = pl.program_id(2)
    col = lax.broadcasted_iota(jnp.int32, (BLOCK, 2 * BLOCK), 1)
    no_prev = jnp.logical_and(n == 0, col < BLOCK)
    lane = lax.broadcasted_iota(jnp.int32, (BLOCK, LANES), 1)
    ones = jnp.ones((2 * BLOCK, HEAD_DIM), BF16)
    for r in range(q_ref.shape[0]):
        for j in range(q_ref.shape[1] // BLOCK):
            rows = slice(j * BLOCK, (j + 1) * BLOCK)
            window = slice((j - 1) * BLOCK, (j + 1) * BLOCK)
            m_tile = jnp.zeros((BLOCK, LANES), F32)
            l_tile = jnp.ones((BLOCK, LANES), F32)
            for h in range(HEADS):
                hs = slice(h * HEAD_DIM, (h + 1) * HEAD_DIM)
                if j == 0:
                    k = jnp.concatenate([kp_ref[r, :, hs], kc_ref[r, rows, hs]], axis=0)
                    v = jnp.concatenate([vp_ref[r, :, hs], vc_ref[r, rows, hs]], axis=0)
                else:
                    k = kc_ref[r, window, hs]
                    v = vc_ref[r, window, hs]
                s = lax.dot_general(q_ref[r, rows, hs], k, (((1,), (1,)), ((), ())),
                                    preferred_element_type=F32) + bias_ref[h]
                if j == 0:
                    s = jnp.where(no_prev, NEG_INF, s)
                m = jnp.max(s, axis=-1, keepdims=True)
                p = jnp.exp2(s - m)
                o_ext = _bdot(p.astype(BF16), jnp.concatenate([v, ones], axis=1))
                o, l = o_ext[:, :HEAD_DIM], o_ext[:, HEAD_DIM:]
                o_ref[r, rows, hs] = (o / l).astype(BF16)
                m_tile = jnp.where(lane == h, m, m_tile)
                l_tile = jnp.where(lane == h, l, l_tile)
            lse_ref[r, rows, :] = m_tile * LN2 + jnp.log(l_tile)


def _attention_group(g, q, k, v, bias):
    batch, dil, sub_len, _ = q.shape
    nq = min(ATTN_QBLOCKS, sub_len // BLOCK)
    n_sub = min(dil, ATTN_QBLOCKS // nq)
    assert sub_len % (nq * BLOCK) == 0 and dil % n_sub == 0
    cur = lambda b, r, n: (b, r, n, 0)
    if sub_len == nq * BLOCK:
        prev = lambda b, r, n: (0, 0, 0, 0)
    else:
        prev = lambda b, r, n: (b, r, jnp.maximum(n * nq - 1, 0), 0)
    blk = (None, n_sub, nq * BLOCK, D)
    blk_prev = (None, n_sub, BLOCK, D)
    return pl.pallas_call(
        _attn_kernel,
        grid=(batch, dil // n_sub, sub_len // (nq * BLOCK)),
        in_specs=[
            pl.BlockSpec(blk, cur),
            pl.BlockSpec(blk_prev, prev),
            pl.BlockSpec(blk, cur),
            pl.BlockSpec(blk_prev, prev),
            pl.BlockSpec(blk, cur),
            pl.BlockSpec((None, HEADS, BLOCK, 2 * BLOCK), lambda b, r, n: (g, 0, 0, 0)),
        ],
        out_specs=[pl.BlockSpec(blk, cur),
                   pl.BlockSpec((None, n_sub, nq * BLOCK, LANES), cur)],
        out_shape=[jax.ShapeDtypeStruct((batch, dil, sub_len, D), BF16),
                   jax.ShapeDtypeStruct((batch, dil, sub_len, LANES), F32)],
        compiler_params=_cparams(3),
        name=f"dilated_attn_g{g}",
    )(q, k, k, v, v, bias)


def _mix_kernel(o0_ref, o1_ref, o2_ref, l0_ref, l1_ref, l2_ref, x_ref, mod_ref,
                expand_ref, wo_ref, ln_ref, out_ref, lse_scr, o_slab):
    tm = x_ref.shape[0]
    n_slabs = D // LANES
    lses = []
    for g, l_ref in enumerate((l0_ref, l1_ref, l2_ref)):
        dil = DIL_GROUPS[g][1]
        if dil == 1:
            lses.append(l_ref[0])
            continue
        for r in range(dil):
            lse_scr[g, pl.ds(r, tm // dil, stride=dil), :] = l_ref[r]
        lses.append(lse_scr[g])
    m = jnp.maximum(lses[0], jnp.maximum(lses[1], lses[2]))
    es = [jnp.exp(a - m) for a in lses]
    den = es[0] + es[1] + es[2]
    expand = expand_ref[...]
    mixed = None
    for g, o_ref in enumerate((o0_ref, o1_ref, o2_ref)):
        dil = DIL_GROUPS[g][1]
        w = _bdot((es[g] / den).astype(BF16), expand)
        if dil == 1:
            o = o_ref[0].astype(F32)
        else:
            for r in range(dil):
                for j in range(n_slabs):
                    o_slab[j, pl.ds(r, tm // dil, stride=dil), :] = (
                        o_ref[r, :, j * LANES:(j + 1) * LANES].astype(F32))
            o = jnp.concatenate([o_slab[j] for j in range(n_slabs)], axis=1)
        mixed = w * o if mixed is None else mixed + w * o
    y = _bdot(mixed.astype(BF16), wo_ref[...])
    g1 = mod_ref[...][2:3]
    ln = ln_ref[...]
    out_ref[...] = _layer_norm(ALPHA * x_ref[...] + (1.0 + g1) * y, ln[0:1], ln[1:2])


def _mix(outs, lses, x, mod, wo, ln):
    batch, seq, _ = x.shape
    tm = WIDE_TOKEN_TILE
    const = lambda b, i: (0, 0)
    tok = pl.BlockSpec((None, tm, D), lambda b, i: (b, i, 0))
    dilated = lambda dil, w: pl.BlockSpec((None, dil, tm // dil, w), lambda b, i: (b, 0, i, 0))
    dils = [dil for _, dil in DIL_GROUPS]
    head_of_lane = jnp.arange(D)[None, :] // HEAD_DIM
    expand = (jnp.arange(LANES)[:, None] == head_of_lane).astype(BF16)
    return pl.pallas_call(
        _mix_kernel,
        grid=(batch, seq // tm),
        in_specs=[dilated(d, D) for d in dils] + [dilated(d, LANES) for d in dils] + [
            tok,
            pl.BlockSpec((None, SUBLANES, D), lambda b, i: (b, 0, 0)),
            pl.BlockSpec((LANES, D), const),
            pl.BlockSpec((D, D), const),
            pl.BlockSpec((2, D), const)],
        out_specs=tok,
        out_shape=jax.ShapeDtypeStruct(x.shape, F32),
        scratch_shapes=[pltpu.VMEM((N_GROUPS, tm, LANES), F32),
                        pltpu.VMEM((D // LANES, tm, LANES), F32)],
        compiler_params=_cparams(2),
        name="attn_mix",
    )(*outs, *lses, x, mod, expand, wo, ln)


def kernel(x, c, ada_w, ada_b, ln_g, ln_b, conv_w_in, conv_w, conv_w_out, kv_ada_w,
           kv_ada_b, w_kv, attn_w_q, attn_w_o, rel_bias, router_w, router_bias,
           moe_w_gate, moe_w_up, moe_w_down):
    batch, seq, _ = x.shape
    assert x.shape[2] == D and batch <= SUBLANES
    assert seq % WIDE_TOKEN_TILE == 0 and seq % (BLOCK * DIL_GROUPS[-1][1]) == 0

    c8 = jnp.pad(c, ((0, SUBLANES - batch), (0, 0)))
    mods = _ada_linear(c8, ada_w, ada_b)
    kvmods = _ada_linear(c8, kv_ada_w[None], kv_ada_b[None])[0]
    mod0 = _mod_rows(mods[0], batch, 6)
    mod1 = _mod_rows(mods[1], batch, 6)
    kvmod = _mod_rows(kvmods, batch, 2)
    ln = jnp.stack([ln_g, ln_b], axis=2)

    rw = router_w.T.astype(BF16)
    rbias = router_bias.reshape(N_EXPERTS, 1)
    experts = (moe_w_gate.astype(BF16), moe_w_up.astype(BF16),
               moe_w_down.astype(BF16).reshape(DEPTH, N_EXPERTS * EXPERT_FF, D))
    conv_w8 = jnp.pad(conv_w[0], ((0, SUBLANES - CONV_WIDTH), (0, 0)))

    x = _conv_mixer(x, mod0, conv_w_in[0].astype(BF16), conv_w8,
                    conv_w_out[0].astype(BF16), ln[0, 0])
    x = _moe(0, x, mod0, rw, rbias, *experts, ln[0, 1])

    qkv = _qkv(x, mod1, kvmod, attn_w_q[0].astype(BF16), w_kv.astype(BF16))
    bias = _bias_tiles(rel_bias)
    outs, lses = zip(*[_attention_group(g, *qkv[3 * g:3 * g + 3], bias)
                       for g in range(N_GROUPS)])
    x = _mix(outs, lses, x, mod1, attn_w_o[0].astype(BF16), ln[1, 0])
    x = _moe(1, x, mod1, rw, rbias, *experts, ln[1, 1])
    return x
```

```python
import math

import jax
import jax.numpy as jnp
from jax import lax
from jax.experimental import pallas as pl
from jax.experimental.pallas import tpu as pltpu

F32 = jnp.float32
BF16 = jnp.bfloat16

D = 1024
DEPTH = 2
CONV_WIDTH = 3
DIL_GROUPS = ((128, 1), (512, 4), (2048, 16))
N_GROUPS = len(DIL_GROUPS)
HEAD_DIM = 128
HEADS = D // HEAD_DIM
QW = N_GROUPS * D
BLOCK = 128
NUM_BUCKETS = 32
MAX_DISTANCE = 2048
N_EXPERTS = 16
EXPERTS_PER_GROUP = 4
N_EXPERT_GROUPS = N_EXPERTS // EXPERTS_PER_GROUP
EXPERT_FF = D // 4
ALPHA = (2 * DEPTH) ** 0.25
LN_EPS = 1e-5
NEG_INF = -1e30
LOG2E = math.log2(math.e)
LN2 = math.log(2.0)

LANES = 128
SUBLANES = 8
VMEM_LIMIT = 56 * 1024 * 1024
TOKEN_TILE = 512
WIDE_TOKEN_TILE = 1024
MOE_WIN_BIG = 256
MOE_WIN_REST = (128, 160, 192, 224, 256)
BF16_ROWS = 16
ATTN_QBLOCKS = 16
ADA_COL_TILE = 2048


def _cparams(n_axes):
    return pltpu.CompilerParams(
        dimension_semantics=("arbitrary",) * n_axes, vmem_limit_bytes=VMEM_LIMIT)


def _layer_norm(r, g, b):
    mu = jnp.mean(r, axis=-1, keepdims=True)
    c = r - mu
    var = jnp.mean(c * c, axis=-1, keepdims=True)
    return c * lax.rsqrt(var + LN_EPS) * g + b


def _bdot(a, b):
    return jnp.dot(a, b, preferred_element_type=F32)


def _ada_kernel(c_ref, w_ref, b_ref, o_ref):
    c = c_ref[...]
    cond = c * jax.nn.sigmoid(c)
    o_ref[...] = _bdot(cond.astype(BF16), w_ref[...].astype(BF16)) + b_ref[...]


def _ada_linear(c8, w, bias):
    n_layers, _, n = w.shape
    nt = ADA_COL_TILE
    return pl.pallas_call(
        _ada_kernel,
        grid=(n_layers, n // nt),
        in_specs=[pl.BlockSpec((SUBLANES, D), lambda l, j: (0, 0)),
                  pl.BlockSpec((None, D, nt), lambda l, j: (l, 0, j)),
                  pl.BlockSpec((None, 1, nt), lambda l, j: (l, 0, j))],
        out_specs=pl.BlockSpec((None, SUBLANES, nt), lambda l, j: (l, 0, j)),
        out_shape=jax.ShapeDtypeStruct((n_layers, SUBLANES, n), F32),
        compiler_params=_cparams(2),
        name="ada_linear",
    )(c8, w, bias.reshape(n_layers, 1, n))


def _mod_rows(mods, batch, n_vec):
    m = mods[:batch].reshape(batch, n_vec, D)
    return jnp.pad(m, ((0, 0), (0, SUBLANES - n_vec), (0, 0)))


def _conv_mixer_kernel(x_ref, mod_ref, w_in_ref, cw_ref, w_out_ref, ln_ref,
                       o_ref, u_scr):
    tm = x_ref.shape[0]
    i = pl.program_id(1)
    mod = mod_ref[...]
    sh1, sc1, g1 = mod[0:1], mod[1:2], mod[2:3]
    x = x_ref[...]
    h = (x * (1.0 + sc1) + sh1).astype(BF16)
    cgate = _bdot(h, w_in_ref[:, D:2 * D])
    v = _bdot(h, w_in_ref[:, 2 * D:3 * D])
    u = cgate * v

    @pl.when(i == 0)
    def _():
        u_scr[0:SUBLANES, :] = jnp.zeros((SUBLANES, D), F32)

    u_scr[SUBLANES:SUBLANES + tm, :] = u
    cw = cw_ref[...]
    conv = (cw[2:3] * u
            + cw[1:2] * u_scr[SUBLANES - 1:SUBLANES - 1 + tm, :]
            + cw[0:1] * u_scr[SUBLANES - 2:SUBLANES - 2 + tm, :])
    u_scr[0:SUBLANES, :] = u_scr[tm:tm + SUBLANES, :]
    bgate = _bdot(h, w_in_ref[:, 0:D])
    y = _bdot((bgate * conv).astype(BF16), w_out_ref[...])
    ln = ln_ref[...]
    o_ref[...] = _layer_norm(ALPHA * x + (1.0 + g1) * y, ln[0:1], ln[1:2])


def _conv_mixer(x, mod, w_in, conv_w, w_out, ln):
    batch, seq, _ = x.shape
    tm = WIDE_TOKEN_TILE
    const = lambda b, i: (0, 0)
    return pl.pallas_call(
        _conv_mixer_kernel,
        grid=(batch, seq // tm),
        in_specs=[pl.BlockSpec((None, tm, D), lambda b, i: (b, i, 0)),
                  pl.BlockSpec((None, SUBLANES, D), lambda b, i: (b, 0, 0)),
                  pl.BlockSpec((D, 3 * D), const),
                  pl.BlockSpec((SUBLANES, D), const),
                  pl.BlockSpec((D, D), const),
                  pl.BlockSpec((2, D), const)],
        out_specs=pl.BlockSpec((None, tm, D), lambda b, i: (b, i, 0)),
        out_shape=jax.ShapeDtypeStruct(x.shape, F32),
        scratch_shapes=[pltpu.VMEM((tm + SUBLANES, D), F32)],
        compiler_params=_cparams(2),
        name="conv_mixer",
    )(x, mod, w_in, conv_w, w_out, ln)


def _route(logits_t, rbias_col):
    aff = jax.nn.sigmoid(logits_t)
    sel = aff + rbias_col
    rows = [sel[e:e + 1, :] for e in range(N_EXPERTS)]
    scores = []
    for g in range(N_EXPERT_GROUPS):
        s0, s1, s2, s3 = rows[EXPERTS_PER_GROUP * g:EXPERTS_PER_GROUP * (g + 1)]
        a, b = jnp.maximum(s0, s1), jnp.minimum(s0, s1)
        c, d = jnp.maximum(s2, s3), jnp.minimum(s2, s3)
        scores.append(jnp.maximum(a, c) + jnp.maximum(jnp.minimum(a, c), jnp.maximum(b, d)))
    best = scores[0]
    best_group = jnp.zeros(best.shape, jnp.int32)
    for g in range(1, N_EXPERT_GROUPS):
        upd = scores[g] > best
        best_group = jnp.where(upd, g, best_group)
        best = jnp.where(upd, scores[g], best)
    e_iota = lax.broadcasted_iota(jnp.int32, sel.shape, 0)
    masked = jnp.where((e_iota // EXPERTS_PER_GROUP) == best_group, sel, NEG_INF)
    m1 = jnp.max(masked, axis=0, keepdims=True)
    idx1 = jnp.min(jnp.where(masked == m1, e_iota, N_EXPERTS), axis=0, keepdims=True)
    masked2 = jnp.where(e_iota == idx1, -jnp.inf, masked)
    m2 = jnp.max(masked2, axis=0, keepdims=True)
    idx2 = jnp.min(jnp.where(masked2 == m2, e_iota, N_EXPERTS), axis=0, keepdims=True)
    a1 = jnp.sum(jnp.where(e_iota == idx1, aff, 0.0), axis=0, keepdims=True)
    a2 = jnp.sum(jnp.where(e_iota == idx2, aff, 0.0), axis=0, keepdims=True)
    den = a1 + a2
    comb = (jnp.where(e_iota == idx1, a1 / den, 0.0)
            + jnp.where(e_iota == idx2, a2 / den, 0.0))
    group1 = idx1 // EXPERTS_PER_GROUP
    split = group1 != idx2 // EXPERTS_PER_GROUP
    return comb, group1, split


def _moe_kernel(x_ref, xp_ref, mod_ref, modp_ref, rw_ref, rb_ref, tri_ref, wg_ref, wu_ref, wd_ref,
                ln_ref, o_ref, hs_scr, cs_scr, ys_scr, pt_scr):
    tm = x_ref.shape[0]
    gff = EXPERTS_PER_GROUP * EXPERT_FF
    t = pl.program_id(0)

    @pl.when(t == 0)
    def _():
        ys_scr[...] = jnp.zeros(ys_scr.shape, F32)
        pt_scr[...] = jnp.zeros(pt_scr.shape, BF16)
        hs_scr[tm:, :] = jnp.zeros((MOE_WIN_BIG, D), BF16)
        cs_scr[tm:, :] = jnp.zeros((MOE_WIN_BIG, LANES), F32)

    y = _bdot(pt_scr[...], ys_scr[0:tm, :].astype(BF16))
    ln = ln_ref[...]
    o_ref[...] = _layer_norm(ALPHA * xp_ref[...] + (1.0 + modp_ref[...][5:6]) * y,
                             ln[0:1], ln[1:2])

    mod = mod_ref[...]
    sh2, sc2 = mod[3:4], mod[4:5]
    h = x_ref[...] * (1.0 + sc2) + sh2
    hb = h.astype(BF16)
    logits_t = lax.dot_general(rw_ref[...], hb, (((1,), (1,)), ((), ())),
                               preferred_element_type=F32)
    comb_t, group, split = _route(logits_t, rb_ref[...])

    g_iota = lax.broadcasted_iota(jnp.int32, (SUBLANES, tm), 0)
    member = jnp.where(g_iota == group, 1.0, 0.0)
    rank = _bdot(member.astype(BF16), tri_ref[...])
    counts = jnp.sum(member, axis=1, keepdims=True)
    cnt = [counts[g, 0] for g in range(N_EXPERT_GROUPS)]
    start = [0.0]
    for g in range(N_EXPERT_GROUPS - 1):
        start.append(start[g] + cnt[g])
    dest = sum(member[g:g + 1] * (start[g] + rank[g:g + 1]) for g in range(N_EXPERT_GROUPS))
    row = lax.broadcasted_iota(jnp.int32, (tm, tm), 0).astype(F32)
    col = lax.broadcasted_iota(jnp.int32, (tm, tm), 1).astype(F32)
    perm = jnp.where(row == dest, 1.0, 0.0).astype(BF16)
    dest_col = jnp.broadcast_to(dest, (LANES, tm)).T
    dest_col = jnp.concatenate([dest_col] * (tm // LANES), axis=1)
    perm_t = jnp.where(dest_col == col, 1.0, 0.0).astype(BF16)
    pt_scr[...] = perm_t

    hs_scr[0:tm, :] = _bdot(perm, hb).astype(BF16)
    comb_hi = comb_t.astype(BF16)
    comb_lo = (comb_t - comb_hi.astype(F32)).astype(BF16)
    cs = _bdot(jnp.concatenate([comb_hi, comb_lo], axis=0), perm_t)
    cs = cs[0:N_EXPERTS] + cs[N_EXPERTS:2 * N_EXPERTS]
    cs_scr[0:tm, :] = jnp.concatenate(
        [cs, jnp.zeros((LANES - N_EXPERTS, tm), F32)], axis=0).T
    ys_scr[0:tm, :] = jnp.zeros((tm, D), F32)

    any_split = jnp.max(jnp.where(split, 1.0, 0.0)) > 0.0
    active = t < pl.num_programs(0) - 1
    for g in range(N_EXPERT_GROUPS):
        first = jnp.where(any_split, 0, jnp.asarray(start[g], F32).astype(jnp.int32))
        last = jnp.where(any_split, tm, (start[g] + cnt[g]).astype(jnp.int32))
        first = (first // BF16_ROWS) * BF16_ROWS
        need = jnp.where(jnp.logical_and(active, last > first), last - first, 0)
        n_full = need // MOE_WIN_BIG
        rest = need - n_full * MOE_WIN_BIG

        def window(start_row, size, g=g):
            rows = pl.ds(pl.multiple_of(start_row, BF16_ROWS), size)
            hs = hs_scr[rows, :]
            parts = []
            for j in range(EXPERTS_PER_GROUP):
                e = g * EXPERTS_PER_GROUP + j
                gate = _bdot(hs, wg_ref[e])
                up = _bdot(hs, wu_ref[e])
                he = gate * jax.nn.sigmoid(gate) * up * cs_scr[rows, e:e + 1]
                parts.append(he.astype(BF16))
            he = jnp.concatenate(parts, axis=1)
            ys_scr[rows, :] += _bdot(he, wd_ref[g * gff:(g + 1) * gff, :])

        def full(w, carry, first=first, window=window):
            window(first + w * MOE_WIN_BIG, MOE_WIN_BIG)
            return carry

        lax.fori_loop(0, n_full, full, 0)
        smaller = 0
        for size in MOE_WIN_REST:
            fits = jnp.logical_and(rest > smaller, rest <= size)

            def last_window(w, carry, first=first, n_full=n_full, window=window, size=size):
                window(first + n_full * MOE_WIN_BIG, size)
                return carry

            lax.fori_loop(0, jnp.where(fits, 1, 0), last_window, 0)
            smaller = size


def _moe(layer, x, mod, rw, rbias, wg, wu, wd, ln):
    batch, seq, _ = x.shape
    tm = TOKEN_TILE
    layer_weights = lambda shape: pl.BlockSpec(
        (None,) + shape, lambda t: (layer,) + (0,) * len(shape), pipeline_mode=pl.Buffered(1))
    per_batch = seq // tm
    n_tiles = batch * per_batch
    const2 = lambda t: (0, 0)
    cur = lambda t: jnp.minimum(t, n_tiles - 1)
    prev = lambda t: jnp.maximum(t - 1, 0)
    tile = lambda which: pl.BlockSpec(
        (None, tm, D), lambda t: (which(t) // per_batch, which(t) % per_batch, 0))
    mods = lambda which: pl.BlockSpec(
        (None, SUBLANES, D), lambda t: (which(t) // per_batch, 0, 0))
    tri = (jnp.arange(tm)[:, None] < jnp.arange(tm)[None, :]).astype(BF16)
    return pl.pallas_call(
        _moe_kernel,
        grid=(n_tiles + 1,),
        in_specs=[tile(cur), tile(prev), mods(cur), mods(prev),
                  pl.BlockSpec((N_EXPERTS, D), const2),
                  pl.BlockSpec((N_EXPERTS, 1), const2),
                  pl.BlockSpec((tm, tm), const2),
                  layer_weights((N_EXPERTS, D, EXPERT_FF)),
                  layer_weights((N_EXPERTS, D, EXPERT_FF)),
                  layer_weights((N_EXPERTS * EXPERT_FF, D)),
                  pl.BlockSpec((2, D), const2)],
        out_specs=tile(prev),
        out_shape=jax.ShapeDtypeStruct(x.shape, F32),
        scratch_shapes=[pltpu.VMEM((tm + MOE_WIN_BIG, D), BF16),
                        pltpu.VMEM((tm + MOE_WIN_BIG, LANES), F32),
                        pltpu.VMEM((tm + MOE_WIN_BIG, D), F32),
                        pltpu.VMEM((tm, tm), BF16)],
        compiler_params=_cparams(1),
        name="moe",
    )(x, x, mod, mod, rw, rbias, tri, wg, wu, wd, ln)


def _qkv_kernel(x_ref, mod_ref, kvmod_ref, wq_ref, wkv_ref, *refs):
    outs, (slab, hq_scr, hkv_scr) = refs[:3 * N_GROUPS], refs[3 * N_GROUPS:]
    tm = x_ref.shape[0]
    mod = mod_ref[...]
    kvmod = kvmod_ref[...]
    q_sc, q_sh = 1.0 + mod[1:2], mod[0:1]
    kv_sc, kv_sh = 1.0 + kvmod[1:2], kvmod[0:1]
    n_slabs = D // LANES
    for j in range(n_slabs):
        slab[j] = x_ref[:, j * LANES:(j + 1) * LANES]
    scale = HEAD_DIM ** -0.5 * LOG2E
    for g, (_, dil) in enumerate(DIL_GROUPS):
        n = tm // dil
        for r in range(dil):
            rows = slice(r * n, (r + 1) * n)
            for j in range(n_slabs):
                ls = slice(j * LANES, (j + 1) * LANES)
                piece = slab[j] if dil == 1 else slab[j, pl.ds(r, n, stride=dil), :]
                hq_scr[rows, ls] = (piece * q_sc[:, ls] + q_sh[:, ls]).astype(BF16)
                hkv_scr[rows, ls] = (piece * kv_sc[:, ls] + kv_sh[:, ls]).astype(BF16)
        hq = hq_scr[...]
        hkv = hkv_scr[...]
        cols = slice(g * D, (g + 1) * D)
        vcols = slice(QW + g * D, QW + (g + 1) * D)
        q = (_bdot(hq, wq_ref[:, cols]) * scale).astype(BF16)
        k = _bdot(hkv, wkv_ref[:, cols]).astype(BF16)
        v = _bdot(hkv, wkv_ref[:, vcols]).astype(BF16)
        for r in range(dil):
            rows = slice(r * n, (r + 1) * n)
            outs[3 * g][r] = q[rows]
            outs[3 * g + 1][r] = k[rows]
            outs[3 * g + 2][r] = v[rows]


def _qkv(x, mod, kvmod, wq, wkv):
    batch, seq, _ = x.shape
    tm = TOKEN_TILE
    const = lambda b, i: (0, 0)
    out_specs, out_shape = [], []
    for _, dil in DIL_GROUPS:
        for _ in range(3):
            out_specs.append(pl.BlockSpec((None, dil, tm // dil, D), lambda b, i: (b, 0, i, 0)))
            out_shape.append(jax.ShapeDtypeStruct((batch, dil, seq // dil, D), BF16))
    return pl.pallas_call(
        _qkv_kernel,
        grid=(batch, seq // tm),
        in_specs=[pl.BlockSpec((None, tm, D), lambda b, i: (b, i, 0)),
                  pl.BlockSpec((None, SUBLANES, D), lambda b, i: (b, 0, 0)),
                  pl.BlockSpec((None, SUBLANES, D), lambda b, i: (b, 0, 0)),
                  pl.BlockSpec((D, QW), const),
                  pl.BlockSpec((D, 2 * QW), const)],
        out_specs=out_specs,
        out_shape=out_shape,
        scratch_shapes=[pltpu.VMEM((D // LANES, tm, LANES), F32),
                        pltpu.VMEM((tm, D), BF16),
                        pltpu.VMEM((tm, D), BF16)],
        compiler_params=_cparams(2),
        name="qkv_proj",
    )(x, mod, kvmod, wq, wkv)


def _bucket_tables():
    qi = jnp.arange(BLOCK)[:, None]
    kj = jnp.arange(2 * BLOCK)[None, :]
    dist = BLOCK + qi - kj
    max_exact = NUM_BUCKETS // 2
    tables = []
    for window, dil in DIL_GROUPS:
        n = jnp.maximum(dist, 0) * dil
        nf = jnp.maximum(n, 1).astype(F32)
        large = max_exact + (jnp.log(nf / max_exact) / math.log(MAX_DISTANCE / max_exact)
                             * (NUM_BUCKETS - max_exact)).astype(jnp.int32)
        large = jnp.minimum(large, NUM_BUCKETS - 1)
        bucket = jnp.where(n < max_exact, n, large)
        valid = (dist >= 0) & (dist <= window // dil)
        tables.append(jnp.where(valid, bucket, -1).astype(jnp.int32))
    return jnp.stack(tables)


def _bias_kernel(rb_ref, bkt_ref, o_ref):
    g = pl.program_id(0)
    bkt = bkt_ref[...]
    for h in range(HEADS):
        acc = jnp.full(bkt.shape, NEG_INF, F32)
        for b in range(NUM_BUCKETS):
            acc = jnp.where(bkt == b, rb_ref[b, g * HEADS + h] * LOG2E, acc)
        o_ref[h] = acc


def _bias_tiles(rel_bias):
    return pl.pallas_call(
        _bias_kernel,
        grid=(N_GROUPS,),
        in_specs=[pl.BlockSpec(memory_space=pltpu.SMEM),
                  pl.BlockSpec((None, BLOCK, 2 * BLOCK), lambda g: (g, 0, 0))],
        out_specs=pl.BlockSpec((None, HEADS, BLOCK, 2 * BLOCK), lambda g: (g, 0, 0, 0)),
        out_shape=jax.ShapeDtypeStruct((N_GROUPS, HEADS, BLOCK, 2 * BLOCK), F32),
        compiler_params=_cparams(1),
        name="rel_bias_tiles",
    )(rel_bias, _bucket_tables())


def _attn_kernel(q_ref, kp_ref, kc_ref, vp_ref, vc_ref, bias_ref, o_ref, lse_ref):
    n = pl.program_id(2)
    col = lax.broadcasted_iota(jnp.int32, (BLOCK, 2 * BLOCK), 1)
    no_prev = jnp.logical_and(n == 0, col < BLOCK)
    lane = lax.broadcasted_iota(jnp.int32, (BLOCK, LANES), 1)
    ones = jnp.ones((2 * BLOCK, HEAD_DIM), BF16)
    for r in range(q_ref.shape[0]):
        for j in range(q_ref.shape[1] // BLOCK):
            rows = slice(j * BLOCK, (j + 1) * BLOCK)
            window = slice((j - 1) * BLOCK, (j + 1) * BLOCK)
            m_tile = jnp.zeros((BLOCK, LANES), F32)
            l_tile = jnp.ones((BLOCK, LANES), F32)
            for h in range(HEADS):
                hs = slice(h * HEAD_DIM, (h + 1) * HEAD_DIM)
                if j == 0:
                    k = jnp.concatenate([kp_ref[r, :, hs], kc_ref[r, rows, hs]], axis=0)
                    v = jnp.concatenate([vp_ref[r, :, hs], vc_ref[r, rows, hs]], axis=0)
                else:
                    k = kc_ref[r, window, hs]
                    v = vc_ref[r, window, hs]
                s = lax.dot_general(q_ref[r, rows, hs], k, (((1,), (1,)), ((), ())),
                                    preferred_element_type=F32) + bias_ref[h]
                if j == 0:
                    s = jnp.where(no_prev, NEG_INF, s)
                m = jnp.max(s, axis=-1, keepdims=True)
                p = jnp.exp2(s - m)
                o_ext = _bdot(p.astype(BF16), jnp.concatenate([v, ones], axis=1))
                o, l = o_ext[:, :HEAD_DIM], o_ext[:, HEAD_DIM:]
                o_ref[r, rows, hs] = (o / l).astype(BF16)
                m_tile = jnp.where(lane == h, m, m_tile)
                l_tile = jnp.where(lane == h, l, l_tile)
            lse_ref[r, rows, :] = m_tile * LN2 + jnp.log(l_tile)


def _attention_group(g, q, k, v, bias):
    batch, dil, sub_len, _ = q.shape
    nq = min(ATTN_QBLOCKS, sub_len // BLOCK)
    n_sub = min(dil, ATTN_QBLOCKS // nq)
    assert sub_len % (nq * BLOCK) == 0 and dil % n_sub == 0
    cur = lambda b, r, n: (b, r, n, 0)
    if sub_len == nq * BLOCK:
        prev = lambda b, r, n: (0, 0, 0, 0)
    else:
        prev = lambda b, r, n: (b, r, jnp.maximum(n * nq - 1, 0), 0)
    blk = (None, n_sub, nq * BLOCK, D)
    blk_prev = (None, n_sub, BLOCK, D)
    return pl.pallas_call(
        _attn_kernel,
        grid=(batch, dil // n_sub, sub_len // (nq * BLOCK)),
        in_specs=[
            pl.BlockSpec(blk, cur),
            pl.BlockSpec(blk_prev, prev),
            pl.BlockSpec(blk, cur),
            pl.BlockSpec(blk_prev, prev),
            pl.BlockSpec(blk, cur),
            pl.BlockSpec((None, HEADS, BLOCK, 2 * BLOCK), lambda b, r, n: (g, 0, 0, 0)),
        ],
        out_specs=[pl.BlockSpec(blk, cur),
                   pl.BlockSpec((None, n_sub, nq * BLOCK, LANES), cur)],
        out_shape=[jax.ShapeDtypeStruct((batch, dil, sub_len, D), BF16),
                   jax.ShapeDtypeStruct((batch, dil, sub_len, LANES), F32)],
        compiler_params=_cparams(3),
        name=f"dilated_attn_g{g}",
    )(q, k, k, v, v, bias)


def _mix_kernel(o0_ref, o1_ref, o2_ref, l0_ref, l1_ref, l2_ref, x_ref, mod_ref,
                expand_ref, wo_ref, ln_ref, out_ref, lse_scr, o_slab):
    tm = x_ref.shape[0]
    n_slabs = D // LANES
    lses = []
    for g, l_ref in enumerate((l0_ref, l1_ref, l2_ref)):
        dil = DIL_GROUPS[g][1]
        if dil == 1:
            lses.append(l_ref[0])
            continue
        for r in range(dil):
            lse_scr[g, pl.ds(r, tm // dil, stride=dil), :] = l_ref[r]
        lses.append(lse_scr[g])
    m = jnp.maximum(lses[0], jnp.maximum(lses[1], lses[2]))
    es = [jnp.exp(a - m) for a in lses]
    den = es[0] + es[1] + es[2]
    expand = expand_ref[...]
    mixed = None
    for g, o_ref in enumerate((o0_ref, o1_ref, o2_ref)):
        dil = DIL_GROUPS[g][1]
        w = _bdot((es[g] / den).astype(BF16), expand)
        if dil == 1:
            o = o_ref[0].astype(F32)
        else:
            for r in range(dil):
                for j in range(n_slabs):
                    o_slab[j, pl.ds(r, tm // dil, stride=dil), :] = (
                        o_ref[r, :, j * LANES:(j + 1) * LANES].astype(F32))
            o = jnp.concatenate([o_slab[j] for j in range(n_slabs)], axis=1)
        mixed = w * o if mixed is None else mixed + w * o
    y = _bdot(mixed.astype(BF16), wo_ref[...])
    g1 = mod_ref[...][2:3]
    ln = ln_ref[...]
    out_ref[...] = _layer_norm(ALPHA * x_ref[...] + (1.0 + g1) * y, ln[0:1], ln[1:2])


def _mix(outs, lses, x, mod, wo, ln):
    batch, seq, _ = x.shape
    tm = WIDE_TOKEN_TILE
    const = lambda b, i: (0, 0)
    tok = pl.BlockSpec((None, tm, D), lambda b, i: (b, i, 0))
    dilated = lambda dil, w: pl.BlockSpec((None, dil, tm // dil, w), lambda b, i: (b, 0, i, 0))
    dils = [dil for _, dil in DIL_GROUPS]
    head_of_lane = jnp.arange(D)[None, :] // HEAD_DIM
    expand = (jnp.arange(LANES)[:, None] == head_of_lane).astype(BF16)
    return pl.pallas_call(
        _mix_kernel,
        grid=(batch, seq // tm),
        in_specs=[dilated(d, D) for d in dils] + [dilated(d, LANES) for d in dils] + [
            tok,
            pl.BlockSpec((None, SUBLANES, D), lambda b, i: (b, 0, 0)),
            pl.BlockSpec((LANES, D), const),
            pl.BlockSpec((D, D), const),
            pl.BlockSpec((2, D), const)],
        out_specs=tok,
        out_shape=jax.ShapeDtypeStruct(x.shape, F32),
        scratch_shapes=[pltpu.VMEM((N_GROUPS, tm, LANES), F32),
                        pltpu.VMEM((D // LANES, tm, LANES), F32)],
        compiler_params=_cparams(2),
        name="attn_mix",
    )(*outs, *lses, x, mod, expand, wo, ln)


def kernel(x, c, ada_w, ada_b, ln_g, ln_b, conv_w_in, conv_w, conv_w_out, kv_ada_w,
           kv_ada_b, w_kv, attn_w_q, attn_w_o, rel_bias, router_w, router_bias,
           moe_w_gate, moe_w_up, moe_w_down):
    batch, seq, _ = x.shape
    assert x.shape[2] == D and batch <= SUBLANES
    assert seq % WIDE_TOKEN_TILE == 0 and seq % (BLOCK * DIL_GROUPS[-1][1]) == 0

    c8 = jnp.pad(c, ((0, SUBLANES - batch), (0, 0)))
    mods = _ada_linear(c8, ada_w, ada_b)
    kvmods = _ada_linear(c8, kv_ada_w[None], kv_ada_b[None])[0]
    mod0 = _mod_rows(mods[0], batch, 6)
    mod1 = _mod_rows(mods[1], batch, 6)
    kvmod = _mod_rows(kvmods, batch, 2)
    ln = jnp.stack([ln_g, ln_b], axis=2)

    rw = router_w.T.astype(BF16)
    rbias = router_bias.reshape(N_EXPERTS, 1)
    experts = (moe_w_gate.astype(BF16), moe_w_up.astype(BF16),
               moe_w_down.astype(BF16).reshape(DEPTH, N_EXPERTS * EXPERT_FF, D))
    conv_w8 = jnp.pad(conv_w[0], ((0, SUBLANES - CONV_WIDTH), (0, 0)))

    x = _conv_mixer(x, mod0, conv_w_in[0].astype(BF16), conv_w8,
                    conv_w_out[0].astype(BF16), ln[0, 0])
    x = _moe(0, x, mod0, rw, rbias, *experts, ln[0, 1])

    qkv = _qkv(x, mod1, kvmod, attn_w_q[0].astype(BF16), w_kv.astype(BF16))
    bias = _bias_tiles(rel_bias)
    outs, lses = zip(*[_attention_group(g, *qkv[3 * g:3 * g + 3], bias)
                       for g in range(N_GROUPS)])
    x = _mix(outs, lses, x, mod1, attn_w_o[0].astype(BF16), ln[1, 0])
    x = _moe(1, x, mod1, rw, rbias, *experts, ln[1, 1])
    return x
```

```python
import math

import jax
import jax.numpy as jnp
from jax import lax
from jax.experimental import pallas as pl
from jax.experimental.pallas import tpu as pltpu

F32 = jnp.float32
BF16 = jnp.bfloat16

D = 1024
DEPTH = 2
CONV_WIDTH = 3
DIL_GROUPS = ((128, 1), (512, 4), (2048, 16))
N_GROUPS = len(DIL_GROUPS)
HEAD_DIM = 128
HEADS = D // HEAD_DIM
QW = N_GROUPS * D
BLOCK = 128
NUM_BUCKETS = 32
MAX_DISTANCE = 2048
N_EXPERTS = 16
EXPERTS_PER_GROUP = 4
N_EXPERT_GROUPS = N_EXPERTS // EXPERTS_PER_GROUP
EXPERT_FF = D // 4
ALPHA = (2 * DEPTH) ** 0.25
LN_EPS = 1e-5
NEG_INF = -1e30
LOG2E = math.log2(math.e)

LANES = 128
SUBLANES = 8
VMEM_LIMIT = 56 * 1024 * 1024
TOKEN_TILE = 512
WIDE_TOKEN_TILE = 1024
MOE_WIN_BIG = 256
MOE_WIN_MID = 192
MOE_WIN_SMALL = 128
BF16_ROWS = 16
ATTN_QBLOCKS = 16
ADA_COL_TILE = 2048


def _cparams(n_axes):
    return pltpu.CompilerParams(
        dimension_semantics=("arbitrary",) * n_axes, vmem_limit_bytes=VMEM_LIMIT)


def _layer_norm(r, g, b):
    mu = jnp.mean(r, axis=-1, keepdims=True)
    c = r - mu
    var = jnp.mean(c * c, axis=-1, keepdims=True)
    return c * lax.rsqrt(var + LN_EPS) * g + b


def _bdot(a, b):
    return jnp.dot(a, b, preferred_element_type=F32)


def _ada_kernel(c_ref, w_ref, b_ref, o_ref):
    c = c_ref[...]
    cond = c * jax.nn.sigmoid(c)
    o_ref[...] = _bdot(cond.astype(BF16), w_ref[...].astype(BF16)) + b_ref[...]


def _ada_linear(c8, w, bias):
    n_layers, _, n = w.shape
    nt = ADA_COL_TILE
    return pl.pallas_call(
        _ada_kernel,
        grid=(n_layers, n // nt),
        in_specs=[pl.BlockSpec((SUBLANES, D), lambda l, j: (0, 0)),
                  pl.BlockSpec((None, D, nt), lambda l, j: (l, 0, j)),
                  pl.BlockSpec((None, 1, nt), lambda l, j: (l, 0, j))],
        out_specs=pl.BlockSpec((None, SUBLANES, nt), lambda l, j: (l, 0, j)),
        out_shape=jax.ShapeDtypeStruct((n_layers, SUBLANES, n), F32),
        compiler_params=_cparams(2),
        name="ada_linear",
    )(c8, w, bias.reshape(n_layers, 1, n))


def _mod_rows(mods, batch, n_vec):
    m = mods[:batch].reshape(batch, n_vec, D)
    return jnp.pad(m, ((0, 0), (0, SUBLANES - n_vec), (0, 0)))


def _conv_mixer_kernel(x_ref, mod_ref, w_in_ref, cw_ref, w_out_ref, ln_ref,
                       o_ref, u_scr):
    tm = x_ref.shape[0]
    i = pl.program_id(1)
    mod = mod_ref[...]
    sh1, sc1, g1 = mod[0:1], mod[1:2], mod[2:3]
    x = x_ref[...]
    h = (x * (1.0 + sc1) + sh1).astype(BF16)
    cgate = _bdot(h, w_in_ref[:, D:2 * D])
    v = _bdot(h, w_in_ref[:, 2 * D:3 * D])
    u = cgate * v

    @pl.when(i == 0)
    def _():
        u_scr[0:SUBLANES, :] = jnp.zeros((SUBLANES, D), F32)

    u_scr[SUBLANES:SUBLANES + tm, :] = u
    cw = cw_ref[...]
    conv = (cw[2:3] * u
            + cw[1:2] * u_scr[SUBLANES - 1:SUBLANES - 1 + tm, :]
            + cw[0:1] * u_scr[SUBLANES - 2:SUBLANES - 2 + tm, :])
    u_scr[0:SUBLANES, :] = u_scr[tm:tm + SUBLANES, :]
    bgate = _bdot(h, w_in_ref[:, 0:D])
    y = _bdot((bgate * conv).astype(BF16), w_out_ref[...])
    ln = ln_ref[...]
    o_ref[...] = _layer_norm(ALPHA * x + (1.0 + g1) * y, ln[0:1], ln[1:2])


def _conv_mixer(x, mod, w_in, conv_w, w_out, ln):
    batch, seq, _ = x.shape
    tm = WIDE_TOKEN_TILE
    const = lambda b, i: (0, 0)
    return pl.pallas_call(
        _conv_mixer_kernel,
        grid=(batch, seq // tm),
        in_specs=[pl.BlockSpec((None, tm, D), lambda b, i: (b, i, 0)),
                  pl.BlockSpec((None, SUBLANES, D), lambda b, i: (b, 0, 0)),
                  pl.BlockSpec((D, 3 * D), const),
                  pl.BlockSpec((SUBLANES, D), const),
                  pl.BlockSpec((D, D), const),
                  pl.BlockSpec((2, D), const)],
        out_specs=pl.BlockSpec((None, tm, D), lambda b, i: (b, i, 0)),
        out_shape=jax.ShapeDtypeStruct(x.shape, F32),
        scratch_shapes=[pltpu.VMEM((tm + SUBLANES, D), F32)],
        compiler_params=_cparams(2),
        name="conv_mixer",
    )(x, mod, w_in, conv_w, w_out, ln)


def _route(logits_t, rbias_col):
    aff = jax.nn.sigmoid(logits_t)
    sel = aff + rbias_col
    rows = [sel[e:e + 1, :] for e in range(N_EXPERTS)]
    scores = []
    for g in range(N_EXPERT_GROUPS):
        s0, s1, s2, s3 = rows[EXPERTS_PER_GROUP * g:EXPERTS_PER_GROUP * (g + 1)]
        a, b = jnp.maximum(s0, s1), jnp.minimum(s0, s1)
        c, d = jnp.maximum(s2, s3), jnp.minimum(s2, s3)
        scores.append(jnp.maximum(a, c) + jnp.maximum(jnp.minimum(a, c), jnp.maximum(b, d)))
    best = scores[0]
    best_group = jnp.zeros(best.shape, jnp.int32)
    for g in range(1, N_EXPERT_GROUPS):
        upd = scores[g] > best
        best_group = jnp.where(upd, g, best_group)
        best = jnp.where(upd, scores[g], best)
    e_iota = lax.broadcasted_iota(jnp.int32, sel.shape, 0)
    masked = jnp.where((e_iota // EXPERTS_PER_GROUP) == best_group, sel, NEG_INF)
    m1 = jnp.max(masked, axis=0, keepdims=True)
    idx1 = jnp.min(jnp.where(masked == m1, e_iota, N_EXPERTS), axis=0, keepdims=True)
    masked2 = jnp.where(e_iota == idx1, -jnp.inf, masked)
    m2 = jnp.max(masked2, axis=0, keepdims=True)
    idx2 = jnp.min(jnp.where(masked2 == m2, e_iota, N_EXPERTS), axis=0, keepdims=True)
    a1 = jnp.sum(jnp.where(e_iota == idx1, aff, 0.0), axis=0, keepdims=True)
    a2 = jnp.sum(jnp.where(e_iota == idx2, aff, 0.0), axis=0, keepdims=True)
    den = a1 + a2
    comb = (jnp.where(e_iota == idx1, a1 / den, 0.0)
            + jnp.where(e_iota == idx2, a2 / den, 0.0))
    group1 = idx1 // EXPERTS_PER_GROUP
    split = group1 != idx2 // EXPERTS_PER_GROUP
    return comb, group1, split


def _moe_kernel(x_ref, xp_ref, mod_ref, modp_ref, rw_ref, rb_ref, tri_ref, wg_ref, wu_ref, wd_ref,
                ln_ref, o_ref, hs_scr, cs_scr, ys_scr, pt_scr):
    tm = x_ref.shape[0]
    gff = EXPERTS_PER_GROUP * EXPERT_FF
    t = pl.program_id(0)

    @pl.when(t == 0)
    def _():
        ys_scr[...] = jnp.zeros(ys_scr.shape, F32)
        pt_scr[...] = jnp.zeros(pt_scr.shape, BF16)
        hs_scr[tm:, :] = jnp.zeros((MOE_WIN_BIG, D), BF16)
        cs_scr[tm:, :] = jnp.zeros((MOE_WIN_BIG, LANES), F32)

    y = _bdot(pt_scr[...], ys_scr[0:tm, :].astype(BF16))
    ln = ln_ref[...]
    o_ref[...] = _layer_norm(ALPHA * xp_ref[...] + (1.0 + modp_ref[...][5:6]) * y,
                             ln[0:1], ln[1:2])

    mod = mod_ref[...]
    sh2, sc2 = mod[3:4], mod[4:5]
    h = x_ref[...] * (1.0 + sc2) + sh2
    hb = h.astype(BF16)
    logits_t = lax.dot_general(rw_ref[...], hb, (((1,), (1,)), ((), ())),
                               preferred_element_type=F32)
    comb_t, group, split = _route(logits_t, rb_ref[...])

    g_iota = lax.broadcasted_iota(jnp.int32, (SUBLANES, tm), 0)
    member = jnp.where(g_iota == group, 1.0, 0.0)
    rank = _bdot(member.astype(BF16), tri_ref[...])
    counts = jnp.sum(member, axis=1, keepdims=True)
    cnt = [counts[g, 0] for g in range(N_EXPERT_GROUPS)]
    start = [0.0]
    for g in range(N_EXPERT_GROUPS - 1):
        start.append(start[g] + cnt[g])
    dest = sum(member[g:g + 1] * (start[g] + rank[g:g + 1]) for g in range(N_EXPERT_GROUPS))
    row = lax.broadcasted_iota(jnp.int32, (tm, tm), 0).astype(F32)
    col = lax.broadcasted_iota(jnp.int32, (tm, tm), 1).astype(F32)
    perm = jnp.where(row == dest, 1.0, 0.0).astype(BF16)
    dest_col = jnp.broadcast_to(dest, (LANES, tm)).T
    dest_col = jnp.concatenate([dest_col] * (tm // LANES), axis=1)
    perm_t = jnp.where(dest_col == col, 1.0, 0.0).astype(BF16)
    pt_scr[...] = perm_t

    hs_scr[0:tm, :] = _bdot(perm, hb).astype(BF16)
    comb_hi = comb_t.astype(BF16)
    comb_lo = (comb_t - comb_hi.astype(F32)).astype(BF16)
    cs = _bdot(jnp.concatenate([comb_hi, comb_lo], axis=0), perm_t)
    cs = cs[0:N_EXPERTS] + cs[N_EXPERTS:2 * N_EXPERTS]
    cs_scr[0:tm, :] = jnp.concatenate(
        [cs, jnp.zeros((LANES - N_EXPERTS, tm), F32)], axis=0).T
    ys_scr[0:tm, :] = jnp.zeros((tm, D), F32)

    any_split = jnp.max(jnp.where(split, 1.0, 0.0)) > 0.0
    active = t < pl.num_programs(0) - 1
    for g in range(N_EXPERT_GROUPS):
        first = jnp.where(any_split, 0, jnp.asarray(start[g], F32).astype(jnp.int32))
        last = jnp.where(any_split, tm, (start[g] + cnt[g]).astype(jnp.int32))
        first = (first // BF16_ROWS) * BF16_ROWS
        need = jnp.where(jnp.logical_and(active, last > first), last - first, 0)
        n_full = need // MOE_WIN_BIG
        rest = need - n_full * MOE_WIN_BIG
        n_big = n_full + jnp.where(rest > MOE_WIN_MID, 1, 0)
        n_mid = jnp.where(jnp.logical_and(rest > MOE_WIN_SMALL, rest <= MOE_WIN_MID), 1, 0)
        n_small = jnp.where(jnp.logical_and(rest > 0, rest <= MOE_WIN_SMALL), 1, 0)

        def window(start_row, size, g=g):
            rows = pl.ds(pl.multiple_of(start_row, BF16_ROWS), size)
            hs = hs_scr[rows, :]
            parts = []
            for j in range(EXPERTS_PER_GROUP):
                e = g * EXPERTS_PER_GROUP + j
                gate = _bdot(hs, wg_ref[e])
                up = _bdot(hs, wu_ref[e])
                he = gate * jax.nn.sigmoid(gate) * up * cs_scr[rows, e:e + 1]
                parts.append(he.astype(BF16))
            he = jnp.concatenate(parts, axis=1)
            ys_scr[rows, :] += _bdot(he, wd_ref[g * gff:(g + 1) * gff, :])

        def big(w, carry, first=first, window=window):
            window(first + w * MOE_WIN_BIG, MOE_WIN_BIG)
            return carry

        def mid(w, carry, first=first, n_full=n_full, window=window):
            window(first + n_full * MOE_WIN_BIG, MOE_WIN_MID)
            return carry

        def small(w, carry, first=first, n_full=n_full, window=window):
            window(first + n_full * MOE_WIN_BIG, MOE_WIN_SMALL)
            return carry

        lax.fori_loop(0, n_big, big, 0)
        lax.fori_loop(0, n_mid, mid, 0)
        lax.fori_loop(0, n_small, small, 0)


def _moe(layer, x, mod, rw, rbias, wg, wu, wd, ln):
    batch, seq, _ = x.shape
    tm = TOKEN_TILE
    layer_weights = lambda shape: pl.BlockSpec(
        (None,) + shape, lambda t: (layer,) + (0,) * len(shape), pipeline_mode=pl.Buffered(1))
    per_batch = seq // tm
    n_tiles = batch * per_batch
    const2 = lambda t: (0, 0)
    cur = lambda t: jnp.minimum(t, n_tiles - 1)
    prev = lambda t: jnp.maximum(t - 1, 0)
    tile = lambda which: pl.BlockSpec(
        (None, tm, D), lambda t: (which(t) // per_batch, which(t) % per_batch, 0))
    mods = lambda which: pl.BlockSpec(
        (None, SUBLANES, D), lambda t: (which(t) // per_batch, 0, 0))
    tri = (jnp.arange(tm)[:, None] < jnp.arange(tm)[None, :]).astype(BF16)
    return pl.pallas_call(
        _moe_kernel,
        grid=(n_tiles + 1,),
        in_specs=[tile(cur), tile(prev), mods(cur), mods(prev),
                  pl.BlockSpec((N_EXPERTS, D), const2),
                  pl.BlockSpec((N_EXPERTS, 1), const2),
                  pl.BlockSpec((tm, tm), const2),
                  layer_weights((N_EXPERTS, D, EXPERT_FF)),
                  layer_weights((N_EXPERTS, D, EXPERT_FF)),
                  layer_weights((N_EXPERTS * EXPERT_FF, D)),
                  pl.BlockSpec((2, D), const2)],
        out_specs=tile(prev),
        out_shape=jax.ShapeDtypeStruct(x.shape, F32),
        scratch_shapes=[pltpu.VMEM((tm + MOE_WIN_BIG, D), BF16),
                        pltpu.VMEM((tm + MOE_WIN_BIG, LANES), F32),
                        pltpu.VMEM((tm + MOE_WIN_BIG, D), F32),
                        pltpu.VMEM((tm, tm), BF16)],
        compiler_params=_cparams(1),
        name="moe",
    )(x, x, mod, mod, rw, rbias, tri, wg, wu, wd, ln)


def _qkv_kernel(x_ref, mod_ref, kvmod_ref, wq_ref, wkv_ref, *refs):
    outs, (slab, hq_scr, hkv_scr) = refs[:3 * N_GROUPS], refs[3 * N_GROUPS:]
    tm = x_ref.shape[0]
    mod = mod_ref[...]
    kvmod = kvmod_ref[...]
    q_sc, q_sh = 1.0 + mod[1:2], mod[0:1]
    kv_sc, kv_sh = 1.0 + kvmod[1:2], kvmod[0:1]
    n_slabs = D // LANES
    for j in range(n_slabs):
        slab[j] = x_ref[:, j * LANES:(j + 1) * LANES]
    scale = HEAD_DIM ** -0.5 * LOG2E
    for g, (_, dil) in enumerate(DIL_GROUPS):
        n = tm // dil
        for r in range(dil):
            rows = slice(r * n, (r + 1) * n)
            for j in range(n_slabs):
                ls = slice(j * LANES, (j + 1) * LANES)
                piece = slab[j] if dil == 1 else slab[j, pl.ds(r, n, stride=dil), :]
                hq_scr[rows, ls] = (piece * q_sc[:, ls] + q_sh[:, ls]).astype(BF16)
                hkv_scr[rows, ls] = (piece * kv_sc[:, ls] + kv_sh[:, ls]).astype(BF16)
        hq = hq_scr[...]
        hkv = hkv_scr[...]
        cols = slice(g * D, (g + 1) * D)
        vcols = slice(QW + g * D, QW + (g + 1) * D)
        q = (_bdot(hq, wq_ref[:, cols]) * scale).astype(BF16)
        k = _bdot(hkv, wkv_ref[:, cols]).astype(BF16)
        v = _bdot(hkv, wkv_ref[:, vcols]).astype(BF16)
        for r in range(dil):
            rows = slice(r * n, (r + 1) * n)
            outs[3 * g][r] = q[rows]
            outs[3 * g + 1][r] = k[rows]
            outs[3 * g + 2][r] = v[rows]


def _qkv(x, mod, kvmod, wq, wkv):
    batch, seq, _ = x.shape
    tm = TOKEN_TILE
    const = lambda b, i: (0, 0)
    out_specs, out_shape = [], []
    for _, dil in DIL_GROUPS:
        for _ in range(3):
            out_specs.append(pl.BlockSpec((None, dil, tm // dil, D), lambda b, i: (b, 0, i, 0)))
            out_shape.append(jax.ShapeDtypeStruct((batch, dil, seq // dil, D), BF16))
    return pl.pallas_call(
        _qkv_kernel,
        grid=(batch, seq // tm),
        in_specs=[pl.BlockSpec((None, tm, D), lambda b, i: (b, i, 0)),
                  pl.BlockSpec((None, SUBLANES, D), lambda b, i: (b, 0, 0)),
                  pl.BlockSpec((None, SUBLANES, D), lambda b, i: (b, 0, 0)),
                  pl.BlockSpec((D, QW), const),
                  pl.BlockSpec((D, 2 * QW), const)],
        out_specs=out_specs,
        out_shape=out_shape,
        scratch_shapes=[pltpu.VMEM((D // LANES, tm, LANES), F32),
                        pltpu.VMEM((tm, D), BF16),
                        pltpu.VMEM((tm, D), BF16)],
        compiler_params=_cparams(2),
        name="qkv_proj",
    )(x, mod, kvmod, wq, wkv)


def _bucket_tables():
    qi = jnp.arange(BLOCK)[:, None]
    kj = jnp.arange(2 * BLOCK)[None, :]
    dist = BLOCK + qi - kj
    max_exact = NUM_BUCKETS // 2
    tables = []
    for window, dil in DIL_GROUPS:
        n = jnp.maximum(dist, 0) * dil
        nf = jnp.maximum(n, 1).astype(F32)
        large = max_exact + (jnp.log(nf / max_exact) / math.log(MAX_DISTANCE / max_exact)
                             * (NUM_BUCKETS - max_exact)).astype(jnp.int32)
        large = jnp.minimum(large, NUM_BUCKETS - 1)
        bucket = jnp.where(n < max_exact, n, large)
        valid = (dist >= 0) & (dist <= window // dil)
        tables.append(jnp.where(valid, bucket, -1).astype(jnp.int32))
    return jnp.stack(tables)


def _bias_kernel(rb_ref, bkt_ref, o_ref):
    g = pl.program_id(0)
    bkt = bkt_ref[...]
    for h in range(HEADS):
        acc = jnp.full(bkt.shape, NEG_INF, F32)
        for b in range(NUM_BUCKETS):
            acc = jnp.where(bkt == b, rb_ref[b, g * HEADS + h] * LOG2E, acc)
        o_ref[h] = acc


def _bias_tiles(rel_bias):
    return pl.pallas_call(
        _bias_kernel,
        grid=(N_GROUPS,),
        in_specs=[pl.BlockSpec(memory_space=pltpu.SMEM),
                  pl.BlockSpec((None, BLOCK, 2 * BLOCK), lambda g: (g, 0, 0))],
        out_specs=pl.BlockSpec((None, HEADS, BLOCK, 2 * BLOCK), lambda g: (g, 0, 0, 0)),
        out_shape=jax.ShapeDtypeStruct((N_GROUPS, HEADS, BLOCK, 2 * BLOCK), F32),
        compiler_params=_cparams(1),
        name="rel_bias_tiles",
    )(rel_bias, _bucket_tables())


def _attn_kernel(q_ref, kp_ref, kc_ref, vp_ref, vc_ref, bias_ref, o_ref, stat_ref):
    n = pl.program_id(2)
    col = lax.broadcasted_iota(jnp.int32, (BLOCK, 2 * BLOCK), 1)
    no_prev = jnp.logical_and(n == 0, col < BLOCK)
    lane = lax.broadcasted_iota(jnp.int32, (BLOCK, LANES), 1)
    ones = jnp.ones((2 * BLOCK, HEAD_DIM), BF16)
    for r in range(q_ref.shape[0]):
        for j in range(q_ref.shape[1] // BLOCK):
            rows = slice(j * BLOCK, (j + 1) * BLOCK)
            window = slice((j - 1) * BLOCK, (j + 1) * BLOCK)
            stat = jnp.zeros((BLOCK, LANES), F32)
            for h in range(HEADS):
                hs = slice(h * HEAD_DIM, (h + 1) * HEAD_DIM)
                if j == 0:
                    k = jnp.concatenate([kp_ref[r, :, hs], kc_ref[r, rows, hs]], axis=0)
                    v = jnp.concatenate([vp_ref[r, :, hs], vc_ref[r, rows, hs]], axis=0)
                else:
                    k = kc_ref[r, window, hs]
                    v = vc_ref[r, window, hs]
                s = lax.dot_general(q_ref[r, rows, hs], k, (((1,), (1,)), ((), ())),
                                    preferred_element_type=F32) + bias_ref[h]
                if j == 0:
                    s = jnp.where(no_prev, NEG_INF, s)
                m = jnp.max(s, axis=-1, keepdims=True)
                p = jnp.exp2(s - m)
                o_ext = _bdot(p.astype(BF16), jnp.concatenate([v, ones], axis=1))
                o, l = o_ext[:, :HEAD_DIM], o_ext[:, HEAD_DIM:]
                o_ref[r, rows, hs] = o.astype(BF16)
                stat = jnp.where(lane == h, m, jnp.where(lane == HEADS + h, l, stat))
            stat_ref[r, rows, :] = stat


def _attention_group(g, q, k, v, bias):
    batch, dil, sub_len, _ = q.shape
    nq = min(ATTN_QBLOCKS, sub_len // BLOCK)
    n_sub = min(dil, ATTN_QBLOCKS // nq)
    assert sub_len % (nq * BLOCK) == 0 and dil % n_sub == 0
    cur = lambda b, r, n: (b, r, n, 0)
    if sub_len == nq * BLOCK:
        prev = lambda b, r, n: (0, 0, 0, 0)
    else:
        prev = lambda b, r, n: (b, r, jnp.maximum(n * nq - 1, 0), 0)
    blk = (None, n_sub, nq * BLOCK, D)
    blk_prev = (None, n_sub, BLOCK, D)
    return pl.pallas_call(
        _attn_kernel,
        grid=(batch, dil // n_sub, sub_len // (nq * BLOCK)),
        in_specs=[
            pl.BlockSpec(blk, cur),
            pl.BlockSpec(blk_prev, prev),
            pl.BlockSpec(blk, cur),
            pl.BlockSpec(blk_prev, prev),
            pl.BlockSpec(blk, cur),
            pl.BlockSpec((None, HEADS, BLOCK, 2 * BLOCK), lambda b, r, n: (g, 0, 0, 0)),
        ],
        out_specs=[pl.BlockSpec(blk, cur),
                   pl.BlockSpec((None, n_sub, nq * BLOCK, LANES), cur)],
        out_shape=[jax.ShapeDtypeStruct((batch, dil, sub_len, D), BF16),
                   jax.ShapeDtypeStruct((batch, dil, sub_len, LANES), F32)],
        compiler_params=_cparams(3),
        name=f"dilated_attn_g{g}",
    )(q, k, k, v, v, bias)


def _mix_kernel(o0_ref, o1_ref, o2_ref, s0_ref, s1_ref, s2_ref, x_ref, mod_ref,
                expand_ref, wo_ref, ln_ref, out_ref, stat_scr, o_slab):
    tm = x_ref.shape[0]
    n_slabs = D // LANES
    stats = []
    for g, s_ref in enumerate((s0_ref, s1_ref, s2_ref)):
        dil = DIL_GROUPS[g][1]
        if dil == 1:
            stats.append(s_ref[0])
            continue
        for r in range(dil):
            stat_scr[g, pl.ds(r, tm // dil, stride=dil), :] = s_ref[r]
        stats.append(stat_scr[g])
    sums = [pltpu.roll(s, LANES - HEADS, axis=1) for s in stats]
    top = jnp.maximum(stats[0], jnp.maximum(stats[1], stats[2]))
    es = [jnp.exp2(s - top) for s in stats]
    den = es[0] * sums[0] + es[1] * sums[1] + es[2] * sums[2]
    head_lane = lax.broadcasted_iota(jnp.int32, top.shape, 1) < HEADS
    expand = expand_ref[...]
    mixed = None
    for g, o_ref in enumerate((o0_ref, o1_ref, o2_ref)):
        dil = DIL_GROUPS[g][1]
        w = jnp.where(head_lane, es[g] / den, 0.0).astype(BF16)
        w = _bdot(w, expand)
        if dil == 1:
            o = o_ref[0].astype(F32)
        else:
            for r in range(dil):
                for j in range(n_slabs):
                    o_slab[j, pl.ds(r, tm // dil, stride=dil), :] = (
                        o_ref[r, :, j * LANES:(j + 1) * LANES].astype(F32))
            o = jnp.concatenate([o_slab[j] for j in range(n_slabs)], axis=1)
        mixed = w * o if mixed is None else mixed + w * o
    y = _bdot(mixed.astype(BF16), wo_ref[...])
    g1 = mod_ref[...][2:3]
    ln = ln_ref[...]
    out_ref[...] = _layer_norm(ALPHA * x_ref[...] + (1.0 + g1) * y, ln[0:1], ln[1:2])


def _mix(outs, stats, x, mod, wo, ln):
    batch, seq, _ = x.shape
    tm = WIDE_TOKEN_TILE
    const = lambda b, i: (0, 0)
    tok = pl.BlockSpec((None, tm, D), lambda b, i: (b, i, 0))
    dilated = lambda dil, w: pl.BlockSpec((None, dil, tm // dil, w), lambda b, i: (b, 0, i, 0))
    dils = [dil for _, dil in DIL_GROUPS]
    head_of_lane = jnp.arange(D)[None, :] // HEAD_DIM
    expand = (jnp.arange(LANES)[:, None] == head_of_lane).astype(BF16)
    return pl.pallas_call(
        _mix_kernel,
        grid=(batch, seq // tm),
        in_specs=[dilated(d, D) for d in dils] + [dilated(d, LANES) for d in dils] + [
            tok,
            pl.BlockSpec((None, SUBLANES, D), lambda b, i: (b, 0, 0)),
            pl.BlockSpec((LANES, D), const),
            pl.BlockSpec((D, D), const),
            pl.BlockSpec((2, D), const)],
        out_specs=tok,
        out_shape=jax.ShapeDtypeStruct(x.shape, F32),
        scratch_shapes=[pltpu.VMEM((N_GROUPS, tm, LANES), F32),
                        pltpu.VMEM((D // LANES, tm, LANES), F32)],
        compiler_params=_cparams(2),
        name="attn_mix",
    )(*outs, *stats, x, mod, expand, wo, ln)


def kernel(x, c, ada_w, ada_b, ln_g, ln_b, conv_w_in, conv_w, conv_w_out, kv_ada_w,
           kv_ada_b, w_kv, attn_w_q, attn_w_o, rel_bias, router_w, router_bias,
           moe_w_gate, moe_w_up, moe_w_down):
    batch, seq, _ = x.shape
    assert x.shape[2] == D and batch <= SUBLANES
    assert seq % WIDE_TOKEN_TILE == 0 and seq % (BLOCK * DIL_GROUPS[-1][1]) == 0

    c8 = jnp.pad(c, ((0, SUBLANES - batch), (0, 0)))
    mods = _ada_linear(c8, ada_w, ada_b)
    kvmods = _ada_linear(c8, kv_ada_w[None], kv_ada_b[None])[0]
    mod0 = _mod_rows(mods[0], batch, 6)
    mod1 = _mod_rows(mods[1], batch, 6)
    kvmod = _mod_rows(kvmods, batch, 2)
    ln = jnp.stack([ln_g, ln_b], axis=2)

    rw = router_w.T.astype(BF16)
    rbias = router_bias.reshape(N_EXPERTS, 1)
    experts = (moe_w_gate.astype(BF16), moe_w_up.astype(BF16),
               moe_w_down.astype(BF16).reshape(DEPTH, N_EXPERTS * EXPERT_FF, D))
    conv_w8 = jnp.pad(conv_w[0], ((0, SUBLANES - CONV_WIDTH), (0, 0)))

    x = _conv_mixer(x, mod0, conv_w_in[0].astype(BF16), conv_w8,
                    conv_w_out[0].astype(BF16), ln[0, 0])
    x = _moe(0, x, mod0, rw, rbias, *experts, ln[0, 1])

    qkv = _qkv(x, mod1, kvmod, attn_w_q[0].astype(BF16), w_kv.astype(BF16))
    bias = _bias_tiles(rel_bias)
    outs, stats = zip(*[_attention_group(g, *qkv[3 * g:3 * g + 3], bias)
                       for g in range(N_GROUPS)])
    x = _mix(outs, stats, x, mod1, attn_w_o[0].astype(BF16), ln[1, 0])
    x = _moe(1, x, mod1, rw, rbias, *experts, ln[1, 1])
    return x
```

```python
import functools
import math

import jax
import jax.numpy as jnp
from jax import lax
from jax.experimental import pallas as pl
from jax.experimental.pallas import tpu as pltpu

F32 = jnp.float32
BF16 = jnp.bfloat16

D = 1024
DEPTH = 2
CONV_WIDTH = 3
DIL_GROUPS = ((128, 1), (512, 4), (2048, 16))
N_GROUPS = len(DIL_GROUPS)
HEAD_DIM = 128
HEADS = D // HEAD_DIM
QW = N_GROUPS * D
BLOCK = 128
NUM_BUCKETS = 32
MAX_DISTANCE = 2048
N_EXPERTS = 16
EXPERTS_PER_GROUP = 4
N_EXPERT_GROUPS = N_EXPERTS // EXPERTS_PER_GROUP
EXPERT_FF = D // 4
ALPHA = (2 * DEPTH) ** 0.25
LN_EPS = 1e-5
NEG_INF = -1e30
LOG2E = math.log2(math.e)

LANES = 128
SUBLANES = 8
VMEM_LIMIT = 56 * 1024 * 1024
FUSED_VMEM_LIMIT = 60 * 1024 * 1024
TOKEN_TILE = 512
WIDE_TOKEN_TILE = 1024
MOE_WIN_BIG = 256
MOE_WIN_MID = 192
MOE_WIN_SMALL = 128
BF16_ROWS = 16
ATTN_QBLOCKS = 16
ADA_COL_TILE = 2048


def _cparams(n_axes):
    return pltpu.CompilerParams(
        dimension_semantics=("arbitrary",) * n_axes, vmem_limit_bytes=VMEM_LIMIT)


def _layer_norm(r, g, b):
    mu = jnp.mean(r, axis=-1, keepdims=True)
    c = r - mu
    var = jnp.mean(c * c, axis=-1, keepdims=True)
    return c * lax.rsqrt(var + LN_EPS) * g + b


def _bdot(a, b):
    return jnp.dot(a, b, preferred_element_type=F32)


def _ada_kernel(c_ref, w_ref, b_ref, o_ref):
    c = c_ref[...]
    cond = c * jax.nn.sigmoid(c)
    o_ref[...] = _bdot(cond.astype(BF16), w_ref[...].astype(BF16)) + b_ref[...]


def _ada_linear(c8, w, bias):
    n_layers, _, n = w.shape
    nt = ADA_COL_TILE
    return pl.pallas_call(
        _ada_kernel,
        grid=(n_layers, n // nt),
        in_specs=[pl.BlockSpec((SUBLANES, D), lambda l, j: (0, 0)),
                  pl.BlockSpec((None, D, nt), lambda l, j: (l, 0, j)),
                  pl.BlockSpec((None, 1, nt), lambda l, j: (l, 0, j))],
        out_specs=pl.BlockSpec((None, SUBLANES, nt), lambda l, j: (l, 0, j)),
        out_shape=jax.ShapeDtypeStruct((n_layers, SUBLANES, n), F32),
        compiler_params=_cparams(2),
        name="ada_linear",
    )(c8, w, bias.reshape(n_layers, 1, n))


def _mod_rows(mods, batch, n_vec):
    m = mods[:batch].reshape(batch, n_vec, D)
    return jnp.pad(m, ((0, 0), (0, SUBLANES - n_vec), (0, 0)))


def _conv_residual_ln(x, mod, seq_start, w_in_ref, cw_ref, w_out_ref, ln, u_scr):
    tm = x.shape[0]
    sh1, sc1, g1 = mod[0:1], mod[1:2], mod[2:3]
    h = (x * (1.0 + sc1) + sh1).astype(BF16)
    cgate = _bdot(h, w_in_ref[:, D:2 * D])
    v = _bdot(h, w_in_ref[:, 2 * D:3 * D])
    u = cgate * v
    u_scr[0:SUBLANES, :] = jnp.where(seq_start, 0.0, u_scr[0:SUBLANES, :])
    u_scr[SUBLANES:SUBLANES + tm, :] = u
    cw = cw_ref[...]
    conv = (cw[2:3] * u
            + cw[1:2] * u_scr[SUBLANES - 1:SUBLANES - 1 + tm, :]
            + cw[0:1] * u_scr[SUBLANES - 2:SUBLANES - 2 + tm, :])
    u_scr[0:SUBLANES, :] = u_scr[tm:tm + SUBLANES, :]
    bgate = _bdot(h, w_in_ref[:, 0:D])
    y = _bdot((bgate * conv).astype(BF16), w_out_ref[...])
    return _layer_norm(ALPHA * x + (1.0 + g1) * y, ln[0:1], ln[1:2])


def _route(logits_t, rbias_col):
    aff = jax.nn.sigmoid(logits_t)
    sel = aff + rbias_col
    rows = [sel[e:e + 1, :] for e in range(N_EXPERTS)]
    scores = []
    for g in range(N_EXPERT_GROUPS):
        s0, s1, s2, s3 = rows[EXPERTS_PER_GROUP * g:EXPERTS_PER_GROUP * (g + 1)]
        a, b = jnp.maximum(s0, s1), jnp.minimum(s0, s1)
        c, d = jnp.maximum(s2, s3), jnp.minimum(s2, s3)
        scores.append(jnp.maximum(a, c) + jnp.maximum(jnp.minimum(a, c), jnp.maximum(b, d)))
    best = scores[0]
    best_group = jnp.zeros(best.shape, jnp.int32)
    for g in range(1, N_EXPERT_GROUPS):
        upd = scores[g] > best
        best_group = jnp.where(upd, g, best_group)
        best = jnp.where(upd, scores[g], best)
    e_iota = lax.broadcasted_iota(jnp.int32, sel.shape, 0)
    masked = jnp.where((e_iota // EXPERTS_PER_GROUP) == best_group, sel, NEG_INF)
    m1 = jnp.max(masked, axis=0, keepdims=True)
    idx1 = jnp.min(jnp.where(masked == m1, e_iota, N_EXPERTS), axis=0, keepdims=True)
    masked2 = jnp.where(e_iota == idx1, -jnp.inf, masked)
    m2 = jnp.max(masked2, axis=0, keepdims=True)
    idx2 = jnp.min(jnp.where(masked2 == m2, e_iota, N_EXPERTS), axis=0, keepdims=True)
    a1 = jnp.sum(jnp.where(e_iota == idx1, aff, 0.0), axis=0, keepdims=True)
    a2 = jnp.sum(jnp.where(e_iota == idx2, aff, 0.0), axis=0, keepdims=True)
    den = a1 + a2
    comb = (jnp.where(e_iota == idx1, a1 / den, 0.0)
            + jnp.where(e_iota == idx2, a2 / den, 0.0))
    group1 = idx1 // EXPERTS_PER_GROUP
    split = group1 != idx2 // EXPERTS_PER_GROUP
    return comb, group1, split


def _moe_init(hs_scr, cs_scr, ys_scr, pt_scr):
    tm = pt_scr.shape[0]
    ys_scr[...] = jnp.zeros(ys_scr.shape, F32)
    pt_scr[...] = jnp.zeros(pt_scr.shape, BF16)
    hs_scr[tm:, :] = jnp.zeros((MOE_WIN_BIG, D), BF16)
    cs_scr[tm:, :] = jnp.zeros((MOE_WIN_BIG, LANES), F32)


def _moe_finish(x, gain, ln, ys_scr, pt_scr):
    tm = pt_scr.shape[0]
    y = _bdot(pt_scr[...], ys_scr[0:tm, :].astype(BF16))
    return _layer_norm(ALPHA * x + gain * y, ln[0:1], ln[1:2])


def _moe_start(x, mod, rw_ref, rb_ref, tri_ref, hs_scr, cs_scr, ys_scr, pt_scr):
    tm = x.shape[0]
    sh2, sc2 = mod[3:4], mod[4:5]
    h = x * (1.0 + sc2) + sh2
    hb = h.astype(BF16)
    logits_t = lax.dot_general(rw_ref[...], hb, (((1,), (1,)), ((), ())),
                               preferred_element_type=F32)
    comb_t, group, split = _route(logits_t, rb_ref[...])

    g_iota = lax.broadcasted_iota(jnp.int32, (SUBLANES, tm), 0)
    member = jnp.where(g_iota == group, 1.0, 0.0)
    rank = _bdot(member.astype(BF16), tri_ref[...])
    counts = jnp.sum(member, axis=1, keepdims=True)
    cnt = [counts[g, 0] for g in range(N_EXPERT_GROUPS)]
    start = [0.0]
    for g in range(N_EXPERT_GROUPS - 1):
        start.append(start[g] + cnt[g])
    dest = sum(member[g:g + 1] * (start[g] + rank[g:g + 1]) for g in range(N_EXPERT_GROUPS))
    row = lax.broadcasted_iota(jnp.int32, (tm, tm), 0).astype(F32)
    col = lax.broadcasted_iota(jnp.int32, (tm, tm), 1).astype(F32)
    perm = jnp.where(row == dest, 1.0, 0.0).astype(BF16)
    dest_col = jnp.broadcast_to(dest, (LANES, tm)).T
    dest_col = jnp.concatenate([dest_col] * (tm // LANES), axis=1)
    perm_t = jnp.where(dest_col == col, 1.0, 0.0).astype(BF16)
    pt_scr[...] = perm_t

    hs_scr[0:tm, :] = _bdot(perm, hb).astype(BF16)
    comb_hi = comb_t.astype(BF16)
    comb_lo = (comb_t - comb_hi.astype(F32)).astype(BF16)
    cs = _bdot(jnp.concatenate([comb_hi, comb_lo], axis=0), perm_t)
    cs = cs[0:N_EXPERTS] + cs[N_EXPERTS:2 * N_EXPERTS]
    cs_scr[0:tm, :] = jnp.concatenate(
        [cs, jnp.zeros((LANES - N_EXPERTS, tm), F32)], axis=0).T
    ys_scr[0:tm, :] = jnp.zeros((tm, D), F32)

    any_split = jnp.max(jnp.where(split, 1.0, 0.0)) > 0.0
    return start, cnt, any_split


def _moe_windows(active, start, cnt, any_split, wg_ref, wu_ref, wd_ref, hs_scr, cs_scr, ys_scr):
    tm = hs_scr.shape[0] - MOE_WIN_BIG
    gff = EXPERTS_PER_GROUP * EXPERT_FF
    for g in range(N_EXPERT_GROUPS):
        first = jnp.where(any_split, 0, jnp.asarray(start[g], F32).astype(jnp.int32))
        last = jnp.where(any_split, tm, (start[g] + cnt[g]).astype(jnp.int32))
        first = (first // BF16_ROWS) * BF16_ROWS
        need = jnp.where(jnp.logical_and(active, last > first), last - first, 0)
        n_full = need // MOE_WIN_BIG
        rest = need - n_full * MOE_WIN_BIG
        n_big = n_full + jnp.where(rest > MOE_WIN_MID, 1, 0)
        n_mid = jnp.where(jnp.logical_and(rest > MOE_WIN_SMALL, rest <= MOE_WIN_MID), 1, 0)
        n_small = jnp.where(jnp.logical_and(rest > 0, rest <= MOE_WIN_SMALL), 1, 0)

        def window(start_row, size, g=g):
            rows = pl.ds(pl.multiple_of(start_row, BF16_ROWS), size)
            hs = hs_scr[rows, :]
            parts = []
            for j in range(EXPERTS_PER_GROUP):
                e = g * EXPERTS_PER_GROUP + j
                gate = _bdot(hs, wg_ref[e])
                up = _bdot(hs, wu_ref[e])
                he = gate * jax.nn.sigmoid(gate) * up * cs_scr[rows, e:e + 1]
                parts.append(he.astype(BF16))
            he = jnp.concatenate(parts, axis=1)
            ys_scr[rows, :] += _bdot(he, wd_ref[g * gff:(g + 1) * gff, :])

        def big(w, carry, first=first, window=window):
            window(first + w * MOE_WIN_BIG, MOE_WIN_BIG)
            return carry

        def mid(w, carry, first=first, n_full=n_full, window=window):
            window(first + n_full * MOE_WIN_BIG, MOE_WIN_MID)
            return carry

        def small(w, carry, first=first, n_full=n_full, window=window):
            window(first + n_full * MOE_WIN_BIG, MOE_WIN_SMALL)
            return carry

        lax.fori_loop(0, n_big, big, 0)
        lax.fori_loop(0, n_mid, mid, 0)
        lax.fori_loop(0, n_small, small, 0)


def _moe_kernel(x_ref, xp_ref, mod_ref, modp_ref, rw_ref, rb_ref, tri_ref, wg_ref, wu_ref, wd_ref,
                ln_ref, o_ref, hs_scr, cs_scr, ys_scr, pt_scr):
    t = pl.program_id(0)

    @pl.when(t == 0)
    def _():
        _moe_init(hs_scr, cs_scr, ys_scr, pt_scr)

    o_ref[...] = _moe_finish(xp_ref[...], 1.0 + modp_ref[...][5:6], ln_ref[...], ys_scr, pt_scr)
    start, cnt, any_split = _moe_start(x_ref[...], mod_ref[...], rw_ref, rb_ref, tri_ref,
                                       hs_scr, cs_scr, ys_scr, pt_scr)
    active = t < pl.num_programs(0) - 1
    _moe_windows(active, start, cnt, any_split, wg_ref, wu_ref, wd_ref, hs_scr, cs_scr, ys_scr)


def _conv_moe_kernel(per_batch, x_ref, mods_ref, w_in_ref, cw_ref, w_out_ref, ln0_ref,
                     rw_ref, rb_ref, tri_ref, wg_ref, wu_ref, wd_ref, ln1_ref,
                     o_ref, u_scr, x1_scr, hs_scr, cs_scr, ys_scr, pt_scr):
    s = pl.program_id(0)
    n_tiles = pl.num_programs(0) - 2
    tile_at = lambda lag: jnp.clip(s - lag, 0, n_tiles - 1)

    @pl.when(s == 0)
    def _():
        _moe_init(hs_scr, cs_scr, ys_scr, pt_scr)
        x1_scr[...] = jnp.zeros(x1_scr.shape, F32)

    new = s % 2
    mod_done = mods_ref[tile_at(2) // per_batch]
    o_ref[...] = _moe_finish(x1_scr[new], 1.0 + mod_done[5:6], ln1_ref[...], ys_scr, pt_scr)
    start, cnt, any_split = _moe_start(x1_scr[1 - new], mods_ref[tile_at(1) // per_batch],
                                       rw_ref, rb_ref, tri_ref, hs_scr, cs_scr, ys_scr, pt_scr)
    tile = tile_at(0)
    x1_scr[new] = _conv_residual_ln(x_ref[...], mods_ref[tile // per_batch],
                                    tile % per_batch == 0, w_in_ref, cw_ref, w_out_ref,
                                    ln0_ref[...], u_scr)
    active = jnp.logical_and(s >= 1, s <= n_tiles)
    _moe_windows(active, start, cnt, any_split, wg_ref, wu_ref, wd_ref, hs_scr, cs_scr, ys_scr)


def _moe_weight_spec(layer, shape):
    return pl.BlockSpec((None,) + shape, lambda t: (layer,) + (0,) * len(shape),
                        pipeline_mode=pl.Buffered(1))


def _moe_scratch(tm):
    return [pltpu.VMEM((tm + MOE_WIN_BIG, D), BF16),
            pltpu.VMEM((tm + MOE_WIN_BIG, LANES), F32),
            pltpu.VMEM((tm + MOE_WIN_BIG, D), F32),
            pltpu.VMEM((tm, tm), BF16)]


def _conv_moe(x, mods, w_in, conv_w, w_out, ln0, rw, rbias, wg, wu, wd, ln1):
    batch, seq, _ = x.shape
    tm = TOKEN_TILE
    per_batch = seq // tm
    n_tiles = batch * per_batch
    const2 = lambda t: (0, 0)
    tile = lambda lag: pl.BlockSpec(
        (None, tm, D), lambda t: (jnp.clip(t - lag, 0, n_tiles - 1) // per_batch,
                                  jnp.clip(t - lag, 0, n_tiles - 1) % per_batch, 0))
    tri = (jnp.arange(tm)[:, None] < jnp.arange(tm)[None, :]).astype(BF16)
    return pl.pallas_call(
        functools.partial(_conv_moe_kernel, per_batch),
        grid=(n_tiles + 2,),
        in_specs=[tile(0),
                  pl.BlockSpec((batch, SUBLANES, D), lambda t: (0, 0, 0)),
                  pl.BlockSpec((D, 3 * D), const2),
                  pl.BlockSpec((SUBLANES, D), const2),
                  pl.BlockSpec((D, D), const2),
                  pl.BlockSpec((2, D), const2),
                  pl.BlockSpec((N_EXPERTS, D), const2),
                  pl.BlockSpec((N_EXPERTS, 1), const2),
                  pl.BlockSpec((tm, tm), const2),
                  _moe_weight_spec(0, (N_EXPERTS, D, EXPERT_FF)),
                  _moe_weight_spec(0, (N_EXPERTS, D, EXPERT_FF)),
                  _moe_weight_spec(0, (N_EXPERTS * EXPERT_FF, D)),
                  pl.BlockSpec((2, D), const2)],
        out_specs=tile(2),
        out_shape=jax.ShapeDtypeStruct(x.shape, F32),
        scratch_shapes=[pltpu.VMEM((tm + SUBLANES, D), F32),
                        pltpu.VMEM((2, tm, D), F32)] + _moe_scratch(tm),
        compiler_params=pltpu.CompilerParams(
            dimension_semantics=("arbitrary",), vmem_limit_bytes=FUSED_VMEM_LIMIT),
        name="conv_moe",
    )(x, mods, w_in, conv_w, w_out, ln0, rw, rbias, tri, wg, wu, wd, ln1)


def _moe(layer, x, mod, rw, rbias, wg, wu, wd, ln):
    batch, seq, _ = x.shape
    tm = TOKEN_TILE
    per_batch = seq // tm
    n_tiles = batch * per_batch
    const2 = lambda t: (0, 0)
    cur = lambda t: jnp.minimum(t, n_tiles - 1)
    prev = lambda t: jnp.maximum(t - 1, 0)
    tile = lambda which: pl.BlockSpec(
        (None, tm, D), lambda t: (which(t) // per_batch, which(t) % per_batch, 0))
    mods = lambda which: pl.BlockSpec(
        (None, SUBLANES, D), lambda t: (which(t) // per_batch, 0, 0))
    tri = (jnp.arange(tm)[:, None] < jnp.arange(tm)[None, :]).astype(BF16)
    return pl.pallas_call(
        _moe_kernel,
        grid=(n_tiles + 1,),
        in_specs=[tile(cur), tile(prev), mods(cur), mods(prev),
                  pl.BlockSpec((N_EXPERTS, D), const2),
                  pl.BlockSpec((N_EXPERTS, 1), const2),
                  pl.BlockSpec((tm, tm), const2),
                  _moe_weight_spec(layer, (N_EXPERTS, D, EXPERT_FF)),
                  _moe_weight_spec(layer, (N_EXPERTS, D, EXPERT_FF)),
                  _moe_weight_spec(layer, (N_EXPERTS * EXPERT_FF, D)),
                  pl.BlockSpec((2, D), const2)],
        out_specs=tile(prev),
        out_shape=jax.ShapeDtypeStruct(x.shape, F32),
        scratch_shapes=_moe_scratch(tm),
        compiler_params=_cparams(1),
        name="moe",
    )(x, x, mod, mod, rw, rbias, tri, wg, wu, wd, ln)


def _qkv_kernel(x_ref, mod_ref, kvmod_ref, wq_ref, wkv_ref, *refs):
    outs, (slab, hq_scr, hkv_scr) = refs[:3 * N_GROUPS], refs[3 * N_GROUPS:]
    tm = x_ref.shape[0]
    mod = mod_ref[...]
    kvmod = kvmod_ref[...]
    q_sc, q_sh = 1.0 + mod[1:2], mod[0:1]
    kv_sc, kv_sh = 1.0 + kvmod[1:2], kvmod[0:1]
    n_slabs = D // LANES
    for j in range(n_slabs):
        slab[j] = x_ref[:, j * LANES:(j + 1) * LANES]
    scale = HEAD_DIM ** -0.5 * LOG2E
    for g, (_, dil) in enumerate(DIL_GROUPS):
        n = tm // dil
        for r in range(dil):
            rows = slice(r * n, (r + 1) * n)
            for j in range(n_slabs):
                ls = slice(j * LANES, (j + 1) * LANES)
                piece = slab[j] if dil == 1 else slab[j, pl.ds(r, n, stride=dil), :]
                hq_scr[rows, ls] = (piece * q_sc[:, ls] + q_sh[:, ls]).astype(BF16)
                hkv_scr[rows, ls] = (piece * kv_sc[:, ls] + kv_sh[:, ls]).astype(BF16)
        hq = hq_scr[...]
        hkv = hkv_scr[...]
        cols = slice(g * D, (g + 1) * D)
        vcols = slice(QW + g * D, QW + (g + 1) * D)
        q = (_bdot(hq, wq_ref[:, cols]) * scale).astype(BF16)
        k = _bdot(hkv, wkv_ref[:, cols]).astype(BF16)
        v = _bdot(hkv, wkv_ref[:, vcols]).astype(BF16)
        for r in range(dil):
            rows = slice(r * n, (r + 1) * n)
            outs[3 * g][r] = q[rows]
            outs[3 * g + 1][r] = k[rows]
            outs[3 * g + 2][r] = v[rows]


def _qkv(x, mod, kvmod, wq, wkv):
    batch, seq, _ = x.shape
    tm = TOKEN_TILE
    const = lambda b, i: (0, 0)
    out_specs, out_shape = [], []
    for _, dil in DIL_GROUPS:
        for _ in range(3):
            out_specs.append(pl.BlockSpec((None, dil, tm // dil, D), lambda b, i: (b, 0, i, 0)))
            out_shape.append(jax.ShapeDtypeStruct((batch, dil, seq // dil, D), BF16))
    return pl.pallas_call(
        _qkv_kernel,
        grid=(batch, seq // tm),
        in_specs=[pl.BlockSpec((None, tm, D), lambda b, i: (b, i, 0)),
                  pl.BlockSpec((None, SUBLANES, D), lambda b, i: (b, 0, 0)),
                  pl.BlockSpec((None, SUBLANES, D), lambda b, i: (b, 0, 0)),
                  pl.BlockSpec((D, QW), const),
                  pl.BlockSpec((D, 2 * QW), const)],
        out_specs=out_specs,
        out_shape=out_shape,
        scratch_shapes=[pltpu.VMEM((D // LANES, tm, LANES), F32),
                        pltpu.VMEM((tm, D), BF16),
                        pltpu.VMEM((tm, D), BF16)],
        compiler_params=_cparams(2),
        name="qkv_proj",
    )(x, mod, kvmod, wq, wkv)


def _bucket_tables():
    qi = jnp.arange(BLOCK)[:, None]
    kj = jnp.arange(2 * BLOCK)[None, :]
    dist = BLOCK + qi - kj
    max_exact = NUM_BUCKETS // 2
    tables = []
    for window, dil in DIL_GROUPS:
        n = jnp.maximum(dist, 0) * dil
        nf = jnp.maximum(n, 1).astype(F32)
        large = max_exact + (jnp.log(nf / max_exact) / math.log(MAX_DISTANCE / max_exact)
                             * (NUM_BUCKETS - max_exact)).astype(jnp.int32)
        large = jnp.minimum(large, NUM_BUCKETS - 1)
        bucket = jnp.where(n < max_exact, n, large)
        valid = (dist >= 0) & (dist <= window // dil)
        tables.append(jnp.where(valid, bucket, -1).astype(jnp.int32))
    return jnp.stack(tables)


def _bias_kernel(rb_ref, bkt_ref, o_ref):
    g = pl.program_id(0)
    bkt = bkt_ref[...]
    for h in range(HEADS):
        acc = jnp.full(bkt.shape, NEG_INF, F32)
        for b in range(NUM_BUCKETS):
            acc = jnp.where(bkt == b, rb_ref[b, g * HEADS + h] * LOG2E, acc)
        o_ref[h] = acc


def _bias_tiles(rel_bias):
    return pl.pallas_call(
        _bias_kernel,
        grid=(N_GROUPS,),
        in_specs=[pl.BlockSpec(memory_space=pltpu.SMEM),
                  pl.BlockSpec((None, BLOCK, 2 * BLOCK), lambda g: (g, 0, 0))],
        out_specs=pl.BlockSpec((None, HEADS, BLOCK, 2 * BLOCK), lambda g: (g, 0, 0, 0)),
        out_shape=jax.ShapeDtypeStruct((N_GROUPS, HEADS, BLOCK, 2 * BLOCK), F32),
        compiler_params=_cparams(1),
        name="rel_bias_tiles",
    )(rel_bias, _bucket_tables())


def _attn_kernel(q_ref, kp_ref, kc_ref, vp_ref, vc_ref, bias_ref, o_ref, stat_ref):
    n = pl.program_id(2)
    col = lax.broadcasted_iota(jnp.int32, (BLOCK, 2 * BLOCK), 1)
    no_prev = jnp.logical_and(n == 0, col < BLOCK)
    lane = lax.broadcasted_iota(jnp.int32, (BLOCK, LANES), 1)
    ones = jnp.ones((2 * BLOCK, HEAD_DIM), BF16)
    for r in range(q_ref.shape[0]):
        for j in range(q_ref.shape[1] // BLOCK):
            rows = slice(j * BLOCK, (j + 1) * BLOCK)
            window = slice((j - 1) * BLOCK, (j + 1) * BLOCK)
            stat = jnp.zeros((BLOCK, LANES), F32)
            for h in range(HEADS):
                hs = slice(h * HEAD_DIM, (h + 1) * HEAD_DIM)
                if j == 0:
                    k = jnp.concatenate([kp_ref[r, :, hs], kc_ref[r, rows, hs]], axis=0)
                    v = jnp.concatenate([vp_ref[r, :, hs], vc_ref[r, rows, hs]], axis=0)
                else:
                    k = kc_ref[r, window, hs]
                    v = vc_ref[r, window, hs]
                s = lax.dot_general(q_ref[r, rows, hs], k, (((1,), (1,)), ((), ())),
                                    preferred_element_type=F32) + bias_ref[h]
                if j == 0:
                    s = jnp.where(no_prev, NEG_INF, s)
                m = jnp.max(s, axis=-1, keepdims=True)
                p = jnp.exp2(s - m)
                o_ext = _bdot(p.astype(BF16), jnp.concatenate([v, ones], axis=1))
                o, l = o_ext[:, :HEAD_DIM], o_ext[:, HEAD_DIM:]
                o_ref[r, rows, hs] = o.astype(BF16)
                stat = jnp.where(lane == h, m, jnp.where(lane == HEADS + h, l, stat))
            stat_ref[r, rows, :] = stat


def _attention_group(g, q, k, v, bias):
    batch, dil, sub_len, _ = q.shape
    nq = min(ATTN_QBLOCKS, sub_len // BLOCK)
    n_sub = min(dil, ATTN_QBLOCKS // nq)
    assert sub_len % (nq * BLOCK) == 0 and dil % n_sub == 0
    cur = lambda b, r, n: (b, r, n, 0)
    if sub_len == nq * BLOCK:
        prev = lambda b, r, n: (0, 0, 0, 0)
    else:
        prev = lambda b, r, n: (b, r, jnp.maximum(n * nq - 1, 0), 0)
    blk = (None, n_sub, nq * BLOCK, D)
    blk_prev = (None, n_sub, BLOCK, D)
    return pl.pallas_call(
        _attn_kernel,
        grid=(batch, dil // n_sub, sub_len // (nq * BLOCK)),
        in_specs=[
            pl.BlockSpec(blk, cur),
            pl.BlockSpec(blk_prev, prev),
            pl.BlockSpec(blk, cur),
            pl.BlockSpec(blk_prev, prev),
            pl.BlockSpec(blk, cur),
            pl.BlockSpec((None, HEADS, BLOCK, 2 * BLOCK), lambda b, r, n: (g, 0, 0, 0)),
        ],
        out_specs=[pl.BlockSpec(blk, cur),
                   pl.BlockSpec((None, n_sub, nq * BLOCK, LANES), cur)],
        out_shape=[jax.ShapeDtypeStruct((batch, dil, sub_len, D), BF16),
                   jax.ShapeDtypeStruct((batch, dil, sub_len, LANES), F32)],
        compiler_params=_cparams(3),
        name=f"dilated_attn_g{g}",
    )(q, k, k, v, v, bias)


def _mix_kernel(o0_ref, o1_ref, o2_ref, s0_ref, s1_ref, s2_ref, x_ref, mod_ref,
                expand_ref, wo_ref, ln_ref, out_ref, stat_scr, o_slab):
    tm = x_ref.shape[0]
    n_slabs = D // LANES
    stats = []
    for g, s_ref in enumerate((s0_ref, s1_ref, s2_ref)):
        dil = DIL_GROUPS[g][1]
        if dil == 1:
            stats.append(s_ref[0])
            continue
        for r in range(dil):
            stat_scr[g, pl.ds(r, tm // dil, stride=dil), :] = s_ref[r]
        stats.append(stat_scr[g])
    sums = [pltpu.roll(s, LANES - HEADS, axis=1) for s in stats]
    top = jnp.maximum(stats[0], jnp.maximum(stats[1], stats[2]))
    es = [jnp.exp2(s - top) for s in stats]
    den = es[0] * sums[0] + es[1] * sums[1] + es[2] * sums[2]
    head_lane = lax.broadcasted_iota(jnp.int32, top.shape, 1) < HEADS
    expand = expand_ref[...]
    mixed = None
    for g, o_ref in enumerate((o0_ref, o1_ref, o2_ref)):
        dil = DIL_GROUPS[g][1]
        w = jnp.where(head_lane, es[g] / den, 0.0).astype(BF16)
        w = _bdot(w, expand)
        if dil == 1:
            o = o_ref[0].astype(F32)
        else:
            for r in range(dil):
                for j in range(n_slabs):
                    o_slab[j, pl.ds(r, tm // dil, stride=dil), :] = (
                        o_ref[r, :, j * LANES:(j + 1) * LANES].astype(F32))
            o = jnp.concatenate([o_slab[j] for j in range(n_slabs)], axis=1)
        mixed = w * o if mixed is None else mixed + w * o
    y = _bdot(mixed.astype(BF16), wo_ref[...])
    g1 = mod_ref[...][2:3]
    ln = ln_ref[...]
    out_ref[...] = _layer_norm(ALPHA * x_ref[...] + (1.0 + g1) * y, ln[0:1], ln[1:2])


def _mix(outs, stats, x, mod, wo, ln):
    batch, seq, _ = x.shape
    tm = WIDE_TOKEN_TILE
    const = lambda b, i: (0, 0)
    tok = pl.BlockSpec((None, tm, D), lambda b, i: (b, i, 0))
    dilated = lambda dil, w: pl.BlockSpec((None, dil, tm // dil, w), lambda b, i: (b, 0, i, 0))
    dils = [dil for _, dil in DIL_GROUPS]
    head_of_lane = jnp.arange(D)[None, :] // HEAD_DIM
    expand = (jnp.arange(LANES)[:, None] == head_of_lane).astype(BF16)
    return pl.pallas_call(
        _mix_kernel,
        grid=(batch, seq // tm),
        in_specs=[dilated(d, D) for d in dils] + [dilated(d, LANES) for d in dils] + [
            tok,
            pl.BlockSpec((None, SUBLANES, D), lambda b, i: (b, 0, 0)),
            pl.BlockSpec((LANES, D), const),
            pl.BlockSpec((D, D), const),
            pl.BlockSpec((2, D), const)],
        out_specs=tok,
        out_shape=jax.ShapeDtypeStruct(x.shape, F32),
        scratch_shapes=[pltpu.VMEM((N_GROUPS, tm, LANES), F32),
                        pltpu.VMEM((D // LANES, tm, LANES), F32)],
        compiler_params=_cparams(2),
        name="attn_mix",
    )(*outs, *stats, x, mod, expand, wo, ln)


def kernel(x, c, ada_w, ada_b, ln_g, ln_b, conv_w_in, conv_w, conv_w_out, kv_ada_w,
           kv_ada_b, w_kv, attn_w_q, attn_w_o, rel_bias, router_w, router_bias,
           moe_w_gate, moe_w_up, moe_w_down):
    batch, seq, _ = x.shape
    assert x.shape[2] == D and batch <= SUBLANES
    assert seq % WIDE_TOKEN_TILE == 0 and seq % (BLOCK * DIL_GROUPS[-1][1]) == 0

    c8 = jnp.pad(c, ((0, SUBLANES - batch), (0, 0)))
    mods = _ada_linear(c8, ada_w, ada_b)
    kvmods = _ada_linear(c8, kv_ada_w[None], kv_ada_b[None])[0]
    mod0 = _mod_rows(mods[0], batch, 6)
    mod1 = _mod_rows(mods[1], batch, 6)
    kvmod = _mod_rows(kvmods, batch, 2)
    ln = jnp.stack([ln_g, ln_b], axis=2)

    rw = router_w.T.astype(BF16)
    rbias = router_bias.reshape(N_EXPERTS, 1)
    experts = (moe_w_gate.astype(BF16), moe_w_up.astype(BF16),
               moe_w_down.astype(BF16).reshape(DEPTH, N_EXPERTS * EXPERT_FF, D))
    conv_w8 = jnp.pad(conv_w[0], ((0, SUBLANES - CONV_WIDTH), (0, 0)))

    x = _conv_moe(x, mod0, conv_w_in[0].astype(BF16), conv_w8, conv_w_out[0].astype(BF16),
                  ln[0, 0], rw, rbias, *experts, ln[0, 1])

    qkv = _qkv(x, mod1, kvmod, attn_w_q[0].astype(BF16), w_kv.astype(BF16))
    bias = _bias_tiles(rel_bias)
    outs, stats = zip(*[_attention_group(g, *qkv[3 * g:3 * g + 3], bias)
                       for g in range(N_GROUPS)])
    x = _mix(outs, stats, x, mod1, attn_w_o[0].astype(BF16), ln[1, 0])
    x = _moe(1, x, mod1, rw, rbias, *experts, ln[1, 1])
    return x
```

```python
import functools
import math

import jax
import jax.numpy as jnp
from jax import lax
from jax.experimental import pallas as pl
from jax.experimental.pallas import tpu as pltpu

F32 = jnp.float32
BF16 = jnp.bfloat16

D = 1024
DEPTH = 2
CONV_WIDTH = 3
DIL_GROUPS = ((128, 1), (512, 4), (2048, 16))
N_GROUPS = len(DIL_GROUPS)
HEAD_DIM = 128
HEADS = D // HEAD_DIM
QW = N_GROUPS * D
BLOCK = 128
NUM_BUCKETS = 32
MAX_DISTANCE = 2048
N_EXPERTS = 16
EXPERTS_PER_GROUP = 4
N_EXPERT_GROUPS = N_EXPERTS // EXPERTS_PER_GROUP
EXPERT_FF = D // 4
ALPHA = (2 * DEPTH) ** 0.25
LN_EPS = 1e-5
NEG_INF = -1e30
LOG2E = math.log2(math.e)

LANES = 128
SUBLANES = 8
VMEM_LIMIT = 56 * 1024 * 1024
FUSED_VMEM_LIMIT = 60 * 1024 * 1024
TOKEN_TILE = 512
WIDE_TOKEN_TILE = 1024
MOE_WIN_BIG = 256
MOE_WIN_MID = 192
MOE_WIN_SMALL = 128
BF16_ROWS = 16
ATTN_QBLOCKS = 16
ADA_COL_TILE = 2048


def _cparams(n_axes):
    return pltpu.CompilerParams(
        dimension_semantics=("arbitrary",) * n_axes, vmem_limit_bytes=VMEM_LIMIT)


def _layer_norm(r, g, b):
    mu = jnp.mean(r, axis=-1, keepdims=True)
    c = r - mu
    var = jnp.mean(c * c, axis=-1, keepdims=True)
    return c * lax.rsqrt(var + LN_EPS) * g + b


def _bdot(a, b):
    return jnp.dot(a, b, preferred_element_type=F32)


def _ada_kernel(c_ref, w_ref, b_ref, o_ref):
    c = c_ref[...]
    cond = c * jax.nn.sigmoid(c)
    o_ref[...] = _bdot(cond.astype(BF16), w_ref[...].astype(BF16)) + b_ref[...]


def _ada_linear(c8, w, bias):
    n_layers, _, n = w.shape
    nt = ADA_COL_TILE
    return pl.pallas_call(
        _ada_kernel,
        grid=(n_layers, n // nt),
        in_specs=[pl.BlockSpec((SUBLANES, D), lambda l, j: (0, 0)),
                  pl.BlockSpec((None, D, nt), lambda l, j: (l, 0, j)),
                  pl.BlockSpec((None, 1, nt), lambda l, j: (l, 0, j))],
        out_specs=pl.BlockSpec((None, SUBLANES, nt), lambda l, j: (l, 0, j)),
        out_shape=jax.ShapeDtypeStruct((n_layers, SUBLANES, n), F32),
        compiler_params=_cparams(2),
        name="ada_linear",
    )(c8, w, bias.reshape(n_layers, 1, n))


def _mod_rows(mods, batch, n_vec):
    m = mods[:batch].reshape(batch, n_vec, D)
    return jnp.pad(m, ((0, 0), (0, SUBLANES - n_vec), (0, 0)))


def _route(logits_t, rbias_col):
    aff = jax.nn.sigmoid(logits_t)
    sel = aff + rbias_col
    rows = [sel[e:e + 1, :] for e in range(N_EXPERTS)]
    scores = []
    for g in range(N_EXPERT_GROUPS):
        s0, s1, s2, s3 = rows[EXPERTS_PER_GROUP * g:EXPERTS_PER_GROUP * (g + 1)]
        a, b = jnp.maximum(s0, s1), jnp.minimum(s0, s1)
        c, d = jnp.maximum(s2, s3), jnp.minimum(s2, s3)
        scores.append(jnp.maximum(a, c) + jnp.maximum(jnp.minimum(a, c), jnp.maximum(b, d)))
    best = scores[0]
    best_group = jnp.zeros(best.shape, jnp.int32)
    for g in range(1, N_EXPERT_GROUPS):
        upd = scores[g] > best
        best_group = jnp.where(upd, g, best_group)
        best = jnp.where(upd, scores[g], best)
    e_iota = lax.broadcasted_iota(jnp.int32, sel.shape, 0)
    masked = jnp.where((e_iota // EXPERTS_PER_GROUP) == best_group, sel, NEG_INF)
    m1 = jnp.max(masked, axis=0, keepdims=True)
    idx1 = jnp.min(jnp.where(masked == m1, e_iota, N_EXPERTS), axis=0, keepdims=True)
    masked2 = jnp.where(e_iota == idx1, -jnp.inf, masked)
    m2 = jnp.max(masked2, axis=0, keepdims=True)
    idx2 = jnp.min(jnp.where(masked2 == m2, e_iota, N_EXPERTS), axis=0, keepdims=True)
    a1 = jnp.sum(jnp.where(e_iota == idx1, aff, 0.0), axis=0, keepdims=True)
    a2 = jnp.sum(jnp.where(e_iota == idx2, aff, 0.0), axis=0, keepdims=True)
    den = a1 + a2
    comb = (jnp.where(e_iota == idx1, a1 / den, 0.0)
            + jnp.where(e_iota == idx2, a2 / den, 0.0))
    group1 = idx1 // EXPERTS_PER_GROUP
    split = group1 != idx2 // EXPERTS_PER_GROUP
    return comb, group1, split


def _moe_init(hs_scr, cs_scr, ys_scr, pt_scr):
    tm = pt_scr.shape[0]
    ys_scr[...] = jnp.zeros(ys_scr.shape, F32)
    pt_scr[...] = jnp.zeros(pt_scr.shape, BF16)
    hs_scr[tm:, :] = jnp.zeros((MOE_WIN_BIG, D), BF16)
    cs_scr[tm:, :] = jnp.zeros((MOE_WIN_BIG, LANES), F32)


def _moe_finish(x, gain, ln, ys_scr, pt_scr):
    tm = pt_scr.shape[0]
    y = _bdot(pt_scr[...], ys_scr[0:tm, :].astype(BF16))
    return _layer_norm(ALPHA * x + gain * y, ln[0:1], ln[1:2])


def _moe_start(x, mod, rw_ref, rb_ref, tri_ref, hs_scr, cs_scr, ys_scr, pt_scr):
    tm = x.shape[0]
    sh2, sc2 = mod[3:4], mod[4:5]
    h = x * (1.0 + sc2) + sh2
    hb = h.astype(BF16)
    logits_t = lax.dot_general(rw_ref[...], hb, (((1,), (1,)), ((), ())),
                               preferred_element_type=F32)
    comb_t, group, split = _route(logits_t, rb_ref[...])

    g_iota = lax.broadcasted_iota(jnp.int32, (SUBLANES, tm), 0)
    member = jnp.where(g_iota == group, 1.0, 0.0)
    rank = _bdot(member.astype(BF16), tri_ref[...])
    counts = jnp.sum(member, axis=1, keepdims=True)
    cnt = [counts[g, 0] for g in range(N_EXPERT_GROUPS)]
    start = [0.0]
    for g in range(N_EXPERT_GROUPS - 1):
        start.append(start[g] + cnt[g])
    dest = sum(member[g:g + 1] * (start[g] + rank[g:g + 1]) for g in range(N_EXPERT_GROUPS))
    row = lax.broadcasted_iota(jnp.int32, (tm, tm), 0).astype(F32)
    col = lax.broadcasted_iota(jnp.int32, (tm, tm), 1).astype(F32)
    perm = jnp.where(row == dest, 1.0, 0.0).astype(BF16)
    dest_col = jnp.broadcast_to(dest, (LANES, tm)).T
    dest_col = jnp.concatenate([dest_col] * (tm // LANES), axis=1)
    perm_t = jnp.where(dest_col == col, 1.0, 0.0).astype(BF16)
    pt_scr[...] = perm_t

    hs_scr[0:tm, :] = _bdot(perm, hb).astype(BF16)
    comb_hi = comb_t.astype(BF16)
    comb_lo = (comb_t - comb_hi.astype(F32)).astype(BF16)
    cs = _bdot(jnp.concatenate([comb_hi, comb_lo], axis=0), perm_t)
    cs = cs[0:N_EXPERTS] + cs[N_EXPERTS:2 * N_EXPERTS]
    cs_scr[0:tm, :] = jnp.concatenate(
        [cs, jnp.zeros((LANES - N_EXPERTS, tm), F32)], axis=0).T
    ys_scr[0:tm, :] = jnp.zeros((tm, D), F32)

    any_split = jnp.max(jnp.where(split, 1.0, 0.0)) > 0.0
    return start, cnt, any_split


def _moe_windows(active, start, cnt, any_split, wg_ref, wu_ref, wd_ref, hs_scr, cs_scr, ys_scr):
    tm = hs_scr.shape[0] - MOE_WIN_BIG
    gff = EXPERTS_PER_GROUP * EXPERT_FF
    for g in range(N_EXPERT_GROUPS):
        first = jnp.where(any_split, 0, jnp.asarray(start[g], F32).astype(jnp.int32))
        last = jnp.where(any_split, tm, (start[g] + cnt[g]).astype(jnp.int32))
        first = (first // BF16_ROWS) * BF16_ROWS
        need = jnp.where(jnp.logical_and(active, last > first), last - first, 0)
        n_full = need // MOE_WIN_BIG
        rest = need - n_full * MOE_WIN_BIG
        n_big = n_full + jnp.where(rest > MOE_WIN_MID, 1, 0)
        n_mid = jnp.where(jnp.logical_and(rest > MOE_WIN_SMALL, rest <= MOE_WIN_MID), 1, 0)
        n_small = jnp.where(jnp.logical_and(rest > 0, rest <= MOE_WIN_SMALL), 1, 0)

        def window(start_row, size, g=g):
            rows = pl.ds(pl.multiple_of(start_row, BF16_ROWS), size)
            hs = hs_scr[rows, :]
            parts = []
            for j in range(EXPERTS_PER_GROUP):
                e = g * EXPERTS_PER_GROUP + j
                gate = _bdot(hs, wg_ref[e])
                up = _bdot(hs, wu_ref[e])
                he = gate * jax.nn.sigmoid(gate) * up * cs_scr[rows, e:e + 1]
                parts.append(he.astype(BF16))
            he = jnp.concatenate(parts, axis=1)
            ys_scr[rows, :] += _bdot(he, wd_ref[g * gff:(g + 1) * gff, :])

        def big(w, carry, first=first, window=window):
            window(first + w * MOE_WIN_BIG, MOE_WIN_BIG)
            return carry

        def mid(w, carry, first=first, n_full=n_full, window=window):
            window(first + n_full * MOE_WIN_BIG, MOE_WIN_MID)
            return carry

        def small(w, carry, first=first, n_full=n_full, window=window):
            window(first + n_full * MOE_WIN_BIG, MOE_WIN_SMALL)
            return carry

        lax.fori_loop(0, n_big, big, 0)
        lax.fori_loop(0, n_mid, mid, 0)
        lax.fori_loop(0, n_small, small, 0)


def _moe_kernel(x_ref, xp_ref, mod_ref, modp_ref, rw_ref, rb_ref, tri_ref, wg_ref, wu_ref, wd_ref,
                ln_ref, o_ref, hs_scr, cs_scr, ys_scr, pt_scr):
    t = pl.program_id(0)

    @pl.when(t == 0)
    def _():
        _moe_init(hs_scr, cs_scr, ys_scr, pt_scr)

    o_ref[...] = _moe_finish(xp_ref[...], 1.0 + modp_ref[...][5:6], ln_ref[...], ys_scr, pt_scr)
    start, cnt, any_split = _moe_start(x_ref[...], mod_ref[...], rw_ref, rb_ref, tri_ref,
                                       hs_scr, cs_scr, ys_scr, pt_scr)
    active = t < pl.num_programs(0) - 1
    _moe_windows(active, start, cnt, any_split, wg_ref, wu_ref, wd_ref, hs_scr, cs_scr, ys_scr)


def _conv_moe_kernel(per_batch, x_ref, mods_ref, w_in_ref, cw_ref, w_out_ref, ln0_ref,
                     rw_ref, rb_ref, tri_ref, wg_ref, wu_ref, wd_ref, ln1_ref,
                     o_ref, u_scr, x1_scr, hs_scr, cs_scr, ys_scr, pt_scr):
    s = pl.program_id(0)
    n_tiles = pl.num_programs(0) - 2
    tile_at = lambda lag: jnp.clip(s - lag, 0, n_tiles - 1)

    @pl.when(s == 0)
    def _():
        _moe_init(hs_scr, cs_scr, ys_scr, pt_scr)
        x1_scr[...] = jnp.zeros(x1_scr.shape, F32)

    new = s % 2
    tm = x_ref.shape[0]
    mod_done = mods_ref[tile_at(2) // per_batch]
    y_done = _bdot(pt_scr[...], ys_scr[0:tm, :].astype(BF16))
    mod_r = mods_ref[tile_at(1) // per_batch]
    hb = (x1_scr[1 - new] * (1.0 + mod_r[4:5]) + mod_r[3:4]).astype(BF16)
    logits_t = lax.dot_general(rw_ref[...], hb, (((1,), (1,)), ((), ())),
                               preferred_element_type=F32)
    tile = tile_at(0)
    mod_c = mods_ref[tile // per_batch]
    x = x_ref[...]
    hc = (x * (1.0 + mod_c[1:2]) + mod_c[0:1]).astype(BF16)
    cgate = _bdot(hc, w_in_ref[:, D:2 * D])
    v = _bdot(hc, w_in_ref[:, 2 * D:3 * D])
    ln1 = ln1_ref[...]
    o_ref[...] = _layer_norm(ALPHA * x1_scr[new] + (1.0 + mod_done[5:6]) * y_done,
                             ln1[0:1], ln1[1:2])
    comb_t, group, split = _route(logits_t, rb_ref[...])
    g_iota = lax.broadcasted_iota(jnp.int32, (SUBLANES, tm), 0)
    member = jnp.where(g_iota == group, 1.0, 0.0)
    rank = _bdot(member.astype(BF16), tri_ref[...])
    counts = jnp.sum(member, axis=1, keepdims=True)
    cnt = [counts[g, 0] for g in range(N_EXPERT_GROUPS)]
    start = [0.0]
    for g in range(N_EXPERT_GROUPS - 1):
        start.append(start[g] + cnt[g])
    dest = sum(member[g:g + 1] * (start[g] + rank[g:g + 1]) for g in range(N_EXPERT_GROUPS))
    u = cgate * v
    u_scr[0:SUBLANES, :] = jnp.where(tile % per_batch == 0, 0.0, u_scr[0:SUBLANES, :])
    u_scr[SUBLANES:SUBLANES + tm, :] = u
    cw = cw_ref[...]
    conv = (cw[2:3] * u
            + cw[1:2] * u_scr[SUBLANES - 1:SUBLANES - 1 + tm, :]
            + cw[0:1] * u_scr[SUBLANES - 2:SUBLANES - 2 + tm, :])
    u_scr[0:SUBLANES, :] = u_scr[tm:tm + SUBLANES, :]
    bgate = _bdot(hc, w_in_ref[:, 0:D])
    y = _bdot((bgate * conv).astype(BF16), w_out_ref[...])
    ln0 = ln0_ref[...]
    x1_scr[new] = _layer_norm(ALPHA * x + (1.0 + mod_c[2:3]) * y, ln0[0:1], ln0[1:2])
    row = lax.broadcasted_iota(jnp.int32, (tm, tm), 0).astype(F32)
    col = lax.broadcasted_iota(jnp.int32, (tm, tm), 1).astype(F32)
    perm = jnp.where(row == dest, 1.0, 0.0).astype(BF16)
    dest_col = jnp.broadcast_to(dest, (LANES, tm)).T
    dest_col = jnp.concatenate([dest_col] * (tm // LANES), axis=1)
    perm_t = jnp.where(dest_col == col, 1.0, 0.0).astype(BF16)
    pt_scr[...] = perm_t
    hs_scr[0:tm, :] = _bdot(perm, hb).astype(BF16)
    comb_hi = comb_t.astype(BF16)
    comb_lo = (comb_t - comb_hi.astype(F32)).astype(BF16)
    cs = _bdot(jnp.concatenate([comb_hi, comb_lo], axis=0), perm_t)
    cs = cs[0:N_EXPERTS] + cs[N_EXPERTS:2 * N_EXPERTS]
    cs_scr[0:tm, :] = jnp.concatenate(
        [cs, jnp.zeros((LANES - N_EXPERTS, tm), F32)], axis=0).T
    ys_scr[0:tm, :] = jnp.zeros((tm, D), F32)
    any_split = jnp.max(jnp.where(split, 1.0, 0.0)) > 0.0
    active = jnp.logical_and(s >= 1, s <= n_tiles)
    _moe_windows(active, start, cnt, any_split, wg_ref, wu_ref, wd_ref, hs_scr, cs_scr, ys_scr)


def _moe_weight_spec(layer, shape):
    return pl.BlockSpec((None,) + shape, lambda t: (layer,) + (0,) * len(shape),
                        pipeline_mode=pl.Buffered(1))


def _moe_scratch(tm):
    return [pltpu.VMEM((tm + MOE_WIN_BIG, D), BF16),
            pltpu.VMEM((tm + MOE_WIN_BIG, LANES), F32),
            pltpu.VMEM((tm + MOE_WIN_BIG, D), F32),
            pltpu.VMEM((tm, tm), BF16)]


def _conv_moe(x, mods, w_in, conv_w, w_out, ln0, rw, rbias, wg, wu, wd, ln1):
    batch, seq, _ = x.shape
    tm = TOKEN_TILE
    per_batch = seq // tm
    n_tiles = batch * per_batch
    const2 = lambda t: (0, 0)
    tile = lambda lag: pl.BlockSpec(
        (None, tm, D), lambda t: (jnp.clip(t - lag, 0, n_tiles - 1) // per_batch,
                                  jnp.clip(t - lag, 0, n_tiles - 1) % per_batch, 0))
    tri = (jnp.arange(tm)[:, None] < jnp.arange(tm)[None, :]).astype(BF16)
    return pl.pallas_call(
        functools.partial(_conv_moe_kernel, per_batch),
        grid=(n_tiles + 2,),
        in_specs=[tile(0),
                  pl.BlockSpec((batch, SUBLANES, D), lambda t: (0, 0, 0)),
                  pl.BlockSpec((D, 3 * D), const2),
                  pl.BlockSpec((SUBLANES, D), const2),
                  pl.BlockSpec((D, D), const2),
                  pl.BlockSpec((2, D), const2),
                  pl.BlockSpec((N_EXPERTS, D), const2),
                  pl.BlockSpec((N_EXPERTS, 1), const2),
                  pl.BlockSpec((tm, tm), const2),
                  _moe_weight_spec(0, (N_EXPERTS, D, EXPERT_FF)),
                  _moe_weight_spec(0, (N_EXPERTS, D, EXPERT_FF)),
                  _moe_weight_spec(0, (N_EXPERTS * EXPERT_FF, D)),
                  pl.BlockSpec((2, D), const2)],
        out_specs=tile(2),
        out_shape=jax.ShapeDtypeStruct(x.shape, F32),
        scratch_shapes=[pltpu.VMEM((tm + SUBLANES, D), F32),
                        pltpu.VMEM((2, tm, D), F32)] + _moe_scratch(tm),
        compiler_params=pltpu.CompilerParams(
            dimension_semantics=("arbitrary",), vmem_limit_bytes=FUSED_VMEM_LIMIT),
        name="conv_moe",
    )(x, mods, w_in, conv_w, w_out, ln0, rw, rbias, tri, wg, wu, wd, ln1)


def _moe(layer, x, mod, rw, rbias, wg, wu, wd, ln):
    batch, seq, _ = x.shape
    tm = TOKEN_TILE
    per_batch = seq // tm
    n_tiles = batch * per_batch
    const2 = lambda t: (0, 0)
    cur = lambda t: jnp.minimum(t, n_tiles - 1)
    prev = lambda t: jnp.maximum(t - 1, 0)
    tile = lambda which: pl.BlockSpec(
        (None, tm, D), lambda t: (which(t) // per_batch, which(t) % per_batch, 0))
    mods = lambda which: pl.BlockSpec(
        (None, SUBLANES, D), lambda t: (which(t) // per_batch, 0, 0))
    tri = (jnp.arange(tm)[:, None] < jnp.arange(tm)[None, :]).astype(BF16)
    return pl.pallas_call(
        _moe_kernel,
        grid=(n_tiles + 1,),
        in_specs=[tile(cur), tile(prev), mods(cur), mods(prev),
                  pl.BlockSpec((N_EXPERTS, D), const2),
                  pl.BlockSpec((N_EXPERTS, 1), const2),
                  pl.BlockSpec((tm, tm), const2),
                  _moe_weight_spec(layer, (N_EXPERTS, D, EXPERT_FF)),
                  _moe_weight_spec(layer, (N_EXPERTS, D, EXPERT_FF)),
                  _moe_weight_spec(layer, (N_EXPERTS * EXPERT_FF, D)),
                  pl.BlockSpec((2, D), const2)],
        out_specs=tile(prev),
        out_shape=jax.ShapeDtypeStruct(x.shape, F32),
        scratch_shapes=_moe_scratch(tm),
        compiler_params=_cparams(1),
        name="moe",
    )(x, x, mod, mod, rw, rbias, tri, wg, wu, wd, ln)


def _qkv_kernel(x_ref, mod_ref, kvmod_ref, wq_ref, wkv_ref, *refs):
    outs, (slab, hq_scr, hkv_scr) = refs[:3 * N_GROUPS], refs[3 * N_GROUPS:]
    tm = x_ref.shape[0]
    mod = mod_ref[...]
    kvmod = kvmod_ref[...]
    q_sc, q_sh = 1.0 + mod[1:2], mod[0:1]
    kv_sc, kv_sh = 1.0 + kvmod[1:2], kvmod[0:1]
    n_slabs = D // LANES
    for j in range(n_slabs):
        slab[j] = x_ref[:, j * LANES:(j + 1) * LANES]
    scale = HEAD_DIM ** -0.5 * LOG2E
    for g, (_, dil) in enumerate(DIL_GROUPS):
        n = tm // dil
        for r in range(dil):
            rows = slice(r * n, (r + 1) * n)
            for j in range(n_slabs):
                ls = slice(j * LANES, (j + 1) * LANES)
                piece = slab[j] if dil == 1 else slab[j, pl.ds(r, n, stride=dil), :]
                hq_scr[rows, ls] = (piece * q_sc[:, ls] + q_sh[:, ls]).astype(BF16)
                hkv_scr[rows, ls] = (piece * kv_sc[:, ls] + kv_sh[:, ls]).astype(BF16)
        hq = hq_scr[...]
        hkv = hkv_scr[...]
        cols = slice(g * D, (g + 1) * D)
        vcols = slice(QW + g * D, QW + (g + 1) * D)
        q = (_bdot(hq, wq_ref[:, cols]) * scale).astype(BF16)
        k = _bdot(hkv, wkv_ref[:, cols]).astype(BF16)
        v = _bdot(hkv, wkv_ref[:, vcols]).astype(BF16)
        for r in range(dil):
            rows = slice(r * n, (r + 1) * n)
            outs[3 * g][r] = q[rows]
            outs[3 * g + 1][r] = k[rows]
            outs[3 * g + 2][r] = v[rows]


def _qkv(x, mod, kvmod, wq, wkv):
    batch, seq, _ = x.shape
    tm = TOKEN_TILE
    const = lambda b, i: (0, 0)
    out_specs, out_shape = [], []
    for _, dil in DIL_GROUPS:
        for _ in range(3):
            out_specs.append(pl.BlockSpec((None, dil, tm // dil, D), lambda b, i: (b, 0, i, 0)))
            out_shape.append(jax.ShapeDtypeStruct((batch, dil, seq // dil, D), BF16))
    return pl.pallas_call(
        _qkv_kernel,
        grid=(batch, seq // tm),
        in_specs=[pl.BlockSpec((None, tm, D), lambda b, i: (b, i, 0)),
                  pl.BlockSpec((None, SUBLANES, D), lambda b, i: (b, 0, 0)),
                  pl.BlockSpec((None, SUBLANES, D), lambda b, i: (b, 0, 0)),
                  pl.BlockSpec((D, QW), const),
                  pl.BlockSpec((D, 2 * QW), const)],
        out_specs=out_specs,
        out_shape=out_shape,
        scratch_shapes=[pltpu.VMEM((D // LANES, tm, LANES), F32),
                        pltpu.VMEM((tm, D), BF16),
                        pltpu.VMEM((tm, D), BF16)],
        compiler_params=_cparams(2),
        name="qkv_proj",
    )(x, mod, kvmod, wq, wkv)


def _bucket_tables():
    qi = jnp.arange(BLOCK)[:, None]
    kj = jnp.arange(2 * BLOCK)[None, :]
    dist = BLOCK + qi - kj
    max_exact = NUM_BUCKETS // 2
    tables = []
    for window, dil in DIL_GROUPS:
        n = jnp.maximum(dist, 0) * dil
        nf = jnp.maximum(n, 1).astype(F32)
        large = max_exact + (jnp.log(nf / max_exact) / math.log(MAX_DISTANCE / max_exact)
                             * (NUM_BUCKETS - max_exact)).astype(jnp.int32)
        large = jnp.minimum(large, NUM_BUCKETS - 1)
        bucket = jnp.where(n < max_exact, n, large)
        valid = (dist >= 0) & (dist <= window // dil)
        tables.append(jnp.where(valid, bucket, -1).astype(jnp.int32))
    return jnp.stack(tables)


def _bias_kernel(rb_ref, bkt_ref, o_ref):
    g = pl.program_id(0)
    bkt = bkt_ref[...]
    for h in range(HEADS):
        acc = jnp.full(bkt.shape, NEG_INF, F32)
        for b in range(NUM_BUCKETS):
            acc = jnp.where(bkt == b, rb_ref[b, g * HEADS + h] * LOG2E, acc)
        o_ref[h] = acc


def _bias_tiles(rel_bias):
    return pl.pallas_call(
        _bias_kernel,
        grid=(N_GROUPS,),
        in_specs=[pl.BlockSpec(memory_space=pltpu.SMEM),
                  pl.BlockSpec((None, BLOCK, 2 * BLOCK), lambda g: (g, 0, 0))],
        out_specs=pl.BlockSpec((None, HEADS, BLOCK, 2 * BLOCK), lambda g: (g, 0, 0, 0)),
        out_shape=jax.ShapeDtypeStruct((N_GROUPS, HEADS, BLOCK, 2 * BLOCK), F32),
        compiler_params=_cparams(1),
        name="rel_bias_tiles",
    )(rel_bias, _bucket_tables())


def _attn_kernel(q_ref, kp_ref, kc_ref, vp_ref, vc_ref, bias_ref, o_ref, stat_ref):
    n = pl.program_id(2)
    col = lax.broadcasted_iota(jnp.int32, (BLOCK, 2 * BLOCK), 1)
    no_prev = jnp.logical_and(n == 0, col < BLOCK)
    lane = lax.broadcasted_iota(jnp.int32, (BLOCK, LANES), 1)
    ones = jnp.ones((2 * BLOCK, HEAD_DIM), BF16)
    for r in range(q_ref.shape[0]):
        for j in range(q_ref.shape[1] // BLOCK):
            rows = slice(j * BLOCK, (j + 1) * BLOCK)
            window = slice((j - 1) * BLOCK, (j + 1) * BLOCK)
            stat = jnp.zeros((BLOCK, LANES), F32)
            for h in range(HEADS):
                hs = slice(h * HEAD_DIM, (h + 1) * HEAD_DIM)
                if j == 0:
                    k = jnp.concatenate([kp_ref[r, :, hs], kc_ref[r, rows, hs]], axis=0)
                    v = jnp.concatenate([vp_ref[r, :, hs], vc_ref[r, rows, hs]], axis=0)
                else:
                    k = kc_ref[r, window, hs]
                    v = vc_ref[r, window, hs]
                s = lax.dot_general(q_ref[r, rows, hs], k, (((1,), (1,)), ((), ())),
                                    preferred_element_type=F32) + bias_ref[h]
                if j == 0:
                    s = jnp.where(no_prev, NEG_INF, s)
                m = jnp.max(s, axis=-1, keepdims=True)
                p = jnp.exp2(s - m)
                o_ext = _bdot(p.astype(BF16), jnp.concatenate([v, ones], axis=1))
                o, l = o_ext[:, :HEAD_DIM], o_ext[:, HEAD_DIM:]
                o_ref[r, rows, hs] = o.astype(BF16)
                stat = jnp.where(lane == h, m, jnp.where(lane == HEADS + h, l, stat))
            stat_ref[r, rows, :] = stat


def _attention_group(g, q, k, v, bias):
    batch, dil, sub_len, _ = q.shape
    nq = min(ATTN_QBLOCKS, sub_len // BLOCK)
    n_sub = min(dil, ATTN_QBLOCKS // nq)
    assert sub_len % (nq * BLOCK) == 0 and dil % n_sub == 0
    cur = lambda b, r, n: (b, r, n, 0)
    if sub_len == nq * BLOCK:
        prev = lambda b, r, n: (0, 0, 0, 0)
    else:
        prev = lambda b, r, n: (b, r, jnp.maximum(n * nq - 1, 0), 0)
    blk = (None, n_sub, nq * BLOCK, D)
    blk_prev = (None, n_sub, BLOCK, D)
    return pl.pallas_call(
        _attn_kernel,
        grid=(batch, dil // n_sub, sub_len // (nq * BLOCK)),
        in_specs=[
            pl.BlockSpec(blk, cur),
            pl.BlockSpec(blk_prev, prev),
            pl.BlockSpec(blk, cur),
            pl.BlockSpec(blk_prev, prev),
            pl.BlockSpec(blk, cur),
            pl.BlockSpec((None, HEADS, BLOCK, 2 * BLOCK), lambda b, r, n: (g, 0, 0, 0)),
        ],
        out_specs=[pl.BlockSpec(blk, cur),
                   pl.BlockSpec((None, n_sub, nq * BLOCK, LANES), cur)],
        out_shape=[jax.ShapeDtypeStruct((batch, dil, sub_len, D), BF16),
                   jax.ShapeDtypeStruct((batch, dil, sub_len, LANES), F32)],
        compiler_params=_cparams(3),
        name=f"dilated_attn_g{g}",
    )(q, k, k, v, v, bias)


def _mix_kernel(o0_ref, o1_ref, o2_ref, s0_ref, s1_ref, s2_ref, x_ref, mod_ref,
                expand_ref, wo_ref, ln_ref, out_ref, stat_scr, o_slab):
    tm = x_ref.shape[0]
    n_slabs = D // LANES
    stats = []
    for g, s_ref in enumerate((s0_ref, s1_ref, s2_ref)):
        dil = DIL_GROUPS[g][1]
        if dil == 1:
            stats.append(s_ref[0])
            continue
        for r in range(dil):
            stat_scr[g, pl.ds(r, tm // dil, stride=dil), :] = s_ref[r]
        stats.append(stat_scr[g])
    sums = [pltpu.roll(s, LANES - HEADS, axis=1) for s in stats]
    top = jnp.maximum(stats[0], jnp.maximum(stats[1], stats[2]))
    es = [jnp.exp2(s - top) for s in stats]
    den = es[0] * sums[0] + es[1] * sums[1] + es[2] * sums[2]
    head_lane = lax.broadcasted_iota(jnp.int32, top.shape, 1) < HEADS
    expand = expand_ref[...]
    mixed = None
    for g, o_ref in enumerate((o0_ref, o1_ref, o2_ref)):
        dil = DIL_GROUPS[g][1]
        w = jnp.where(head_lane, es[g] / den, 0.0).astype(BF16)
        w = _bdot(w, expand)
        if dil == 1:
            o = o_ref[0].astype(F32)
        else:
            for r in range(dil):
                for j in range(n_slabs):
                    o_slab[j, pl.ds(r, tm // dil, stride=dil), :] = (
                        o_ref[r, :, j * LANES:(j + 1) * LANES].astype(F32))
            o = jnp.concatenate([o_slab[j] for j in range(n_slabs)], axis=1)
        mixed = w * o if mixed is None else mixed + w * o
    y = _bdot(mixed.astype(BF16), wo_ref[...])
    g1 = mod_ref[...][2:3]
    ln = ln_ref[...]
    out_ref[...] = _layer_norm(ALPHA * x_ref[...] + (1.0 + g1) * y, ln[0:1], ln[1:2])


def _mix(outs, stats, x, mod, wo, ln):
    batch, seq, _ = x.shape
    tm = WIDE_TOKEN_TILE
    const = lambda b, i: (0, 0)
    tok = pl.BlockSpec((None, tm, D), lambda b, i: (b, i, 0))
    dilated = lambda dil, w: pl.BlockSpec((None, dil, tm // dil, w), lambda b, i: (b, 0, i, 0))
    dils = [dil for _, dil in DIL_GROUPS]
    head_of_lane = jnp.arange(D)[None, :] // HEAD_DIM
    expand = (jnp.arange(LANES)[:, None] == head_of_lane).astype(BF16)
    return pl.pallas_call(
        _mix_kernel,
        grid=(batch, seq // tm),
        in_specs=[dilated(d, D) for d in dils] + [dilated(d, LANES) for d in dils] + [
            tok,
            pl.BlockSpec((None, SUBLANES, D), lambda b, i: (b, 0, 0)),
            pl.BlockSpec((LANES, D), const),
            pl.BlockSpec((D, D), const),
            pl.BlockSpec((2, D), const)],
        out_specs=tok,
        out_shape=jax.ShapeDtypeStruct(x.shape, F32),
        scratch_shapes=[pltpu.VMEM((N_GROUPS, tm, LANES), F32),
                        pltpu.VMEM((D // LANES, tm, LANES), F32)],
        compiler_params=_cparams(2),
        name="attn_mix",
    )(*outs, *stats, x, mod, expand, wo, ln)


def kernel(x, c, ada_w, ada_b, ln_g, ln_b, conv_w_in, conv_w, conv_w_out, kv_ada_w,
           kv_ada_b, w_kv, attn_w_q, attn_w_o, rel_bias, router_w, router_bias,
           moe_w_gate, moe_w_up, moe_w_down):
    batch, seq, _ = x.shape
    assert x.shape[2] == D and batch <= SUBLANES
    assert seq % WIDE_TOKEN_TILE == 0 and seq % (BLOCK * DIL_GROUPS[-1][1]) == 0

    c8 = jnp.pad(c, ((0, SUBLANES - batch), (0, 0)))
    mods = _ada_linear(c8, ada_w, ada_b)
    kvmods = _ada_linear(c8, kv_ada_w[None], kv_ada_b[None])[0]
    mod0 = _mod_rows(mods[0], batch, 6)
    mod1 = _mod_rows(mods[1], batch, 6)
    kvmod = _mod_rows(kvmods, batch, 2)
    ln = jnp.stack([ln_g, ln_b], axis=2)

    rw = router_w.T.astype(BF16)
    rbias = router_bias.reshape(N_EXPERTS, 1)
    experts = (moe_w_gate.astype(BF16), moe_w_up.astype(BF16),
               moe_w_down.astype(BF16).reshape(DEPTH, N_EXPERTS * EXPERT_FF, D))
    conv_w8 = jnp.pad(conv_w[0], ((0, SUBLANES - CONV_WIDTH), (0, 0)))

    x = _conv_moe(x, mod0, conv_w_in[0].astype(BF16), conv_w8, conv_w_out[0].astype(BF16),
                  ln[0, 0], rw, rbias, *experts, ln[0, 1])

    qkv = _qkv(x, mod1, kvmod, attn_w_q[0].astype(BF16), w_kv.astype(BF16))
    bias = _bias_tiles(rel_bias)
    outs, stats = zip(*[_attention_group(g, *qkv[3 * g:3 * g + 3], bias)
                       for g in range(N_GROUPS)])
    x = _mix(outs, stats, x, mod1, attn_w_o[0].astype(BF16), ln[1, 0])
    x = _moe(1, x, mod1, rw, rbias, *experts, ln[1, 1])
    return x
```

```python
import functools
import math

import jax
import jax.numpy as jnp
from jax import lax
from jax.experimental import pallas as pl
from jax.experimental.pallas import tpu as pltpu

F32 = jnp.float32
BF16 = jnp.bfloat16

D = 1024
DEPTH = 2
CONV_WIDTH = 3
DIL_GROUPS = ((128, 1), (512, 4), (2048, 16))
N_GROUPS = len(DIL_GROUPS)
HEAD_DIM = 128
HEADS = D // HEAD_DIM
QW = N_GROUPS * D
BLOCK = 128
NUM_BUCKETS = 32
MAX_DISTANCE = 2048
N_EXPERTS = 16
EXPERTS_PER_GROUP = 4
N_EXPERT_GROUPS = N_EXPERTS // EXPERTS_PER_GROUP
EXPERT_FF = D // 4
ALPHA = (2 * DEPTH) ** 0.25
LN_EPS = 1e-5
NEG_INF = -1e30
LOG2E = math.log2(math.e)

LANES = 128
SUBLANES = 8
VMEM_LIMIT = 56 * 1024 * 1024
FUSED_VMEM_LIMIT = 60 * 1024 * 1024
TOKEN_TILE = 512
WIDE_TOKEN_TILE = 1024
MOE_WIN_BIG = 256
MOE_WIN_MID = 192
MOE_WIN_SMALL = 128
BF16_ROWS = 16
ATTN_QBLOCKS = 16
ADA_COL_TILE = 2048


def _cparams(n_axes):
    return pltpu.CompilerParams(
        dimension_semantics=("arbitrary",) * n_axes, vmem_limit_bytes=VMEM_LIMIT)


def _layer_norm(r, g, b):
    mu = jnp.mean(r, axis=-1, keepdims=True)
    c = r - mu
    var = jnp.mean(c * c, axis=-1, keepdims=True)
    return c * lax.rsqrt(var + LN_EPS) * g + b


def _bdot(a, b):
    return jnp.dot(a, b, preferred_element_type=F32)


def _ada_kernel(c_ref, w_ref, b_ref, o_ref):
    c = c_ref[...]
    cond = c * jax.nn.sigmoid(c)
    o_ref[...] = _bdot(cond.astype(BF16), w_ref[...].astype(BF16)) + b_ref[...]


def _ada_linear(c8, w, bias):
    n_layers, _, n = w.shape
    nt = ADA_COL_TILE
    return pl.pallas_call(
        _ada_kernel,
        grid=(n_layers, n // nt),
        in_specs=[pl.BlockSpec((SUBLANES, D), lambda l, j: (0, 0)),
                  pl.BlockSpec((None, D, nt), lambda l, j: (l, 0, j)),
                  pl.BlockSpec((None, 1, nt), lambda l, j: (l, 0, j))],
        out_specs=pl.BlockSpec((None, SUBLANES, nt), lambda l, j: (l, 0, j)),
        out_shape=jax.ShapeDtypeStruct((n_layers, SUBLANES, n), F32),
        compiler_params=_cparams(2),
        name="ada_linear",
    )(c8, w, bias.reshape(n_layers, 1, n))


def _mod_rows(mods, batch, n_vec):
    m = mods[:batch].reshape(batch, n_vec, D)
    return jnp.pad(m, ((0, 0), (0, SUBLANES - n_vec), (0, 0)))


def _route(logits_t, rbias_col):
    aff = jax.nn.sigmoid(logits_t)
    sel = aff + rbias_col
    rows = [sel[e:e + 1, :] for e in range(N_EXPERTS)]
    scores = []
    for g in range(N_EXPERT_GROUPS):
        s0, s1, s2, s3 = rows[EXPERTS_PER_GROUP * g:EXPERTS_PER_GROUP * (g + 1)]
        a, b = jnp.maximum(s0, s1), jnp.minimum(s0, s1)
        c, d = jnp.maximum(s2, s3), jnp.minimum(s2, s3)
        scores.append(jnp.maximum(a, c) + jnp.maximum(jnp.minimum(a, c), jnp.maximum(b, d)))
    best = scores[0]
    best_group = jnp.zeros(best.shape, jnp.int32)
    for g in range(1, N_EXPERT_GROUPS):
        upd = scores[g] > best
        best_group = jnp.where(upd, g, best_group)
        best = jnp.where(upd, scores[g], best)
    e_iota = lax.broadcasted_iota(jnp.int32, sel.shape, 0)
    masked = jnp.where((e_iota // EXPERTS_PER_GROUP) == best_group, sel, NEG_INF)
    m1 = jnp.max(masked, axis=0, keepdims=True)
    idx1 = jnp.min(jnp.where(masked == m1, e_iota, N_EXPERTS), axis=0, keepdims=True)
    masked2 = jnp.where(e_iota == idx1, -jnp.inf, masked)
    m2 = jnp.max(masked2, axis=0, keepdims=True)
    idx2 = jnp.min(jnp.where(masked2 == m2, e_iota, N_EXPERTS), axis=0, keepdims=True)
    a1 = jnp.sum(jnp.where(e_iota == idx1, aff, 0.0), axis=0, keepdims=True)
    a2 = jnp.sum(jnp.where(e_iota == idx2, aff, 0.0), axis=0, keepdims=True)
    den = a1 + a2
    comb = (jnp.where(e_iota == idx1, a1 / den, 0.0)
            + jnp.where(e_iota == idx2, a2 / den, 0.0))
    group1 = idx1 // EXPERTS_PER_GROUP
    split = group1 != idx2 // EXPERTS_PER_GROUP
    return comb, group1, split


def _moe_init(hs_scr, cs_scr, ys_scr, pt_scr):
    tm = pt_scr.shape[0]
    ys_scr[...] = jnp.zeros(ys_scr.shape, F32)
    pt_scr[...] = jnp.zeros(pt_scr.shape, BF16)
    hs_scr[tm:, :] = jnp.zeros((MOE_WIN_BIG, D), BF16)
    cs_scr[tm:, :] = jnp.zeros((MOE_WIN_BIG, LANES), F32)


def _moe_windows(active, start, cnt, any_split, wg_ref, wu_ref, wd_ref, hs_scr, cs_scr, ys_scr):
    tm = hs_scr.shape[0] - MOE_WIN_BIG
    gff = EXPERTS_PER_GROUP * EXPERT_FF
    for g in range(N_EXPERT_GROUPS):
        first = jnp.where(any_split, 0, jnp.asarray(start[g], F32).astype(jnp.int32))
        last = jnp.where(any_split, tm, (start[g] + cnt[g]).astype(jnp.int32))
        first = (first // BF16_ROWS) * BF16_ROWS
        need = jnp.where(jnp.logical_and(active, last > first), last - first, 0)
        n_full = need // MOE_WIN_BIG
        rest = need - n_full * MOE_WIN_BIG
        n_big = n_full + jnp.where(rest > MOE_WIN_MID, 1, 0)
        n_mid = jnp.where(jnp.logical_and(rest > MOE_WIN_SMALL, rest <= MOE_WIN_MID), 1, 0)
        n_small = jnp.where(jnp.logical_and(rest > 0, rest <= MOE_WIN_SMALL), 1, 0)

        def window(start_row, size, g=g):
            rows = pl.ds(pl.multiple_of(start_row, BF16_ROWS), size)
            hs = hs_scr[rows, :]
            parts = []
            for j in range(EXPERTS_PER_GROUP):
                e = g * EXPERTS_PER_GROUP + j
                gate = _bdot(hs, wg_ref[e])
                up = _bdot(hs, wu_ref[e])
                he = gate * jax.nn.sigmoid(gate) * up * cs_scr[rows, e:e + 1]
                parts.append(he.astype(BF16))
            he = jnp.concatenate(parts, axis=1)
            ys_scr[rows, :] += _bdot(he, wd_ref[g * gff:(g + 1) * gff, :])

        def big(w, carry, first=first, window=window):
            window(first + w * MOE_WIN_BIG, MOE_WIN_BIG)
            return carry

        def mid(w, carry, first=first, n_full=n_full, window=window):
            window(first + n_full * MOE_WIN_BIG, MOE_WIN_MID)
            return carry

        def small(w, carry, first=first, n_full=n_full, window=window):
            window(first + n_full * MOE_WIN_BIG, MOE_WIN_SMALL)
            return carry

        lax.fori_loop(0, n_big, big, 0)
        lax.fori_loop(0, n_mid, mid, 0)
        lax.fori_loop(0, n_small, small, 0)


def _moe_kernel(x_ref, xp_ref, mod_ref, modp_ref, rw_ref, rb_ref, tri_ref, wg_ref, wu_ref, wd_ref,
                ln_ref, o_ref, hs_scr, cs_scr, ys_scr, pt_scr):
    t = pl.program_id(0)

    @pl.when(t == 0)
    def _():
        _moe_init(hs_scr, cs_scr, ys_scr, pt_scr)

    tm = x_ref.shape[0]
    mod = mod_ref[...]
    hb = (x_ref[...] * (1.0 + mod[4:5]) + mod[3:4]).astype(BF16)
    logits_t = lax.dot_general(rw_ref[...], hb, (((1,), (1,)), ((), ())),
                               preferred_element_type=F32)
    y_done = _bdot(pt_scr[...], ys_scr[0:tm, :].astype(BF16))
    comb_t, group, split = _route(logits_t, rb_ref[...])
    g_iota = lax.broadcasted_iota(jnp.int32, (SUBLANES, tm), 0)
    member = jnp.where(g_iota == group, 1.0, 0.0)
    rank = _bdot(member.astype(BF16), tri_ref[...])
    ln = ln_ref[...]
    o_ref[...] = _layer_norm(ALPHA * xp_ref[...] + (1.0 + modp_ref[...][5:6]) * y_done,
                             ln[0:1], ln[1:2])
    counts = jnp.sum(member, axis=1, keepdims=True)
    cnt = [counts[g, 0] for g in range(N_EXPERT_GROUPS)]
    start = [0.0]
    for g in range(N_EXPERT_GROUPS - 1):
        start.append(start[g] + cnt[g])
    dest = sum(member[g:g + 1] * (start[g] + rank[g:g + 1]) for g in range(N_EXPERT_GROUPS))
    row = lax.broadcasted_iota(jnp.int32, (tm, tm), 0).astype(F32)
    col = lax.broadcasted_iota(jnp.int32, (tm, tm), 1).astype(F32)
    perm = jnp.where(row == dest, 1.0, 0.0).astype(BF16)
    dest_col = jnp.broadcast_to(dest, (LANES, tm)).T
    dest_col = jnp.concatenate([dest_col] * (tm // LANES), axis=1)
    perm_t = jnp.where(dest_col == col, 1.0, 0.0).astype(BF16)
    pt_scr[...] = perm_t
    hs_scr[0:tm, :] = _bdot(perm, hb).astype(BF16)
    comb_hi = comb_t.astype(BF16)
    comb_lo = (comb_t - comb_hi.astype(F32)).astype(BF16)
    cs = _bdot(jnp.concatenate([comb_hi, comb_lo], axis=0), perm_t)
    cs = cs[0:N_EXPERTS] + cs[N_EXPERTS:2 * N_EXPERTS]
    cs_scr[0:tm, :] = jnp.concatenate(
        [cs, jnp.zeros((LANES - N_EXPERTS, tm), F32)], axis=0).T
    ys_scr[0:tm, :] = jnp.zeros((tm, D), F32)
    any_split = jnp.max(jnp.where(split, 1.0, 0.0)) > 0.0
    active = t < pl.num_programs(0) - 1
    _moe_windows(active, start, cnt, any_split, wg_ref, wu_ref, wd_ref, hs_scr, cs_scr, ys_scr)


def _conv_moe_kernel(per_batch, x_ref, mods_ref, w_in_ref, cw_ref, w_out_ref, ln0_ref,
                     rw_ref, rb_ref, tri_ref, wg_ref, wu_ref, wd_ref, ln1_ref,
                     o_ref, u_scr, x1_scr, hs_scr, cs_scr, ys_scr, pt_scr):
    s = pl.program_id(0)
    n_tiles = pl.num_programs(0) - 2
    tile_at = lambda lag: jnp.clip(s - lag, 0, n_tiles - 1)

    @pl.when(s == 0)
    def _():
        _moe_init(hs_scr, cs_scr, ys_scr, pt_scr)
        x1_scr[...] = jnp.zeros(x1_scr.shape, F32)

    new = s % 2
    tm = x_ref.shape[0]
    mod_done = mods_ref[tile_at(2) // per_batch]
    y_done = _bdot(pt_scr[...], ys_scr[0:tm, :].astype(BF16))
    mod_r = mods_ref[tile_at(1) // per_batch]
    hb = (x1_scr[1 - new] * (1.0 + mod_r[4:5]) + mod_r[3:4]).astype(BF16)
    logits_t = lax.dot_general(rw_ref[...], hb, (((1,), (1,)), ((), ())),
                               preferred_element_type=F32)
    tile = tile_at(0)
    mod_c = mods_ref[tile // per_batch]
    x = x_ref[...]
    hc = (x * (1.0 + mod_c[1:2]) + mod_c[0:1]).astype(BF16)
    cgate = _bdot(hc, w_in_ref[:, D:2 * D])
    v = _bdot(hc, w_in_ref[:, 2 * D:3 * D])
    ln1 = ln1_ref[...]
    o_ref[...] = _layer_norm(ALPHA * x1_scr[new] + (1.0 + mod_done[5:6]) * y_done,
                             ln1[0:1], ln1[1:2])
    comb_t, group, split = _route(logits_t, rb_ref[...])
    g_iota = lax.broadcasted_iota(jnp.int32, (SUBLANES, tm), 0)
    member = jnp.where(g_iota == group, 1.0, 0.0)
    rank = _bdot(member.astype(BF16), tri_ref[...])
    counts = jnp.sum(member, axis=1, keepdims=True)
    cnt = [counts[g, 0] for g in range(N_EXPERT_GROUPS)]
    start = [0.0]
    for g in range(N_EXPERT_GROUPS - 1):
        start.append(start[g] + cnt[g])
    dest = sum(member[g:g + 1] * (start[g] + rank[g:g + 1]) for g in range(N_EXPERT_GROUPS))
    u = cgate * v
    u_scr[0:SUBLANES, :] = jnp.where(tile % per_batch == 0, 0.0, u_scr[0:SUBLANES, :])
    u_scr[SUBLANES:SUBLANES + tm, :] = u
    cw = cw_ref[...]
    conv = (cw[2:3] * u
            + cw[1:2] * u_scr[SUBLANES - 1:SUBLANES - 1 + tm, :]
            + cw[0:1] * u_scr[SUBLANES - 2:SUBLANES - 2 + tm, :])
    u_scr[0:SUBLANES, :] = u_scr[tm:tm + SUBLANES, :]
    bgate = _bdot(hc, w_in_ref[:, 0:D])
    y = _bdot((bgate * conv).astype(BF16), w_out_ref[...])
    ln0 = ln0_ref[...]
    x1_scr[new] = _layer_norm(ALPHA * x + (1.0 + mod_c[2:3]) * y, ln0[0:1], ln0[1:2])
    row = lax.broadcasted_iota(jnp.int32, (tm, tm), 0).astype(F32)
    col = lax.broadcasted_iota(jnp.int32, (tm, tm), 1).astype(F32)
    perm = jnp.where(row == dest, 1.0, 0.0).astype(BF16)
    dest_col = jnp.broadcast_to(dest, (LANES, tm)).T
    dest_col = jnp.concatenate([dest_col] * (tm // LANES), axis=1)
    perm_t = jnp.where(dest_col == col, 1.0, 0.0).astype(BF16)
    pt_scr[...] = perm_t
    hs_scr[0:tm, :] = _bdot(perm, hb).astype(BF16)
    comb_hi = comb_t.astype(BF16)
    comb_lo = (comb_t - comb_hi.astype(F32)).astype(BF16)
    cs = _bdot(jnp.concatenate([comb_hi, comb_lo], axis=0), perm_t)
    cs = cs[0:N_EXPERTS] + cs[N_EXPERTS:2 * N_EXPERTS]
    cs_scr[0:tm, :] = jnp.concatenate(
        [cs, jnp.zeros((LANES - N_EXPERTS, tm), F32)], axis=0).T
    ys_scr[0:tm, :] = jnp.zeros((tm, D), F32)
    any_split = jnp.max(jnp.where(split, 1.0, 0.0)) > 0.0
    active = jnp.logical_and(s >= 1, s <= n_tiles)
    _moe_windows(active, start, cnt, any_split, wg_ref, wu_ref, wd_ref, hs_scr, cs_scr, ys_scr)


def _moe_weight_spec(layer, shape):
    return pl.BlockSpec((None,) + shape, lambda t: (layer,) + (0,) * len(shape),
                        pipeline_mode=pl.Buffered(1))


def _moe_scratch(tm):
    return [pltpu.VMEM((tm + MOE_WIN_BIG, D), BF16),
            pltpu.VMEM((tm + MOE_WIN_BIG, LANES), F32),
            pltpu.VMEM((tm + MOE_WIN_BIG, D), F32),
            pltpu.VMEM((tm, tm), BF16)]


def _conv_moe(x, mods, w_in, conv_w, w_out, ln0, rw, rbias, wg, wu, wd, ln1):
    batch, seq, _ = x.shape
    tm = TOKEN_TILE
    per_batch = seq // tm
    n_tiles = batch * per_batch
    const2 = lambda t: (0, 0)
    tile = lambda lag: pl.BlockSpec(
        (None, tm, D), lambda t: (jnp.clip(t - lag, 0, n_tiles - 1) // per_batch,
                                  jnp.clip(t - lag, 0, n_tiles - 1) % per_batch, 0))
    tri = (jnp.arange(tm)[:, None] < jnp.arange(tm)[None, :]).astype(BF16)
    return pl.pallas_call(
        functools.partial(_conv_moe_kernel, per_batch),
        grid=(n_tiles + 2,),
        in_specs=[tile(0),
                  pl.BlockSpec((batch, SUBLANES, D), lambda t: (0, 0, 0)),
                  pl.BlockSpec((D, 3 * D), const2),
                  pl.BlockSpec((SUBLANES, D), const2),
                  pl.BlockSpec((D, D), const2),
                  pl.BlockSpec((2, D), const2),
                  pl.BlockSpec((N_EXPERTS, D), const2),
                  pl.BlockSpec((N_EXPERTS, 1), const2),
                  pl.BlockSpec((tm, tm), const2),
                  _moe_weight_spec(0, (N_EXPERTS, D, EXPERT_FF)),
                  _moe_weight_spec(0, (N_EXPERTS, D, EXPERT_FF)),
                  _moe_weight_spec(0, (N_EXPERTS * EXPERT_FF, D)),
                  pl.BlockSpec((2, D), const2)],
        out_specs=tile(2),
        out_shape=jax.ShapeDtypeStruct(x.shape, F32),
        scratch_shapes=[pltpu.VMEM((tm + SUBLANES, D), F32),
                        pltpu.VMEM((2, tm, D), F32)] + _moe_scratch(tm),
        compiler_params=pltpu.CompilerParams(
            dimension_semantics=("arbitrary",), vmem_limit_bytes=FUSED_VMEM_LIMIT),
        name="conv_moe",
    )(x, mods, w_in, conv_w, w_out, ln0, rw, rbias, tri, wg, wu, wd, ln1)


def _moe(layer, x, mod, rw, rbias, wg, wu, wd, ln):
    batch, seq, _ = x.shape
    tm = TOKEN_TILE
    per_batch = seq // tm
    n_tiles = batch * per_batch
    const2 = lambda t: (0, 0)
    cur = lambda t: jnp.minimum(t, n_tiles - 1)
    prev = lambda t: jnp.maximum(t - 1, 0)
    tile = lambda which: pl.BlockSpec(
        (None, tm, D), lambda t: (which(t) // per_batch, which(t) % per_batch, 0))
    mods = lambda which: pl.BlockSpec(
        (None, SUBLANES, D), lambda t: (which(t) // per_batch, 0, 0))
    tri = (jnp.arange(tm)[:, None] < jnp.arange(tm)[None, :]).astype(BF16)
    return pl.pallas_call(
        _moe_kernel,
        grid=(n_tiles + 1,),
        in_specs=[tile(cur), tile(prev), mods(cur), mods(prev),
                  pl.BlockSpec((N_EXPERTS, D), const2),
                  pl.BlockSpec((N_EXPERTS, 1), const2),
                  pl.BlockSpec((tm, tm), const2),
                  _moe_weight_spec(layer, (N_EXPERTS, D, EXPERT_FF)),
                  _moe_weight_spec(layer, (N_EXPERTS, D, EXPERT_FF)),
                  _moe_weight_spec(layer, (N_EXPERTS * EXPERT_FF, D)),
                  pl.BlockSpec((2, D), const2)],
        out_specs=tile(prev),
        out_shape=jax.ShapeDtypeStruct(x.shape, F32),
        scratch_shapes=_moe_scratch(tm),
        compiler_params=_cparams(1),
        name="moe",
    )(x, x, mod, mod, rw, rbias, tri, wg, wu, wd, ln)


def _qkv_kernel(x_ref, mod_ref, kvmod_ref, wq_ref, wkv_ref, *refs):
    outs, (slab, hq_scr, hkv_scr) = refs[:3 * N_GROUPS], refs[3 * N_GROUPS:]
    tm = x_ref.shape[0]
    mod = mod_ref[...]
    kvmod = kvmod_ref[...]
    q_sc, q_sh = 1.0 + mod[1:2], mod[0:1]
    kv_sc, kv_sh = 1.0 + kvmod[1:2], kvmod[0:1]
    n_slabs = D // LANES
    for j in range(n_slabs):
        slab[j] = x_ref[:, j * LANES:(j + 1) * LANES]
    scale = HEAD_DIM ** -0.5 * LOG2E
    for g, (_, dil) in enumerate(DIL_GROUPS):
        n = tm // dil
        for r in range(dil):
            rows = slice(r * n, (r + 1) * n)
            for j in range(n_slabs):
                ls = slice(j * LANES, (j + 1) * LANES)
                piece = slab[j] if dil == 1 else slab[j, pl.ds(r, n, stride=dil), :]
                hq_scr[rows, ls] = (piece * q_sc[:, ls] + q_sh[:, ls]).astype(BF16)
                hkv_scr[rows, ls] = (piece * kv_sc[:, ls] + kv_sh[:, ls]).astype(BF16)
        hq = hq_scr[...]
        hkv = hkv_scr[...]
        cols = slice(g * D, (g + 1) * D)
        vcols = slice(QW + g * D, QW + (g + 1) * D)
        q = (_bdot(hq, wq_ref[:, cols]) * scale).astype(BF16)
        k = _bdot(hkv, wkv_ref[:, cols]).astype(BF16)
        v = _bdot(hkv, wkv_ref[:, vcols]).astype(BF16)
        for r in range(dil):
            rows = slice(r * n, (r + 1) * n)
            outs[3 * g][r] = q[rows]
            outs[3 * g + 1][r] = k[rows]
            outs[3 * g + 2][r] = v[rows]


def _qkv(x, mod, kvmod, wq, wkv):
    batch, seq, _ = x.shape
    tm = TOKEN_TILE
    const = lambda b, i: (0, 0)
    out_specs, out_shape = [], []
    for _, dil in DIL_GROUPS:
        for _ in range(3):
            out_specs.append(pl.BlockSpec((None, dil, tm // dil, D), lambda b, i: (b, 0, i, 0)))
            out_shape.append(jax.ShapeDtypeStruct((batch, dil, seq // dil, D), BF16))
    return pl.pallas_call(
        _qkv_kernel,
        grid=(batch, seq // tm),
        in_specs=[pl.BlockSpec((None, tm, D), lambda b, i: (b, i, 0)),
                  pl.BlockSpec((None, SUBLANES, D), lambda b, i: (b, 0, 0)),
                  pl.BlockSpec((None, SUBLANES, D), lambda b, i: (b, 0, 0)),
                  pl.BlockSpec((D, QW), const),
                  pl.BlockSpec((D, 2 * QW), const)],
        out_specs=out_specs,
        out_shape=out_shape,
        scratch_shapes=[pltpu.VMEM((D // LANES, tm, LANES), F32),
                        pltpu.VMEM((tm, D), BF16),
                        pltpu.VMEM((tm, D), BF16)],
        compiler_params=_cparams(2),
        name="qkv_proj",
    )(x, mod, kvmod, wq, wkv)


def _bucket_tables():
    qi = jnp.arange(BLOCK)[:, None]
    kj = jnp.arange(2 * BLOCK)[None, :]
    dist = BLOCK + qi - kj
    max_exact = NUM_BUCKETS // 2
    tables = []
    for window, dil in DIL_GROUPS:
        n = jnp.maximum(dist, 0) * dil
        nf = jnp.maximum(n, 1).astype(F32)
        large = max_exact + (jnp.log(nf / max_exact) / math.log(MAX_DISTANCE / max_exact)
                             * (NUM_BUCKETS - max_exact)).astype(jnp.int32)
        large = jnp.minimum(large, NUM_BUCKETS - 1)
        bucket = jnp.where(n < max_exact, n, large)
        valid = (dist >= 0) & (dist <= window // dil)
        tables.append(jnp.where(valid, bucket, -1).astype(jnp.int32))
    return jnp.stack(tables)


def _bias_kernel(rb_ref, bkt_ref, o_ref):
    g = pl.program_id(0)
    bkt = bkt_ref[...]
    for h in range(HEADS):
        acc = jnp.full(bkt.shape, NEG_INF, F32)
        for b in range(NUM_BUCKETS):
            acc = jnp.where(bkt == b, rb_ref[b, g * HEADS + h] * LOG2E, acc)
        o_ref[h] = acc


def _bias_tiles(rel_bias):
    return pl.pallas_call(
        _bias_kernel,
        grid=(N_GROUPS,),
        in_specs=[pl.BlockSpec(memory_space=pltpu.SMEM),
                  pl.BlockSpec((None, BLOCK, 2 * BLOCK), lambda g: (g, 0, 0))],
        out_specs=pl.BlockSpec((None, HEADS, BLOCK, 2 * BLOCK), lambda g: (g, 0, 0, 0)),
        out_shape=jax.ShapeDtypeStruct((N_GROUPS, HEADS, BLOCK, 2 * BLOCK), F32),
        compiler_params=_cparams(1),
        name="rel_bias_tiles",
    )(rel_bias, _bucket_tables())


def _attn_kernel(q_ref, kp_ref, kc_ref, vp_ref, vc_ref, bias_ref, o_ref, stat_ref):
    n = pl.program_id(2)
    col = lax.broadcasted_iota(jnp.int32, (BLOCK, 2 * BLOCK), 1)
    no_prev = jnp.logical_and(n == 0, col < BLOCK)
    lane = lax.broadcasted_iota(jnp.int32, (BLOCK, LANES), 1)
    ones = jnp.ones((2 * BLOCK, HEAD_DIM), BF16)
    for r in range(q_ref.shape[0]):
        for j in range(q_ref.shape[1] // BLOCK):
            rows = slice(j * BLOCK, (j + 1) * BLOCK)
            window = slice((j - 1) * BLOCK, (j + 1) * BLOCK)
            stat = jnp.zeros((BLOCK, LANES), F32)
            for h in range(HEADS):
                hs = slice(h * HEAD_DIM, (h + 1) * HEAD_DIM)
                if j == 0:
                    k = jnp.concatenate([kp_ref[r, :, hs], kc_ref[r, rows, hs]], axis=0)
                    v = jnp.concatenate([vp_ref[r, :, hs], vc_ref[r, rows, hs]], axis=0)
                else:
                    k = kc_ref[r, window, hs]
                    v = vc_ref[r, window, hs]
                s = lax.dot_general(q_ref[r, rows, hs], k, (((1,), (1,)), ((), ())),
                                    preferred_element_type=F32) + bias_ref[h]
                if j == 0:
                    s = jnp.where(no_prev, NEG_INF, s)
                m = jnp.max(s, axis=-1, keepdims=True)
                p = jnp.exp2(s - m)
                o_ext = _bdot(p.astype(BF16), jnp.concatenate([v, ones], axis=1))
                o, l = o_ext[:, :HEAD_DIM], o_ext[:, HEAD_DIM:]
                o_ref[r, rows, hs] = o.astype(BF16)
                stat = jnp.where(lane == h, m, jnp.where(lane == HEADS + h, l, stat))
            stat_ref[r, rows, :] = stat


def _attention_group(g, q, k, v, bias):
    batch, dil, sub_len, _ = q.shape
    nq = min(ATTN_QBLOCKS, sub_len // BLOCK)
    n_sub = min(dil, ATTN_QBLOCKS // nq)
    assert sub_len % (nq * BLOCK) == 0 and dil % n_sub == 0
    cur = lambda b, r, n: (b, r, n, 0)
    if sub_len == nq * BLOCK:
        prev = lambda b, r, n: (0, 0, 0, 0)
    else:
        prev = lambda b, r, n: (b, r, jnp.maximum(n * nq - 1, 0), 0)
    blk = (None, n_sub, nq * BLOCK, D)
    blk_prev = (None, n_sub, BLOCK, D)
    return pl.pallas_call(
        _attn_kernel,
        grid=(batch, dil // n_sub, sub_len // (nq * BLOCK)),
        in_specs=[
            pl.BlockSpec(blk, cur),
            pl.BlockSpec(blk_prev, prev),
            pl.BlockSpec(blk, cur),
            pl.BlockSpec(blk_prev, prev),
            pl.BlockSpec(blk, cur),
            pl.BlockSpec((None, HEADS, BLOCK, 2 * BLOCK), lambda b, r, n: (g, 0, 0, 0)),
        ],
        out_specs=[pl.BlockSpec(blk, cur),
                   pl.BlockSpec((None, n_sub, nq * BLOCK, LANES), cur)],
        out_shape=[jax.ShapeDtypeStruct((batch, dil, sub_len, D), BF16),
                   jax.ShapeDtypeStruct((batch, dil, sub_len, LANES), F32)],
        compiler_params=_cparams(3),
        name=f"dilated_attn_g{g}",
    )(q, k, k, v, v, bias)


def _mix_kernel(o0_ref, o1_ref, o2_ref, s0_ref, s1_ref, s2_ref, x_ref, mod_ref,
                expand_ref, wo_ref, ln_ref, out_ref, stat_scr, o_slab):
    tm = x_ref.shape[0]
    n_slabs = D // LANES
    stats = []
    for g, s_ref in enumerate((s0_ref, s1_ref, s2_ref)):
        dil = DIL_GROUPS[g][1]
        if dil == 1:
            stats.append(s_ref[0])
            continue
        for r in range(dil):
            stat_scr[g, pl.ds(r, tm // dil, stride=dil), :] = s_ref[r]
        stats.append(stat_scr[g])
    sums = [pltpu.roll(s, LANES - HEADS, axis=1) for s in stats]
    top = jnp.maximum(stats[0], jnp.maximum(stats[1], stats[2]))
    es = [jnp.exp2(s - top) for s in stats]
    den = es[0] * sums[0] + es[1] * sums[1] + es[2] * sums[2]
    head_lane = lax.broadcasted_iota(jnp.int32, top.shape, 1) < HEADS
    expand = expand_ref[...]
    mixed = None
    for g, o_ref in enumerate((o0_ref, o1_ref, o2_ref)):
        dil = DIL_GROUPS[g][1]
        w = jnp.where(head_lane, es[g] / den, 0.0).astype(BF16)
        w = _bdot(w, expand)
        if dil == 1:
            o = o_ref[0].astype(F32)
        else:
            for r in range(dil):
                for j in range(n_slabs):
                    o_slab[j, pl.ds(r, tm // dil, stride=dil), :] = (
                        o_ref[r, :, j * LANES:(j + 1) * LANES].astype(F32))
            o = jnp.concatenate([o_slab[j] for j in range(n_slabs)], axis=1)
        mixed = w * o if mixed is None else mixed + w * o
    y = _bdot(mixed.astype(BF16), wo_ref[...])
    g1 = mod_ref[...][2:3]
    ln = ln_ref[...]
    out_ref[...] = _layer_norm(ALPHA * x_ref[...] + (1.0 + g1) * y, ln[0:1], ln[1:2])


def _mix(outs, stats, x, mod, wo, ln):
    batch, seq, _ = x.shape
    tm = WIDE_TOKEN_TILE
    const = lambda b, i: (0, 0)
    tok = pl.BlockSpec((None, tm, D), lambda b, i: (b, i, 0))
    dilated = lambda dil, w: pl.BlockSpec((None, dil, tm // dil, w), lambda b, i: (b, 0, i, 0))
    dils = [dil for _, dil in DIL_GROUPS]
    head_of_lane = jnp.arange(D)[None, :] // HEAD_DIM
    expand = (jnp.arange(LANES)[:, None] == head_of_lane).astype(BF16)
    return pl.pallas_call(
        _mix_kernel,
        grid=(batch, seq // tm),
        in_specs=[dilated(d, D) for d in dils] + [dilated(d, LANES) for d in dils] + [
            tok,
            pl.BlockSpec((None, SUBLANES, D), lambda b, i: (b, 0, 0)),
            pl.BlockSpec((LANES, D), const),
            pl.BlockSpec((D, D), const),
            pl.BlockSpec((2, D), const)],
        out_specs=tok,
        out_shape=jax.ShapeDtypeStruct(x.shape, F32),
        scratch_shapes=[pltpu.VMEM((N_GROUPS, tm, LANES), F32),
                        pltpu.VMEM((D // LANES, tm, LANES), F32)],
        compiler_params=_cparams(2),
        name="attn_mix",
    )(*outs, *stats, x, mod, expand, wo, ln)


def kernel(x, c, ada_w, ada_b, ln_g, ln_b, conv_w_in, conv_w, conv_w_out, kv_ada_w,
           kv_ada_b, w_kv, attn_w_q, attn_w_o, rel_bias, router_w, router_bias,
           moe_w_gate, moe_w_up, moe_w_down):
    batch, seq, _ = x.shape
    assert x.shape[2] == D and batch <= SUBLANES
    assert seq % WIDE_TOKEN_TILE == 0 and seq % (BLOCK * DIL_GROUPS[-1][1]) == 0

    c8 = jnp.pad(c, ((0, SUBLANES - batch), (0, 0)))
    mods = _ada_linear(c8, ada_w, ada_b)
    kvmods = _ada_linear(c8, kv_ada_w[None], kv_ada_b[None])[0]
    mod0 = _mod_rows(mods[0], batch, 6)
    mod1 = _mod_rows(mods[1], batch, 6)
    kvmod = _mod_rows(kvmods, batch, 2)
    ln = jnp.stack([ln_g, ln_b], axis=2)

    rw = router_w.T.astype(BF16)
    rbias = router_bias.reshape(N_EXPERTS, 1)
    experts = (moe_w_gate.astype(BF16), moe_w_up.astype(BF16),
               moe_w_down.astype(BF16).reshape(DEPTH, N_EXPERTS * EXPERT_FF, D))
    conv_w8 = jnp.pad(conv_w[0], ((0, SUBLANES - CONV_WIDTH), (0, 0)))

    x = _conv_moe(x, mod0, conv_w_in[0].astype(BF16), conv_w8, conv_w_out[0].astype(BF16),
                  ln[0, 0], rw, rbias, *experts, ln[0, 1])

    qkv = _qkv(x, mod1, kvmod, attn_w_q[0].astype(BF16), w_kv.astype(BF16))
    bias = _bias_tiles(rel_bias)
    outs, stats = zip(*[_attention_group(g, *qkv[3 * g:3 * g + 3], bias)
                       for g in range(N_GROUPS)])
    x = _mix(outs, stats, x, mod1, attn_w_o[0].astype(BF16), ln[1, 0])
    x = _moe(1, x, mod1, rw, rbias, *experts, ln[1, 1])
    return x
```
